```python
import math
import jax, jax.numpy as jnp
from jax import lax
import numpy as np

D_MODEL = 1024
BATCH = 1
SEQ = 16384
DEPTH = 1
DEC_BATCH = 32
DEC_SEQ = 4
PAST_LEN = 16384
PAGE_SIZE = 128

DA_HEADS = 8
DA_HD = 64
DA_W = DA_HEADS * 2 * DA_HD
NSA_HEADS = 16
NSA_KV_HEADS = 2
NSA_HD = 64
NSA_HPG = NSA_HEADS // NSA_KV_HEADS
NSA_W = NSA_HEADS * NSA_HD
NSA_KV_W = NSA_KV_HEADS * NSA_HD
CMP_LEN = 32
CMP_STRIDE = 16
CMP_HIDDEN = 2 * NSA_HD
SEL_BLOCK = 64
SEL_TOP = 16
WINDOW = 512
MEM_LEN = 256
X_HEADS = 4
X_HD = 64
X_W = X_HEADS * X_HD
FFN_HIDDEN = -(-8 * D_MODEL // (3 * 256)) * 256

Q_BLOCK = 128
EPS = 1e-6
NEG_INF = -1e30
FORCE_BONUS = 1e6
IN_SIZES = (DA_W, DA_W, DA_W, NSA_W, NSA_KV_W, NSA_KV_W, NSA_KV_W, NSA_KV_W, NSA_KV_W, NSA_KV_W, 3 * NSA_HEADS, 2 * D_MODEL)
IN_TOTAL = sum(IN_SIZES)

kernel_name = 'diff_nsa_hybrid_decode_step'


def rms_norm(x, g):
    xf = x.astype(jnp.float32)
    y = xf * lax.rsqrt(jnp.mean(xf * xf, axis=-1, keepdims=True) + EPS)
    return (y * g.astype(jnp.float32)).astype(x.dtype)


def alibi_slopes(n_heads):
    return jnp.asarray(2.0 ** (-8.0 * np.arange(1, n_heads + 1) / n_heads), dtype=jnp.float32)


def masked_softmax(s, mask):
    p = jax.nn.softmax(jnp.where(mask, s, NEG_INF), axis=-1)
    return jnp.where(mask, p, 0.0)


def gather_pages(pool, layer, page_table):
    g = pool[layer, page_table]
    return g.reshape(g.shape[0], g.shape[1] * g.shape[2], *g.shape[3:])


def mixer_inputs(xn, w_in):
    B, T, _ = xn.shape
    z = xn @ w_in
    dq, dk, dv, nq, kc, vc, ks, vs, kw, vw, ng, mg = jnp.split(z, np.cumsum(IN_SIZES)[:-1].tolist(), axis=-1)
    r = lambda a, h: a.reshape(B, T, h, -1)
    return (r(dq, DA_HEADS), r(dk, DA_HEADS), r(dv, DA_HEADS), r(nq, NSA_HEADS),
            r(kc, NSA_KV_HEADS), r(vc, NSA_KV_HEADS), r(ks, NSA_KV_HEADS), r(vs, NSA_KV_HEADS),
            r(kw, NSA_KV_HEADS), r(vw, NSA_KV_HEADS), ng.reshape(B, T, NSA_HEADS, 3), mg)


def diff_lambda(lq1, lk1, lq2, lk2, lam_init):
    f = jnp.float32
    return (jnp.exp(jnp.sum(lq1.astype(f) * lk1.astype(f)))
            - jnp.exp(jnp.sum(lq2.astype(f) * lk2.astype(f))) + lam_init)


def diff_attn_core(q, k, v, q_pos, k_pos, lam, slopes):
    f32 = jnp.float32
    B, Tq, H, _ = q.shape
    Tk = k.shape[1]
    q2 = q.astype(f32).reshape(B, Tq, H, 2, DA_HD) * (DA_HD ** -0.5)
    k2 = k.astype(f32).reshape(B, Tk, H, 2, DA_HD)
    dist = (q_pos[:, None] - k_pos[None, :]).astype(f32)
    s = jnp.einsum('bqhmd,bkhmd->mbhqk', q2, k2) - slopes[:, None, None] * dist
    p = masked_softmax(s, dist >= 0)
    a = p[0] - lam * p[1]
    return jnp.einsum('bhqk,bkhe->bqhe', a, v.astype(f32))


def diff_attn_prompt(q, k, v, lam, slopes):
    B, S = q.shape[:2]
    nb = S // Q_BLOCK
    k_pos = jnp.arange(S, dtype=jnp.int32)
    qb = q.reshape(B, nb, Q_BLOCK, DA_HEADS, 2 * DA_HD).swapaxes(0, 1)
    starts = jnp.arange(nb, dtype=jnp.int32) * Q_BLOCK

    def block(args):
        qi, start = args
        return diff_attn_core(qi, k, v, start + jnp.arange(Q_BLOCK, dtype=jnp.int32), k_pos, lam, slopes)

    o = lax.map(block, (qb, starts))
    return o.swapaxes(0, 1).reshape(B, S, DA_HEADS, 2 * DA_HD)


def diff_post(o, subln, lam_init, dtype):
    B, T = o.shape[:2]
    return (rms_norm(o, subln) * (1.0 - lam_init)).reshape(B, T, DA_W).astype(dtype)


def compress(rows, w1, pe, w2):
    B, T, G, hd = rows.shape
    nch = -(-T // CMP_STRIDE)
    rows = jnp.pad(rows, ((0, 0), (0, nch * CMP_STRIDE - T), (0, 0), (0, 0)))
    ch = rows.reshape(B, nch, CMP_STRIDE, G, hd)
    w1r = w1.reshape(CMP_LEN // CMP_STRIDE, CMP_STRIDE, hd, CMP_HIDDEN)
    h_lo = jnp.einsum('bcsgd,sdh->bcgh', ch, w1r[0])
    h_hi = jnp.einsum('bcsgd,sdh->bcgh', ch, w1r[1])
    h_pe = jnp.einsum('ld,ldh->h', pe, w1.reshape(CMP_LEN, hd, CMP_HIDDEN))
    h = h_lo[:, :-1] + h_hi[:, 1:] + h_pe
    blocks = jnp.einsum('bngh,hd->bngd', jax.nn.silu(h), w2)
    blk_end = jnp.arange(nch - 1, dtype=jnp.int32) * CMP_STRIDE + (CMP_LEN - 1)
    return blocks, blk_end


def sel_blocks(rows):
    B, T, G, hd = rows.shape
    nsel = -(-T // SEL_BLOCK)
    rows = jnp.pad(rows, ((0, 0), (0, nsel * SEL_BLOCK - T), (0, 0), (0, 0)))
    return rows.reshape(B, nsel, SEL_BLOCK, G, hd).transpose(0, 3, 1, 2, 4)


def sel_agg_matrix(nblk, nsel):
    m = np.zeros((nblk, nsel), np.float32)
    j = np.arange(nsel)
    r, c = SEL_BLOCK // CMP_STRIDE, CMP_LEN // CMP_STRIDE
    for a in range(r):
        for b in range(c):
            i = r * j + a - b
            ok = (i >= 0) & (i < nblk)
            np.add.at(m, (i[ok], j[ok]), 1.0)
    return jnp.asarray(m)


def nsa_context(kc_rows, vc_rows, ks_rows, vs_rows, cmp_k, cmp_v):
    kc, blk_end = compress(kc_rows, *cmp_k)
    vc, _ = compress(vc_rows, *cmp_v)
    ks_t, vs_t = sel_blocks(ks_rows), sel_blocks(vs_rows)
    agg = sel_agg_matrix(kc.shape[1], ks_t.shape[2])
    return kc, vc, blk_end, agg, ks_t, vs_t


def nsa_core(q, q_pos, kc, vc, blk_end, agg, ks_t, vs_t, kw, vw, w_pos, gates, slopes):
    f32 = jnp.float32
    B, Tq = q.shape[:2]
    G = NSA_KV_HEADS
    qg = q.astype(f32).reshape(B, Tq, G, NSA_HPG, NSA_HD) * (NSA_HD ** -0.5)
    m = slopes.reshape(G, NSA_HPG, 1, 1)
    dist_c = (q_pos[:, None] - blk_end[None, :]).astype(f32)
    s_c = jnp.einsum('bqghd,bngd->bghqn', qg, kc.astype(f32)) - m * dist_c
    p_c = masked_softmax(s_c, dist_c >= 0)
    o_c = jnp.einsum('bghqn,bngd->bqghd', p_c, vc.astype(f32))
    nsel = ks_t.shape[2]
    imp = jnp.einsum('bghqn,nj->bgqj', p_c, agg)
    cur = q_pos // SEL_BLOCK
    j = jnp.arange(nsel, dtype=jnp.int32)
    valid = j[None, :] <= cur[:, None]
    forced = valid & ((j[None, :] == 0) | (j[None, :] >= cur[:, None] - 1))
    score = jnp.where(valid, imp + jnp.where(forced, FORCE_BONUS, 0.0), NEG_INF)
    _, idx = lax.top_k(score, min(SEL_TOP, nsel))
    sel_ok = idx <= cur[None, None, :, None]
    bi = jnp.arange(B)[:, None, None, None]
    gi = jnp.arange(G)[None, :, None, None]
    kg = ks_t[bi, gi, idx].astype(f32)
    vg = vs_t[bi, gi, idx].astype(f32)
    pos_s = idx[..., None] * SEL_BLOCK + jnp.arange(SEL_BLOCK, dtype=jnp.int32)
    dist_s = (q_pos[None, None, :, None, None] - pos_s).astype(f32)
    mask_s = (dist_s >= 0) & sel_ok[..., None]
    s_s = (jnp.einsum('bqghd,bgqnsd->bghqns', qg, kg)
           - m.reshape(1, G, NSA_HPG, 1, 1, 1) * dist_s[:, :, None])
    n_sel, sb = kg.shape[3], kg.shape[4]
    p_s = masked_softmax(s_s.reshape(B, G, NSA_HPG, Tq, n_sel * sb),
                         mask_s.reshape(B, G, 1, Tq, n_sel * sb)).reshape(B, G, NSA_HPG, Tq, n_sel, sb)
    o_s = jnp.einsum('bghqns,bgqnsd->bqghd', p_s, vg)
    dist_w = (q_pos[:, None] - w_pos[None, :]).astype(f32)
    mask_w = (dist_w >= 0) & (dist_w < WINDOW) & (w_pos[None, :] >= 0)
    s_w = jnp.einsum('bqghd,bkgd->bghqk', qg, kw.astype(f32)) - m * dist_w
    p_w = masked_softmax(s_w, mask_w)
    o_w = jnp.einsum('bghqk,bkgd->bqghd', p_w, vw.astype(f32))
    g = jax.nn.sigmoid(gates.astype(f32)).reshape(B, Tq, G, NSA_HPG, 3)
    o = g[..., 0:1] * o_c + g[..., 1:2] * o_s + g[..., 2:3] * o_w
    return o.reshape(B, Tq, NSA_W)


def nsa_prompt(nq, ng, kw_rows, vw_rows, ctx, slopes):
    B, S = nq.shape[:2]
    nb = S // Q_BLOCK
    pad = ((0, 0), (WINDOW, 0), (0, 0), (0, 0))
    kw_pad, vw_pad = jnp.pad(kw_rows, pad), jnp.pad(vw_rows, pad)

    def block(args):
        qi, gi, start = args
        qp = start + jnp.arange(Q_BLOCK, dtype=jnp.int32)
        kwi = lax.dynamic_slice_in_dim(kw_pad, start, WINDOW + Q_BLOCK, axis=1)
        vwi = lax.dynamic_slice_in_dim(vw_pad, start, WINDOW + Q_BLOCK, axis=1)
        wp = start - WINDOW + jnp.arange(WINDOW + Q_BLOCK, dtype=jnp.int32)
        return nsa_core(qi, qp, *ctx, kwi, vwi, wp, gi, slopes)

    qb = nq.reshape(B, nb, Q_BLOCK, NSA_HEADS, NSA_HD).swapaxes(0, 1)
    gb = ng.reshape(B, nb, Q_BLOCK, NSA_HEADS, 3).swapaxes(0, 1)
    starts = jnp.arange(nb, dtype=jnp.int32) * Q_BLOCK
    o = lax.map(block, (qb, gb, starts))
    return o.swapaxes(0, 1).reshape(B, S, NSA_W)


def memory_kv(mem, w_mem_kv):
    B, M, _ = mem.shape
    kv = (mem @ w_mem_kv).reshape(B, M, 2, X_HEADS, X_HD)
    return kv[:, :, 0], kv[:, :, 1]


def cross_attn(xn, mk, mv, w_xq, w_xo):
    f32 = jnp.float32
    B, T, _ = xn.shape
    q = (xn @ w_xq).reshape(B, T, X_HEADS, X_HD).astype(f32) * (X_HD ** -0.5)
    p = jax.nn.softmax(jnp.einsum('bqhd,bmhd->bhqm', q, mk.astype(f32)), axis=-1)
    o = jnp.einsum('bhqm,bmhd->bqhd', p, mv.astype(f32)).reshape(B, T, X_W).astype(xn.dtype)
    return o @ w_xo


def swiglu(x, w_gate_up, w_down):
    g, u = jnp.split(x @ w_gate_up, 2, axis=-1)
    return (jax.nn.silu(g) * u) @ w_down


def finish_layer(h, o_da, o_nsa, mg, mk, mv, lp):
    w_o, norm_x, w_xq, w_xo, norm_ffn, w_gate_up, w_down = lp[5:]
    ga, gb = jnp.split(mg, 2, axis=-1)
    h = h + (jax.nn.sigmoid(ga) * o_da + jax.nn.sigmoid(gb) * o_nsa) @ w_o
    h = h + cross_attn(rms_norm(h, norm_x), mk, mv, w_xq, w_xo)
    return h + swiglu(rms_norm(h, norm_ffn), w_gate_up, w_down)


def layer_prompt(h, mem, w_mem_kv, lp, lam, lam_init, da_slopes, nsa_slopes):
    norm_mix, w_in, da_subln, cmp_k, cmp_v = lp[:5]
    S = h.shape[1]
    xn = rms_norm(h, norm_mix)
    dq, dk, dv, nq, kc, vc, ks, vs, kw, vw, ng, mg = mixer_inputs(xn, w_in)
    o_da = diff_post(diff_attn_prompt(dq, dk, dv, lam, da_slopes), da_subln, lam_init, xn.dtype)
    ctx = nsa_context(kc, vc, ks, vs, cmp_k, cmp_v)
    o_nsa = nsa_prompt(nq, ng, kw, vw, ctx, nsa_slopes).astype(xn.dtype)
    mk, mv = memory_kv(mem, w_mem_kv)
    h = finish_layer(h, o_da, o_nsa, mg, mk, mv, lp)
    wb = min(WINDOW, S)
    return h, (dk, dv, kc, vc, ks, vs, kw[:, S - wb:], vw[:, S - wb:], mk, mv)


def layer_sample(h, past_rows, win_buf, mem_kv, lp, lam, lam_init, da_slopes, nsa_slopes):
    norm_mix, w_in, da_subln, cmp_k, cmp_v = lp[:5]
    p_dk, p_dv, p_ck, p_cv, p_sk, p_sv = past_rows
    c_wk, c_wv = win_buf
    mk, mv = mem_kv
    DS = h.shape[1]
    past = p_dk.shape[1]
    q_pos = past + jnp.arange(DS, dtype=jnp.int32)
    k_pos = jnp.arange(past + DS, dtype=jnp.int32)
    xn = rms_norm(h, norm_mix)
    dq, dk, dv, nq, kc, vc, ks, vs, kw, vw, ng, mg = mixer_inputs(xn, w_in)
    cat = lambda a, b: jnp.concatenate([a, b], axis=1)
    o_da = diff_post(diff_attn_core(dq, cat(p_dk, dk), cat(p_dv, dv), q_pos, k_pos, lam, da_slopes),
                     da_subln, lam_init, xn.dtype)
    ctx = nsa_context(cat(p_ck, kc), cat(p_cv, vc), cat(p_sk, ks), cat(p_sv, vs), cmp_k, cmp_v)
    wb = c_wk.shape[1]
    kw_all, vw_all = cat(c_wk, kw), cat(c_wv, vw)
    w_pos = past - wb + jnp.arange(wb + DS, dtype=jnp.int32)
    o_nsa = nsa_core(nq, q_pos, *ctx, kw_all, vw_all, w_pos, ng, nsa_slopes).astype(xn.dtype)
    h = finish_layer(h, o_da, o_nsa, mg, mk, mv, lp)
    return h, (dk, dv, kc, vc, ks, vs, kw_all[:, DS:], vw_all[:, DS:])


def setup_inputs(seed: int = 0) -> dict:
    key = jax.random.key(seed)
    keys = iter(list(jax.random.split(key, 40)))
    nrm = lambda shape, scale: jax.random.normal(next(keys), shape, jnp.float32) * scale
    gain = lambda shape: 1.0 + 0.02 * jax.random.normal(next(keys), shape, jnp.float32)
    n_pages = PAST_LEN // PAGE_SIZE
    n_used = DEC_BATCH * n_pages
    n_pool = n_used + (n_used + 3) // 4
    wb = min(WINDOW, PAST_LEN)
    page_table = jax.random.permutation(next(keys), n_pool)[:n_used].reshape(DEC_BATCH, n_pages).astype(jnp.int32)
    pool_da = (DEPTH, n_pool, PAGE_SIZE, DA_HEADS, 2 * DA_HD)
    pool_nsa = (DEPTH, n_pool, PAGE_SIZE, NSA_KV_HEADS, NSA_HD)
    win = (DEPTH, DEC_BATCH, wb, NSA_KV_HEADS, NSA_HD)
    memc = (DEPTH, DEC_BATCH, MEM_LEN, X_HEADS, X_HD)
    return {
        'x_prompt': nrm((BATCH, SEQ, D_MODEL), 1.0),
        'x_sample': nrm((DEC_BATCH, DEC_SEQ, D_MODEL), 1.0),
        'cache_diff_k': nrm(pool_da, 1.0),
        'cache_diff_v': nrm(pool_da, 1.0),
        'cache_cmp_k': nrm(pool_nsa, 1.0),
        'cache_cmp_v': nrm(pool_nsa, 1.0),
        'cache_sel_k': nrm(pool_nsa, 1.0),
        'cache_sel_v': nrm(pool_nsa, 1.0),
        'cache_win_k': nrm(win, 1.0),
        'cache_win_v': nrm(win, 1.0),
        'cache_mem_k': nrm(memc, 1.0),
        'cache_mem_v': nrm(memc, 1.0),
        'page_table': page_table,
        'mem_prompt': nrm((BATCH, MEM_LEN, D_MODEL), 1.0),
        'norm_mix': gain((DEPTH, D_MODEL)),
        'w_in': nrm((DEPTH, D_MODEL, IN_TOTAL), D_MODEL ** -0.5),
        'lam_q1': nrm((DEPTH, DA_HD), 0.1),
        'lam_k1': nrm((DEPTH, DA_HD), 0.1),
        'lam_q2': nrm((DEPTH, DA_HD), 0.1),
        'lam_k2': nrm((DEPTH, DA_HD), 0.1),
        'da_subln': gain((DEPTH, 2 * DA_HD)),
        'w_cmp_k1': nrm((DEPTH, CMP_LEN * NSA_HD, CMP_HIDDEN), (CMP_LEN * NSA_HD) ** -0.5),
        'pe_cmp_k': nrm((DEPTH, CMP_LEN, NSA_HD), 0.1),
        'w_cmp_k2': nrm((DEPTH, CMP_HIDDEN, NSA_HD), CMP_HIDDEN ** -0.5),
        'w_cmp_v1': nrm((DEPTH, CMP_LEN * NSA_HD, CMP_HIDDEN), (CMP_LEN * NSA_HD) ** -0.5),
        'pe_cmp_v': nrm((DEPTH, CMP_LEN, NSA_HD), 0.1),
        'w_cmp_v2': nrm((DEPTH, CMP_HIDDEN, NSA_HD), CMP_HIDDEN ** -0.5),
        'w_o': nrm((DEPTH, D_MODEL, D_MODEL), D_MODEL ** -0.5),
        'norm_x': gain((DEPTH, D_MODEL)),
        'w_xq': nrm((DEPTH, D_MODEL, X_W), D_MODEL ** -0.5),
        'w_mem_kv': nrm((DEPTH, D_MODEL, 2 * X_W), D_MODEL ** -0.5),
        'w_xo': nrm((DEPTH, X_W, D_MODEL), X_W ** -0.5),
        'norm_ffn': gain((DEPTH, D_MODEL)),
        'w_gate_up': nrm((DEPTH, D_MODEL, 2 * FFN_HIDDEN), D_MODEL ** -0.5),
        'w_down': nrm((DEPTH, FFN_HIDDEN, D_MODEL), FFN_HIDDEN ** -0.5),
        'norm_final': gain((D_MODEL,)),
    }


def reference(x_prompt, x_sample, cache_diff_k, cache_diff_v, cache_cmp_k, cache_cmp_v, cache_sel_k, cache_sel_v,
              cache_win_k, cache_win_v, cache_mem_k, cache_mem_v, page_table, mem_prompt,
              norm_mix, w_in, lam_q1, lam_k1, lam_q2, lam_k2, da_subln,
              w_cmp_k1, pe_cmp_k, w_cmp_k2, w_cmp_v1, pe_cmp_v, w_cmp_v2,
              w_o, norm_x, w_xq, w_mem_kv, w_xo, norm_ffn, w_gate_up, w_down, norm_final):
    da_slopes = alibi_slopes(DA_HEADS)
    nsa_slopes = alibi_slopes(NSA_HEADS)
    hp, hs = x_prompt, x_sample
    p_states, s_states = [], []
    for l in range(DEPTH):
        lam_init = 0.8 - 0.6 * math.exp(-0.3 * l)
        lam = diff_lambda(lam_q1[l], lam_k1[l], lam_q2[l], lam_k2[l], lam_init)
        lp = (norm_mix[l], w_in[l], da_subln[l],
              (w_cmp_k1[l], pe_cmp_k[l], w_cmp_k2[l]), (w_cmp_v1[l], pe_cmp_v[l], w_cmp_v2[l]),
              w_o[l], norm_x[l], w_xq[l], w_xo[l], norm_ffn[l], w_gate_up[l], w_down[l])
        hp, ps = layer_prompt(hp, mem_prompt, w_mem_kv[l], lp, lam, lam_init, da_slopes, nsa_slopes)
        past_rows = (gather_pages(cache_diff_k, l, page_table), gather_pages(cache_diff_v, l, page_table),
                     gather_pages(cache_cmp_k, l, page_table), gather_pages(cache_cmp_v, l, page_table),
                     gather_pages(cache_sel_k, l, page_table), gather_pages(cache_sel_v, l, page_table))
        hs, ss = layer_sample(hs, past_rows, (cache_win_k[l], cache_win_v[l]), (cache_mem_k[l], cache_mem_v[l]),
                              lp, lam, lam_init, da_slopes, nsa_slopes)
        p_states.append(ps)
        s_states.append(ss)
    y_prompt = rms_norm(hp, norm_final)
    y_sample = rms_norm(hs, norm_final)
    p_dk, p_dv, p_ck, p_cv, p_sk, p_sv, p_wk, p_wv, p_mk, p_mv = [jnp.stack(a) for a in zip(*p_states)]
    s_dk, s_dv, s_ck, s_cv, s_sk, s_sv, s_wk, s_wv = [jnp.stack(a) for a in zip(*s_states)]
    return (y_prompt, y_sample, p_dk, p_dv, p_ck, p_cv, p_sk, p_sv, p_wk, p_wv, p_mk, p_mv,
            s_dk, s_dv, s_ck, s_cv, s_sk, s_sv, s_wk, s_wv)
```

```python
import functools

import numpy as np
import jax
import jax.numpy as jnp
from jax import lax
from jax.experimental import pallas as pl
from jax.experimental.pallas import tpu as pltpu

F32 = jnp.float32
BF16 = jnp.bfloat16

D_MODEL = 1024
DA_HEADS = 8
DA_HD = 64
NSA_HEADS = 16
NSA_GROUPS = 2
NSA_HPG = NSA_HEADS // NSA_GROUPS
NSA_HD = 64
CMP_LEN = 32
CMP_STRIDE = 16
SEL_BLOCK = 64
SEL_TOP = 16
WINDOW = 512
X_HEADS = 4
X_HD = 64
X_W = X_HEADS * X_HD
EPS = 1e-6
NEG = -1e30
PICKED = -3e38
FORCE_BONUS = 1e6
LAM_INIT = 0.2
LANE = 128
PAGE = 128
VMEM_LIMIT = 56 * 1024 * 1024

C_DQ, C_DK, C_DV, C_NQ, C_GA, C_GB = 0, 1024, 2048, 3072, 4096, 5120
C_KC, C_VC, C_KS, C_VS, C_KW, C_VW, C_NG = 6144, 6272, 6400, 6528, 6656, 6784, 6912
ZP = 7040
ZP_TILE = 1408

NT_DIMS = (((1,), (1,)), ((), ()))


def _cparams(sem):
    return pltpu.CompilerParams(dimension_semantics=sem, vmem_limit_bytes=VMEM_LIMIT)


def _smem_spec():
    return pl.BlockSpec(memory_space=pltpu.SMEM)


def _sigmoid(x):
    return 1.0 / (1.0 + jnp.exp(-x))


def _rms(x, g):
    return x * lax.rsqrt(jnp.mean(x * x, axis=-1, keepdims=True) + EPS) * g


def _inproj_kernel(x_ref, g_ref, w_ref, o_ref, ob_ref, xn_ref):
    @pl.when(pl.program_id(1) == 0)
    def _():
        xn_ref[...] = _rms(x_ref[...], g_ref[...]).astype(BF16)

    acc = jnp.dot(xn_ref[...], w_ref[...], preferred_element_type=F32)
    o_ref[...] = acc
    ob_ref[...] = acc.astype(BF16)


def _inproj(x, g, w_pad):
    m = x.shape[0]
    tm = min(m, 512)
    return pl.pallas_call(
        _inproj_kernel,
        grid=(m // tm, ZP // ZP_TILE),
        in_specs=[pl.BlockSpec((tm, D_MODEL), lambda i, j: (i, 0)),
                  pl.BlockSpec((1, D_MODEL), lambda i, j: (0, 0)),
                  pl.BlockSpec((D_MODEL, ZP_TILE), lambda i, j: (0, j))],
        out_specs=[pl.BlockSpec((tm, ZP_TILE), lambda i, j: (i, j)),
                   pl.BlockSpec((tm, ZP_TILE), lambda i, j: (i, j))],
        out_shape=[jax.ShapeDtypeStruct((m, ZP), F32), jax.ShapeDtypeStruct((m, ZP), BF16)],
        scratch_shapes=[pltpu.VMEM((tm, D_MODEL), BF16)],
        compiler_params=_cparams(("parallel", "arbitrary")),
        name="inproj",
    )(x, g.reshape(1, D_MODEL), w_pad)


def _prep_w_in(w_in):
    a = w_in[:, :4096]
    kv = w_in[:, 4096:4864]
    ng = w_in[:, 4864:4912]
    mg = w_in[:, 4912:]
    return jnp.concatenate([a, mg, kv, jnp.pad(ng, ((0, 0), (0, LANE - ng.shape[1])))], axis=1).astype(BF16)


def _matmul_kernel(x_ref, w_ref, o_ref):
    o_ref[...] = jnp.dot(x_ref[...].astype(BF16), w_ref[...], preferred_element_type=F32)


def _matmul(x, w_bf16):
    m, n = x.shape[0], w_bf16.shape[1]
    return pl.pallas_call(
        _matmul_kernel,
        out_shape=jax.ShapeDtypeStruct((m, n), F32),
        compiler_params=pltpu.CompilerParams(vmem_limit_bytes=VMEM_LIMIT),
        name="matmul",
    )(x, w_bf16)


def _da_post(o0, o1, lam, sub):
    o = o0 - lam * o1
    return _rms(o, sub) * (1.0 - LAM_INIT)


def _da_prompt_kernel(sc_ref, q_ref, k_ref, v_ref, sub_ref, o_ref, l_ref, lm_ref, m_ref, s_ref, acc_ref, *, tq):
    h = pl.program_id(0)
    qi = pl.program_id(1)
    lam = sc_ref[0]
    slope = sc_ref[1 + h]

    @pl.when(qi == 0)
    def _():
        i = lax.broadcasted_iota(jnp.int32, (2 * tq, tq), 0) & (tq - 1)
        j = lax.broadcasted_iota(jnp.int32, (2 * tq, tq), 1)
        loc = slope * (j - i).astype(F32)
        l_ref[...] = loc
        lm_ref[...] = jnp.where(j <= i, loc, NEG)

    q = q_ref[...] * (DA_HD ** -0.5)
    lane = lax.broadcasted_iota(jnp.int32, (tq, 2 * DA_HD), 1)
    zero = jnp.zeros_like(q)
    qbd = jnp.concatenate([jnp.where(lane < DA_HD, q, zero), jnp.where(lane >= DA_HD, q, zero)], axis=0)
    m_ref[...] = jnp.full(m_ref.shape, NEG, F32)
    s_ref[...] = jnp.zeros(s_ref.shape, F32)
    acc_ref[...] = jnp.zeros(acc_ref.shape, F32)

    def step(kc, bias_ref):
        off = pl.multiple_of(kc * tq, tq)
        k = k_ref[pl.ds(off, tq), :]
        v = v_ref[pl.ds(off, tq), :]
        s = lax.dot_general(qbd, k, NT_DIMS, preferred_element_type=F32)
        cc = slope * ((qi - kc) * tq).astype(F32)
        t = s + bias_ref[...]
        m_old = m_ref[...]
        m_new = jnp.maximum(m_old, jnp.max(t, axis=-1, keepdims=True) - cc)
        alpha = jnp.exp(m_old - m_new)
        p = jnp.exp(t - (m_new + cc))
        s_ref[...] = alpha * s_ref[...] + jnp.sum(p, axis=-1, keepdims=True)
        acc_ref[...] = alpha * acc_ref[...] + jnp.dot(p.astype(BF16), v, preferred_element_type=F32)
        m_ref[...] = m_new

    def body(kc, carry):
        step(kc, l_ref)
        return carry

    lax.fori_loop(0, qi, body, 0)
    step(qi, lm_ref)

    inv = 1.0 / s_ref[...]
    o0 = acc_ref[0:tq, :] * inv[0:tq]
    o1 = acc_ref[tq:2 * tq, :] * inv[tq:2 * tq]
    o_ref[...] = _da_post(o0, o1, lam, sub_ref[...])


def _da_prompt(zb, sc, subln, tq=256):
    s = zb.shape[0]
    tq = min(tq, s)
    kb, vb = C_DK // LANE, C_DV // LANE
    return pl.pallas_call(
        functools.partial(_da_prompt_kernel, tq=tq),
        grid=(DA_HEADS, s // tq),
        in_specs=[_smem_spec(),
                  pl.BlockSpec((tq, LANE), lambda h, i: (i, h)),
                  pl.BlockSpec((s, LANE), lambda h, i: (0, kb + h)),
                  pl.BlockSpec((s, LANE), lambda h, i: (0, vb + h)),
                  pl.BlockSpec((1, LANE), lambda h, i: (0, 0))],
        out_specs=pl.BlockSpec((tq, LANE), lambda h, i: (i, h)),
        out_shape=jax.ShapeDtypeStruct((s, DA_HEADS * LANE), F32),
        scratch_shapes=[pltpu.VMEM((2 * tq, tq), F32), pltpu.VMEM((2 * tq, tq), F32),
                        pltpu.VMEM((2 * tq, 1), F32), pltpu.VMEM((2 * tq, 1), F32),
                        pltpu.VMEM((2 * tq, LANE), F32)],
        compiler_params=_cparams(("arbitrary", "arbitrary")),
        name="da_prompt",
    )(sc, zb, zb, zb, subln.reshape(1, LANE))


def _compress_kernel(pt_ref, *refs, pp, nch):
    del pt_ref
    pages = refs[:pp]
    tail_ref, wbig_ref, w1_ref, pe_ref, w2_ref, out_ref, ab_ref = refs[pp:]
    s = pl.program_id(1)
    x = jnp.concatenate([p[0] for p in pages], axis=0).astype(BF16)
    rows = 8 * pp
    ab_ref[pl.ds(pl.multiple_of(s * rows, rows), rows), :] = jnp.dot(x, wbig_ref[...], preferred_element_type=F32)

    @pl.when(s == pl.num_programs(1) - 1)
    def _():
        ab_ref[nch:nch + 8, :] = jnp.dot(tail_ref[0].astype(BF16), wbig_ref[...], preferred_element_type=F32)
        hpe = jnp.dot(pe_ref[...].astype(BF16), w1_ref[...], preferred_element_type=F32)[0:1]
        outs = []
        hid = 2 * NSA_HD
        for g in range(NSA_GROUPS):
            a = ab_ref[0:nch, 2 * hid * g:2 * hid * g + hid]
            b = ab_ref[1:nch + 1, 2 * hid * g + hid:2 * hid * (g + 1)]
            hd = a + b + hpe
            act = hd * _sigmoid(hd)
            outs.append(jnp.dot(act.astype(BF16), w2_ref[...], preferred_element_type=F32))
        out_ref[0] = jnp.concatenate(outs, axis=-1)


def _compress(pool, page_table, tail, w1, pe, w2):
    b, n_pages = page_table.shape
    nch = n_pages * 8
    pp = min(16, n_pages)
    ck = CMP_STRIDE * NSA_GROUPS * NSA_HD
    hid = 2 * NSA_HD
    w1r = w1.reshape(2, CMP_STRIDE, NSA_HD, hid)
    wbig = jnp.einsum('psdh,gk->sgdkph', w1r, jnp.eye(NSA_GROUPS, dtype=F32)).reshape(ck, 2 * NSA_GROUPS * hid).astype(BF16)
    pe8 = jnp.pad(pe.reshape(1, CMP_LEN * NSA_HD), ((0, 7), (0, 0)))

    def page_spec(k):
        return pl.BlockSpec((1, 8, ck), lambda bi, s, pt: (pt[bi * n_pages + s * pp + k], 0, 0))

    const = lambda shape: pl.BlockSpec(shape, lambda bi, s, pt: tuple(0 for _ in shape))
    grid_spec = pltpu.PrefetchScalarGridSpec(
        num_scalar_prefetch=1,
        grid=(b, n_pages // pp),
        in_specs=[page_spec(k) for k in range(pp)] + [
            pl.BlockSpec((1, 8, ck), lambda bi, s, pt: (bi, 0, 0)),
            const((ck, 2 * NSA_GROUPS * hid)), const((CMP_LEN * NSA_HD, hid)), const((8, CMP_LEN * NSA_HD)),
            const((hid, NSA_HD))],
        out_specs=pl.BlockSpec((1, nch, NSA_GROUPS * NSA_HD), lambda bi, s, pt: (bi, 0, 0)),
        scratch_shapes=[pltpu.VMEM((nch + 8, 2 * NSA_GROUPS * hid), F32)],
    )
    return pl.pallas_call(
        functools.partial(_compress_kernel, pp=pp, nch=nch),
        grid_spec=grid_spec,
        out_shape=jax.ShapeDtypeStruct((b, nch, NSA_GROUPS * NSA_HD), F32),
        compiler_params=_cparams(("arbitrary", "arbitrary")),
        name="compress",
    )(page_table.reshape(-1), *([pool] * pp), tail, wbig, w1.astype(BF16), pe8, w2.astype(BF16))


def _stack_group_queries(q, g, tq):
    q = q * (NSA_HD ** -0.5)
    lane = lax.broadcasted_iota(jnp.int32, (tq, 2 * NSA_HD), 1)
    mine = jnp.where(lane >= NSA_HD, 1, 0) == g
    parts = []
    for hh in range(NSA_HPG):
        qh = q[:, NSA_HD * hh:NSA_HD * (hh + 1)]
        parts.append(jnp.where(mine, jnp.concatenate([qh, qh], axis=1), jnp.zeros((tq, 2 * NSA_HD), q.dtype)))
    return jnp.concatenate(parts, axis=0)


def _group_half(x, g):
    return jnp.where(g == 0, x[:, :NSA_HD], x[:, NSA_HD:])


def _cmp_topk_kernel(sl_ref, q_ref, kc_ref, vc_ref, agg_ref, oc_ref, sel_ref, any_ref, *, tq, nb, nselp, qpos_base, topk):
    g = pl.program_id(1)
    t = pl.program_id(2)
    q0 = qpos_base + t * tq
    qpad = _stack_group_queries(q_ref[...], g, tq)
    kcb = kc_ref[0].astype(BF16)
    vcb = vc_ref[0].astype(BF16)
    s_all = lax.dot_general(qpad, kcb, NT_DIMS, preferred_element_type=F32)
    i = lax.broadcasted_iota(jnp.int32, (tq, nb), 0)
    n = lax.broadcasted_iota(jnp.int32, (tq, nb), 1)
    dist = (q0 + i - (CMP_STRIDE * n + (CMP_LEN - 1))).astype(F32)
    mask = dist >= 0
    psum = jnp.zeros((tq, nb), F32)
    for hh in range(NSA_HPG):
        slope = sl_ref[NSA_HPG * g + hh]
        tt = jnp.where(mask, s_all[hh * tq:(hh + 1) * tq] - slope * dist, NEG)
        m = jnp.max(tt, axis=-1, keepdims=True)
        e = jnp.where(mask, jnp.exp(tt - m), 0.0)
        l = jnp.sum(e, axis=-1, keepdims=True)
        p = e / jnp.where(l == 0.0, 1.0, l)
        psum = psum + p
        o = jnp.dot(p.astype(BF16), vcb, preferred_element_type=F32)
        oc_ref[:, NSA_HD * hh:NSA_HD * (hh + 1)] = _group_half(o, g)

    p_hi = psum.astype(BF16)
    p_lo = (psum - p_hi.astype(F32)).astype(BF16)
    agg = agg_ref[...]
    imp = jnp.dot(p_hi, agg, preferred_element_type=F32) + jnp.dot(p_lo, agg, preferred_element_type=F32)
    jj = lax.broadcasted_iota(jnp.int32, (tq, nselp), 1)
    cur = (q0 + lax.broadcasted_iota(jnp.int32, (tq, nselp), 0)) // SEL_BLOCK
    valid = jj <= cur
    forced = jnp.where(valid, jnp.where(jj == 0, 1, jnp.where(jj >= cur - 1, 1, 0)), 0)
    score = jnp.where(valid, imp + jnp.where(forced == 1, FORCE_BONUS, 0.0), NEG)

    def pick(_, carry):
        score, sel = carry
        mx = jnp.max(score, axis=-1, keepdims=True)
        idx = jnp.min(jnp.where(score == mx, jj, nselp), axis=-1, keepdims=True)
        hit = jj == idx
        return jnp.where(hit, PICKED, score), jnp.where(hit, 1.0, sel)

    _, sel = lax.fori_loop(0, topk, pick, (score, jnp.zeros((tq, nselp), F32)))
    sel = jnp.where(valid, sel, 0.0)
    sel_ref[0, 0] = sel
    any_ref[0, 0, 0] = jnp.broadcast_to(jnp.max(sel, axis=0, keepdims=True), (8, nselp))


def _sel_agg_matrix(nblk, nsel, nb, nselp):
    m = np.zeros((nb, nselp), np.float32)
    j = np.arange(nsel)
    r, c = SEL_BLOCK // CMP_STRIDE, CMP_LEN // CMP_STRIDE
    for a in range(r):
        for b in range(c):
            i = r * j + a - b
            ok = (i >= 0) & (i < nblk)
            np.add.at(m, (i[ok], j[ok]), 1.0)
    return jnp.asarray(m, dtype=BF16)


def _cmp_topk(qarr, qcol0, kcb, vcb, slopes, *, batch, sq, tq, nblk, nsel, qpos_base):
    nb = kcb.shape[1]
    nselp = -(-nsel // LANE) * LANE
    nt = sq // tq
    agg = _sel_agg_matrix(nblk, nsel, nb, nselp)
    gw = NSA_HPG * NSA_HD
    return pl.pallas_call(
        functools.partial(_cmp_topk_kernel, tq=tq, nb=nb, nselp=nselp, qpos_base=qpos_base, topk=min(SEL_TOP, nsel)),
        grid=(batch, NSA_GROUPS, nt),
        in_specs=[_smem_spec(),
                  pl.BlockSpec((tq, gw), lambda b, g, t: (b * nt + t, qcol0 + g)),
                  pl.BlockSpec((1, nb, LANE), lambda b, g, t: (b, 0, 0)),
                  pl.BlockSpec((1, nb, LANE), lambda b, g, t: (b, 0, 0)),
                  pl.BlockSpec((nb, nselp), lambda b, g, t: (0, 0))],
        out_specs=[pl.BlockSpec((tq, gw), lambda b, g, t: (b * nt + t, g)),
                   pl.BlockSpec((1, 1, tq, nselp), lambda b, g, t: (b, g, t, 0)),
                   pl.BlockSpec((1, 1, 1, 8, nselp), lambda b, g, t: (b, g, t, 0, 0))],
        out_shape=[jax.ShapeDtypeStruct((batch * sq, NSA_GROUPS * gw), F32),
                   jax.ShapeDtypeStruct((batch, NSA_GROUPS, sq, nselp), F32),
                   jax.ShapeDtypeStruct((batch, NSA_GROUPS, nt, 8, nselp), F32)],
        compiler_params=_cparams(("arbitrary", "arbitrary", "arbitrary")),
        name="cmp_topk",
    )(slopes, qarr, kcb, vcb, agg)


def _nsa_sw_kernel(fl_ref, sl_ref, q_ref, ks_ref, vs_ref, kw_ref, vw_ref, sel_ref, oc_ref, ng_ref, out_ref,
                   m_ref, s_ref, acc_ref, os_ref, *, tq, nt, nselp, nwords):
    g = pl.program_id(0)
    t = pl.program_id(1)
    rows = NSA_HPG * tq
    qpad = _stack_group_queries(q_ref[...], g, tq)
    slope_row = jnp.concatenate([jnp.full((tq, 1), sl_ref[NSA_HPG * g + hh], F32) for hh in range(NSA_HPG)], axis=0)
    i_loc = lax.broadcasted_iota(jnp.int32, (rows, LANE), 0) & (tq - 1)
    j_loc = lax.broadcasted_iota(jnp.int32, (rows, LANE), 1)
    loc = slope_row * (j_loc - i_loc).astype(F32)
    causal = j_loc <= i_loc
    selb = sel_ref[0, 0].astype(BF16)

    def reset():
        m_ref[...] = jnp.full(m_ref.shape, NEG, F32)
        s_ref[...] = jnp.zeros(s_ref.shape, F32)
        acc_ref[...] = jnp.zeros(acc_ref.shape, F32)

    def attend(k_ref, v_ref, c, mask):
        off = pl.multiple_of(c * LANE, LANE)
        k = k_ref[pl.ds(off, LANE), :]
        v = v_ref[pl.ds(off, LANE), :]
        s = lax.dot_general(qpad, k, NT_DIMS, preferred_element_type=F32)
        cc = slope_row * ((t - c) * tq).astype(F32)
        tt = s + loc - cc
        if mask is not None:
            tt = jnp.where(mask, tt, NEG)
        m_old = m_ref[...]
        m_new = jnp.maximum(m_old, jnp.max(tt, axis=-1, keepdims=True))
        alpha = jnp.exp(m_old - m_new)
        p = jnp.exp(tt - m_new)
        if mask is not None:
            p = jnp.where(mask, p, 0.0)
        s_ref[...] = alpha * s_ref[...] + jnp.sum(p, axis=-1, keepdims=True)
        acc_ref[...] = alpha * acc_ref[...] + jnp.dot(p.astype(BF16), v, preferred_element_type=F32)
        m_ref[...] = m_new

    def result():
        l = s_ref[...]
        return acc_ref[...] / jnp.where(l == 0.0, 1.0, l)

    def sel_mask(c):
        jrow = lax.broadcasted_iota(jnp.int32, (nselp, LANE), 0)
        r = lax.broadcasted_iota(jnp.int32, (nselp, LANE), 1)
        expand = jnp.where(jrow == 2 * c + r // SEL_BLOCK, 1.0, 0.0).astype(BF16)
        mk = jnp.dot(selb, expand, preferred_element_type=F32)
        return jnp.concatenate([mk] * NSA_HPG, axis=0)

    reset()

    def sel_body(c, carry):
        word = fl_ref[(g * nt + t) * nwords + c // 32]
        bit = lax.shift_right_logical(word, c % 32) & 1

        @pl.when(bit == 1)
        def _():
            attend(ks_ref, vs_ref, c, sel_mask(c) > 0.5)

        return carry

    lax.fori_loop(0, t, sel_body, 0)
    attend(ks_ref, vs_ref, t, jnp.where(causal, sel_mask(t), 0.0) > 0.5)
    os_ref[...] = result()

    reset()
    nback = WINDOW // tq
    for kk in range(nback + 1):
        c = t - nback + kk

        @pl.when(c >= 0)
        def _():
            cs = jnp.maximum(c, 0)
            if kk == 0:
                attend(kw_ref, vw_ref, cs, j_loc > i_loc)
            elif kk == nback:
                attend(kw_ref, vw_ref, cs, causal)
            else:
                attend(kw_ref, vw_ref, cs, None)

    o_w = result()
    o_s = os_ref[...]
    gate = _sigmoid(ng_ref[...])
    glane = lax.broadcasted_iota(jnp.int32, (tq, LANE), 1)

    def gate_col(idx):
        return jnp.sum(jnp.where(glane == idx, gate, 0.0), axis=-1, keepdims=True)

    for hh in range(NSA_HPG):
        base = 3 * (NSA_HPG * g + hh)
        r0, r1 = hh * tq, (hh + 1) * tq
        o = (gate_col(base) * oc_ref[:, NSA_HD * hh:NSA_HD * (hh + 1)]
             + gate_col(base + 1) * _group_half(o_s[r0:r1], g)
             + gate_col(base + 2) * _group_half(o_w[r0:r1], g))
        out_ref[:, NSA_HD * hh:NSA_HD * (hh + 1)] = o


def _pack_chunk_flags(anyblk):
    g, nt, nselp = anyblk.shape
    chunk = jnp.max(anyblk.reshape(g, nt, nselp // 2, 2), axis=-1) > 0.5
    nchunk = nselp // 2
    nwords = -(-nchunk // 32)
    chunk = jnp.pad(chunk, ((0, 0), (0, 0), (0, nwords * 32 - nchunk)))
    bits = chunk.reshape(g, nt, nwords, 32).astype(jnp.uint32) << jnp.arange(32, dtype=jnp.uint32)
    words = jnp.sum(bits, axis=-1, dtype=jnp.uint32)
    return lax.bitcast_convert_type(words, jnp.int32).reshape(-1), nwords


def _nsa_sw_prompt(z, zb, sel, anyblk, oc, slopes, tq=128):
    s = zb.shape[0]
    nt = s // tq
    nselp = sel.shape[-1]
    flags, nwords = _pack_chunk_flags(anyblk[0, :, :, 0, :])
    gw = NSA_HPG * NSA_HD
    rows = NSA_HPG * tq
    res = lambda col: pl.BlockSpec((s, LANE), lambda g, t, fl: (0, col // LANE), pipeline_mode=pl.Buffered(1))
    grid_spec = pltpu.PrefetchScalarGridSpec(
        num_scalar_prefetch=1,
        grid=(NSA_GROUPS, nt),
        in_specs=[_smem_spec(),
                  pl.BlockSpec((tq, gw), lambda g, t, fl: (t, C_NQ // gw + g)),
                  res(C_KS), res(C_VS), res(C_KW), res(C_VW),
                  pl.BlockSpec((1, 1, tq, nselp), lambda g, t, fl: (0, g, t, 0)),
                  pl.BlockSpec((tq, gw), lambda g, t, fl: (t, g)),
                  pl.BlockSpec((tq, LANE), lambda g, t, fl: (t, C_NG // LANE))],
        out_specs=pl.BlockSpec((tq, gw), lambda g, t, fl: (t, g)),
        scratch_shapes=[pltpu.VMEM((rows, 1), F32), pltpu.VMEM((rows, 1), F32),
                        pltpu.VMEM((rows, LANE), F32), pltpu.VMEM((rows, LANE), F32)],
    )
    return pl.pallas_call(
        functools.partial(_nsa_sw_kernel, tq=tq, nt=nt, nselp=nselp, nwords=nwords),
        grid_spec=grid_spec,
        out_shape=jax.ShapeDtypeStruct((s, NSA_GROUPS * gw), F32),
        compiler_params=_cparams(("arbitrary", "arbitrary")),
        name="nsa_sel_win",
    )(flags, slopes, zb, zb, zb, zb, zb, sel, oc, z)


def _decode_kernel(pt_ref, *refs, pp, ncols, width, pos0, qpos0, win, n_new, has_sel, nselp):
    del pt_ref
    kpages = refs[:pp]
    vpages = refs[pp:2 * pp]
    rest = refs[2 * pp:]
    if has_sel:
        wq_ref, ci_ref, kn_ref, vn_ref, sel_ref, o_ref, m_ref, s_ref, acc_ref = rest
    else:
        wq_ref, ci_ref, kn_ref, vn_ref, o_ref, m_ref, s_ref, acc_ref = rest
        sel_ref = None
    st = pl.program_id(1)

    @pl.when(st == 0)
    def _():
        m_ref[...] = jnp.full(m_ref.shape, NEG, F32)
        s_ref[...] = jnp.zeros(s_ref.shape, F32)
        acc_ref[...] = jnp.zeros(acc_ref.shape, F32)

    wq = wq_ref[0]
    slope = ci_ref[:, 0:1]
    qrel = ci_ref[:, 1:2]
    r = lax.broadcasted_iota(jnp.int32, (ncols, PAGE), 1)
    rf = r.astype(F32)

    def attend(k, v, kpos0, row_ok):
        s = lax.dot_general(wq, k.astype(BF16), NT_DIMS, preferred_element_type=F32)
        dist = (qrel + (qpos0 - kpos0).astype(F32)) - rf
        mask = jnp.where(dist >= 0.0, jnp.where(dist < win, 1.0, 0.0), 0.0)
        if row_ok is not None:
            mask = jnp.where(row_ok, mask, 0.0)
        if has_sel:
            jrow = lax.broadcasted_iota(jnp.int32, (nselp, PAGE), 0)
            kp = kpos0 + lax.broadcasted_iota(jnp.int32, (nselp, PAGE), 1)
            expand = jnp.where(jrow == kp // SEL_BLOCK, 1.0, 0.0).astype(BF16)
            mask = mask * jnp.dot(sel_ref[0].astype(BF16), expand, preferred_element_type=F32)
        ok = mask > 0.5
        tt = jnp.where(ok, s - slope * dist, NEG)
        m_old = m_ref[...]
        m_new = jnp.maximum(m_old, jnp.max(tt, axis=-1, keepdims=True))
        alpha = jnp.exp(m_old - m_new)
        p = jnp.where(ok, jnp.exp(tt - m_new), 0.0)
        s_ref[...] = alpha * s_ref[...] + jnp.sum(p, axis=-1, keepdims=True)
        acc_ref[...] = alpha * acc_ref[...] + jnp.dot(p.astype(BF16), v.astype(BF16), preferred_element_type=F32)
        m_ref[...] = m_new

    for k in range(pp):
        attend(kpages[k][0], vpages[k][0], pos0 + (st * pp + k) * PAGE, None)

    @pl.when(st == pl.num_programs(1) - 1)
    def _():
        attend(kn_ref[0], vn_ref[0], jnp.int32(qpos0), r < n_new)
        l = s_ref[...]
        o_ref[0] = acc_ref[...] / jnp.where(l == 0.0, 1.0, l)


def _decode_attn(wq, colinfo, pool_k, pool_v, page_table, knew, vnew, sel, *, pos0, qpos0, win, n_new, pp):
    b, n_pages = page_table.shape
    ncols, width = wq.shape[1], wq.shape[2]
    pp = min(pp, n_pages)
    has_sel = sel is not None
    nselp = sel.shape[-1] if has_sel else 0

    def page_spec(k):
        return pl.BlockSpec((1, PAGE, width), lambda bi, s, pt: (pt[bi * n_pages + s * pp + k], 0, 0))

    per_batch = lambda shape: pl.BlockSpec((1,) + shape, lambda bi, s, pt: (bi, 0, 0))
    in_specs = ([page_spec(k) for k in range(pp)] * 2
                + [per_batch((ncols, width)), pl.BlockSpec((ncols, LANE), lambda bi, s, pt: (0, 0)),
                   per_batch((PAGE, width)), per_batch((PAGE, width))])
    args = [pool_k] * pp + [pool_v] * pp + [wq, colinfo, knew, vnew]
    if has_sel:
        in_specs.append(per_batch((ncols, nselp)))
        args.append(sel)
    grid_spec = pltpu.PrefetchScalarGridSpec(
        num_scalar_prefetch=1,
        grid=(b, n_pages // pp),
        in_specs=in_specs,
        out_specs=per_batch((ncols, width)),
        scratch_shapes=[pltpu.VMEM((ncols, 1), F32), pltpu.VMEM((ncols, 1), F32), pltpu.VMEM((ncols, width), F32)],
    )
    return pl.pallas_call(
        functools.partial(_decode_kernel, pp=pp, ncols=ncols, width=width, pos0=pos0, qpos0=qpos0, win=float(win),
                          n_new=n_new, has_sel=has_sel, nselp=nselp),
        grid_spec=grid_spec,
        out_shape=jax.ShapeDtypeStruct((b, ncols, width), F32),
        compiler_params=_cparams(("arbitrary", "arbitrary")),
        name="decode_attn",
    )(page_table.reshape(-1), *args)


def _da_post_kernel(sc_ref, o0_ref, o1_ref, sub_ref, o_ref):
    o_ref[...] = _da_post(o0_ref[...], o1_ref[...], sc_ref[0], sub_ref[...])


def _da_post_call(o0, o1, sc, subln):
    return pl.pallas_call(
        _da_post_kernel,
        in_specs=[_smem_spec(), pl.BlockSpec(o0.shape, lambda: (0, 0)), pl.BlockSpec(o0.shape, lambda: (0, 0)),
                  pl.BlockSpec((1, LANE), lambda: (0, 0))],
        out_specs=pl.BlockSpec(o0.shape, lambda: (0, 0)),
        out_shape=jax.ShapeDtypeStruct(o0.shape, F32),
        name="da_post",
    )(sc, o0, o1, subln.reshape(1, LANE))


def _gate3_kernel(ng_ref, e_ref, oc_ref, os_ref, ow_ref, o_ref):
    gate = _sigmoid(ng_ref[...])
    acc = jnp.zeros(o_ref.shape, F32)
    for br, ref in enumerate((oc_ref, os_ref, ow_ref)):
        ge = jnp.dot(gate, e_ref[br], preferred_element_type=F32, precision=lax.Precision.HIGHEST)
        acc = acc + ge * ref[...]
    o_ref[...] = acc


def _gate3(ng, oc, os_, ow):
    e = np.zeros((3, LANE, NSA_HEADS * NSA_HD), np.float32)
    for br in range(3):
        for h in range(NSA_HEADS):
            e[br, 3 * h + br, NSA_HD * h:NSA_HD * (h + 1)] = 1.0
    return pl.pallas_call(
        _gate3_kernel,
        out_shape=jax.ShapeDtypeStruct(oc.shape, F32),
        name="gate3",
    )(ng, jnp.asarray(e), oc, os_, ow)


def _tail_a_kernel(x_ref, da_ref, nsa_ref, ga_ref, gb_ref, wo_ref, nx_ref, wxq_ref, h_ref, q_ref):
    m = _sigmoid(ga_ref[...]) * da_ref[...] + _sigmoid(gb_ref[...]) * nsa_ref[...]
    h = x_ref[...] + jnp.dot(m.astype(BF16), wo_ref[...], preferred_element_type=F32)
    h_ref[...] = h
    xn = _rms(h, nx_ref[...]).astype(BF16)
    q_ref[...] = jnp.dot(xn, wxq_ref[...], preferred_element_type=F32)


def _tail_a(x, o_da, o_nsa, z, w_o, norm_x, w_xq):
    m = x.shape[0]
    tm = min(m, 512)
    row = lambda cb: pl.BlockSpec((tm, D_MODEL), lambda i: (i, cb))
    const = lambda shape: pl.BlockSpec(shape, lambda i: (0, 0), pipeline_mode=pl.Buffered(1))
    return pl.pallas_call(
        _tail_a_kernel,
        grid=(m // tm,),
        in_specs=[row(0), row(0), row(0), row(C_GA // D_MODEL), row(C_GB // D_MODEL),
                  const((D_MODEL, D_MODEL)), const((1, D_MODEL)), const((D_MODEL, X_W))],
        out_specs=[row(0), pl.BlockSpec((tm, X_W), lambda i: (i, 0))],
        out_shape=[jax.ShapeDtypeStruct((m, D_MODEL), F32), jax.ShapeDtypeStruct((m, X_W), F32)],
        compiler_params=_cparams(("parallel",)),
        name="tail_merge_wo",
    )(x, o_da, o_nsa, z, z, w_o.astype(BF16), norm_x.reshape(1, D_MODEL), w_xq.astype(BF16))


def _cross_kernel(q_ref, mk_ref, mv_ref, o_ref):
    q = (q_ref[0] * (X_HD ** -0.5)).astype(BF16)
    mk = mk_ref[0].astype(BF16)
    mv = mv_ref[0].astype(BF16)
    outs = []
    for h in range(X_HEADS):
        sl = slice(X_HD * h, X_HD * (h + 1))
        s = lax.dot_general(q[:, sl], mk[:, sl], NT_DIMS, preferred_element_type=F32)
        e = jnp.exp(s - jnp.max(s, axis=-1, keepdims=True))
        p = e / jnp.sum(e, axis=-1, keepdims=True)
        outs.append(jnp.dot(p.astype(BF16), mv[:, sl], preferred_element_type=F32))
    o_ref[0] = jnp.concatenate(outs, axis=-1)


def _cross(q, mk, mv):
    b, t, _ = q.shape
    mlen = mk.shape[1]
    tt = min(t, 512)
    return pl.pallas_call(
        _cross_kernel,
        grid=(b, t // tt),
        in_specs=[pl.BlockSpec((1, tt, X_W), lambda bi, i: (bi, i, 0)),
                  pl.BlockSpec((1, mlen, X_W), lambda bi, i: (bi, 0, 0)),
                  pl.BlockSpec((1, mlen, X_W), lambda bi, i: (bi, 0, 0))],
        out_specs=pl.BlockSpec((1, tt, X_W), lambda bi, i: (bi, i, 0)),
        out_shape=jax.ShapeDtypeStruct((b, t, X_W), F32),
        compiler_params=_cparams(("parallel", "parallel")),
        name="cross_attn",
    )(q, mk, mv)


def _tail_c_kernel(h_ref, ox_ref, wxo_ref, nf_ref, wg_ref, wu_ref, wd_ref, nfin_ref, y_ref, *, nchunk, chunk):
    h = h_ref[...] + jnp.dot(ox_ref[...].astype(BF16), wxo_ref[...], preferred_element_type=F32)
    xn = _rms(h, nf_ref[...]).astype(BF16)
    acc = jnp.zeros(h.shape, F32)
    for c in range(nchunk):
        sl = slice(c * chunk, (c + 1) * chunk)
        gt = jnp.dot(xn, wg_ref[:, sl], preferred_element_type=F32)
        up = jnp.dot(xn, wu_ref[:, sl], preferred_element_type=F32)
        act = (gt * _sigmoid(gt) * up).astype(BF16)
        acc = acc + jnp.dot(act, wd_ref[sl, :], preferred_element_type=F32)
    y_ref[...] = _rms(h + acc, nfin_ref[...])


def _tail_c(h, ox, w_xo, norm_ffn, w_gate_up, w_down, norm_final):
    m = h.shape[0]
    tm = min(m, 512)
    hid = w_down.shape[0]
    chunk = hid // 2
    const = lambda shape: pl.BlockSpec(shape, lambda i: (0, 0), pipeline_mode=pl.Buffered(1))
    return pl.pallas_call(
        functools.partial(_tail_c_kernel, nchunk=2, chunk=chunk),
        grid=(m // tm,),
        in_specs=[pl.BlockSpec((tm, D_MODEL), lambda i: (i, 0)), pl.BlockSpec((tm, X_W), lambda i: (i, 0)),
                  const((X_W, D_MODEL)), const((1, D_MODEL)), const((D_MODEL, hid)), const((D_MODEL, hid)),
                  const((hid, D_MODEL)), const((1, D_MODEL))],
        out_specs=pl.BlockSpec((tm, D_MODEL), lambda i: (i, 0)),
        out_shape=jax.ShapeDtypeStruct((m, D_MODEL), F32),
        compiler_params=_cparams(("parallel",)),
        name="tail_ffn",
    )(h, ox, w_xo.astype(BF16), norm_ffn.reshape(1, D_MODEL), w_gate_up[:, :hid].astype(BF16),
      w_gate_up[:, hid:].astype(BF16), w_down.astype(BF16), norm_final.reshape(1, D_MODEL))


def _alibi(n):
    return np.asarray(2.0 ** (-8.0 * np.arange(1, n + 1) / n), dtype=np.float32)


def _finish(x, o_da, o_nsa, z, mk, mv, batch, w_o, norm_x, w_xq, w_xo, norm_ffn, w_gate_up, w_down, norm_final):
    m = x.shape[0]
    t = m // batch
    h1, qx = _tail_a(x, o_da, o_nsa, z, w_o, norm_x, w_xq)
    q3 = qx.reshape(batch, t, X_W)
    tpad = -(-t // 8) * 8
    if tpad != t:
        q3 = jnp.pad(q3, ((0, 0), (0, tpad - t), (0, 0)))
    ox = _cross(q3, mk, mv)[:, :t].reshape(m, X_W)
    return _tail_c(h1, ox, w_xo, norm_ffn, w_gate_up, w_down, norm_final)


def kernel(x_prompt, x_sample, cache_diff_k, cache_diff_v, cache_cmp_k, cache_cmp_v, cache_sel_k, cache_sel_v,
           cache_win_k, cache_win_v, cache_mem_k, cache_mem_v, page_table, mem_prompt,
           norm_mix, w_in, lam_q1, lam_k1, lam_q2, lam_k2, da_subln,
           w_cmp_k1, pe_cmp_k, w_cmp_k2, w_cmp_v1, pe_cmp_v, w_cmp_v2,
           w_o, norm_x, w_xq, w_mem_kv, w_xo, norm_ffn, w_gate_up, w_down, norm_final):
    batch, seq, _ = x_prompt.shape
    db, ds, _ = x_sample.shape
    assert batch == 1 and norm_mix.shape[0] == 1
    n_pages = page_table.shape[1]
    past = n_pages * PAGE
    wb = cache_win_k.shape[2]
    kvw = NSA_GROUPS * NSA_HD

    lam = (jnp.exp(jnp.sum(lam_q1[0] * lam_k1[0])) - jnp.exp(jnp.sum(lam_q2[0] * lam_k2[0])) + LAM_INIT).astype(F32)
    da_sc = jnp.concatenate([lam.reshape(1), jnp.asarray(_alibi(DA_HEADS))])
    nsa_sl = jnp.asarray(_alibi(NSA_HEADS))
    w_pad = _prep_w_in(w_in[0])
    tail_w = (w_o[0], norm_x[0], w_xq[0], w_xo[0], norm_ffn[0], w_gate_up[0], w_down[0], norm_final)
    cmp_k = (w_cmp_k1[0], pe_cmp_k[0], w_cmp_k2[0])
    cmp_v = (w_cmp_v1[0], pe_cmp_v[0], w_cmp_v2[0])

    xp = x_prompt.reshape(seq, D_MODEL)
    z, zb = _inproj(xp, norm_mix[0], w_pad)
    o_da = _da_prompt(zb, da_sc, da_subln[0])

    ident = jnp.arange(seq // PAGE, dtype=jnp.int32).reshape(1, -1)
    zero_tail = jnp.zeros((1, 8, CMP_STRIDE * kvw), F32)
    p_kc = z[:, C_KC:C_KC + kvw]
    p_vc = z[:, C_VC:C_VC + kvw]
    kcb = _compress(p_kc.reshape(seq // PAGE, 8, CMP_STRIDE * kvw), ident, zero_tail, *cmp_k)
    vcb = _compress(p_vc.reshape(seq // PAGE, 8, CMP_STRIDE * kvw), ident, zero_tail, *cmp_v)
    nch = seq // CMP_STRIDE
    oc, sel, anyblk = _cmp_topk(zb, C_NQ // (NSA_HPG * NSA_HD), kcb, vcb, nsa_sl, batch=1, sq=seq, tq=128,
                                nblk=nch - 1, nsel=seq // SEL_BLOCK, qpos_base=0)
    o_nsa = _nsa_sw_prompt(z, zb, sel, anyblk, oc, nsa_sl)

    mem_kv = _matmul(mem_prompt.reshape(-1, D_MODEL), w_mem_kv[0].astype(BF16))
    p_mk, p_mv = mem_kv[:, :X_W], mem_kv[:, X_W:]
    y_prompt = _finish(xp, o_da, o_nsa, z, p_mk[None], p_mv[None], 1, *tail_w)

    r5 = lambda a, h: a.reshape(1, 1, a.shape[0], h, -1)
    p_states = (r5(z[:, C_DK:C_DK + 1024], DA_HEADS), r5(z[:, C_DV:C_DV + 1024], DA_HEADS),
                r5(p_kc, NSA_GROUPS), r5(p_vc, NSA_GROUPS),
                r5(z[:, C_KS:C_KS + kvw], NSA_GROUPS), r5(z[:, C_VS:C_VS + kvw], NSA_GROUPS),
                r5(z[seq - min(WINDOW, seq):, C_KW:C_KW + kvw], NSA_GROUPS),
                r5(z[seq - min(WINDOW, seq):, C_VW:C_VW + kvw], NSA_GROUPS),
                r5(p_mk, X_HEADS), r5(p_mv, X_HEADS))

    ms = db * ds
    xs = x_sample.reshape(ms, D_MODEL)
    zs, zsb = _inproj(xs, norm_mix[0], w_pad)
    z3 = zs.reshape(db, ds, ZP)
    pad_rows = lambda a, n: jnp.pad(a, ((0, 0), (0, n - a.shape[1]), (0, 0)))

    dq = z3[:, :, C_DQ:C_DQ + 1024].reshape(db, ds, DA_HEADS, 2, DA_HD) * (DA_HD ** -0.5)
    wq_da = jnp.einsum('bqhmd,hk,mn->bhmqknd', dq, jnp.eye(DA_HEADS, dtype=F32), jnp.eye(2, dtype=F32))
    wq_da = wq_da.reshape(db, DA_HEADS * 2 * ds, DA_HEADS * 2 * DA_HD).astype(BF16)
    ci = np.zeros((DA_HEADS * 2 * ds, LANE), np.float32)
    ci[:, 0] = np.repeat(_alibi(DA_HEADS), 2 * ds)
    ci[:, 1] = np.tile(np.arange(ds), DA_HEADS * 2)
    s_dk, s_dv = z3[:, :, C_DK:C_DK + 1024], z3[:, :, C_DV:C_DV + 1024]
    o_pair = _decode_attn(wq_da, jnp.asarray(ci), cache_diff_k[0].reshape(-1, PAGE, 1024),
                          cache_diff_v[0].reshape(-1, PAGE, 1024), page_table, pad_rows(s_dk, PAGE),
                          pad_rows(s_dv, PAGE), None, pos0=0, qpos0=past, win=1e9, n_new=ds, pp=8)
    o_pair = o_pair.reshape(db, DA_HEADS, 2, ds, DA_HEADS, 2 * DA_HD)
    o_pair = jnp.stack([o_pair[:, h, :, :, h] for h in range(DA_HEADS)], axis=3)
    o_da_s = _da_post_call(o_pair[:, 0].reshape(ms * DA_HEADS, LANE), o_pair[:, 1].reshape(ms * DA_HEADS, LANE),
                           da_sc, da_subln[0]).reshape(ms, DA_HEADS * LANE)

    ck = CMP_STRIDE * kvw
    s_kc, s_vc = z3[:, :, C_KC:C_KC + kvw], z3[:, :, C_VC:C_VC + kvw]
    tail_of = lambda a: jnp.pad(a.reshape(db, 1, ds * kvw), ((0, 0), (0, 7), (0, ck - ds * kvw)))
    kcb_s = _compress(cache_cmp_k[0].reshape(-1, 8, ck), page_table, tail_of(s_kc), *cmp_k)
    vcb_s = _compress(cache_cmp_v[0].reshape(-1, 8, ck), page_table, tail_of(s_vc), *cmp_v)
    tq_s = 16
    nq_pad = pad_rows(zsb.reshape(db, ds, ZP)[:, :, C_NQ:C_NQ + 1024], tq_s).reshape(db * tq_s, 1024)
    nsel_s = -(-(past + ds) // SEL_BLOCK)
    oc_s, sel_s, _ = _cmp_topk(nq_pad, 0, kcb_s, vcb_s, nsa_sl, batch=db, sq=tq_s, tq=tq_s,
                               nblk=(past + ds + CMP_STRIDE - 1) // CMP_STRIDE - 1, nsel=nsel_s, qpos_base=past)
    oc_s = oc_s.reshape(db, tq_s, 1024)[:, :ds].reshape(ms, 1024)

    nq = z3[:, :, C_NQ:C_NQ + 1024].reshape(db, ds, NSA_GROUPS, NSA_HPG, NSA_HD) * (NSA_HD ** -0.5)
    wq_n = jnp.einsum('bqghd,gk->bghqkd', nq, jnp.eye(NSA_GROUPS, dtype=F32))
    wq_n = wq_n.reshape(db, NSA_HEADS * ds, kvw).astype(BF16)
    cn = np.zeros((NSA_HEADS * ds, LANE), np.float32)
    cn[:, 0] = np.repeat(_alibi(NSA_HEADS), ds)
    cn[:, 1] = np.tile(np.arange(ds), NSA_HEADS)
    cn = jnp.asarray(cn)
    sel_cols = jnp.repeat(sel_s[:, :, None, :ds, :], NSA_HPG, axis=2).reshape(db, NSA_HEADS * ds, -1)
    s_ks, s_vs = z3[:, :, C_KS:C_KS + kvw], z3[:, :, C_VS:C_VS + kvw]
    s_kw, s_vw = z3[:, :, C_KW:C_KW + kvw], z3[:, :, C_VW:C_VW + kvw]
    o_sel = _decode_attn(wq_n, cn, cache_sel_k[0].reshape(-1, PAGE, kvw), cache_sel_v[0].reshape(-1, PAGE, kvw),
                         page_table, pad_rows(s_ks, PAGE), pad_rows(s_vs, PAGE), sel_cols,
                         pos0=0, qpos0=past, win=1e9, n_new=ds, pp=16)
    win_pages = wb // PAGE
    win_pt = jnp.arange(db * win_pages, dtype=jnp.int32).reshape(db, win_pages)
    o_win = _decode_attn(wq_n, cn, cache_win_k[0].reshape(-1, PAGE, kvw), cache_win_v[0].reshape(-1, PAGE, kvw),
                         win_pt, pad_rows(s_kw, PAGE), pad_rows(s_vw, PAGE), None,
                         pos0=past - wb, qpos0=past, win=WINDOW, n_new=ds, pp=win_pages)

    def own_group(o):
        o = o.reshape(db, NSA_GROUPS, NSA_HPG, ds, NSA_GROUPS, NSA_HD)
        o = jnp.stack([o[:, g, :, :, g] for g in range(NSA_GROUPS)], axis=1)
        return o.transpose(0, 3, 1, 2, 4).reshape(ms, NSA_HEADS * NSA_HD)

    o_nsa_s = _gate3(zs[:, C_NG:C_NG + LANE], oc_s, own_group(o_sel), own_group(o_win))
    s_mk = cache_mem_k[0].reshape(db, -1, X_W)
    s_mv = cache_mem_v[0].reshape(db, -1, X_W)
    y_sample = _finish(xs, o_da_s, o_nsa_s, zs, s_mk, s_mv, db, *tail_w)

    s5 = lambda a, h: a.reshape(1, db, a.shape[1], h, -1)
    new_win = lambda c, a: jnp.concatenate([c[0].reshape(db, wb, kvw), a], axis=1)[:, ds:]
    s_states = (s5(s_dk, DA_HEADS), s5(s_dv, DA_HEADS), s5(s_kc, NSA_GROUPS), s5(s_vc, NSA_GROUPS),
                s5(s_ks, NSA_GROUPS), s5(s_vs, NSA_GROUPS),
                s5(new_win(cache_win_k, s_kw), NSA_GROUPS), s5(new_win(cache_win_v, s_vw), NSA_GROUPS))

    return (y_prompt.reshape(1, seq, D_MODEL), y_sample.reshape(db, ds, D_MODEL)) + p_states + s_states
```

```python
import functools

import numpy as np
import jax
import jax.numpy as jnp
from jax import lax
from jax.experimental import pallas as pl
from jax.experimental.pallas import tpu as pltpu

F32 = jnp.float32
BF16 = jnp.bfloat16

D_MODEL = 1024
DA_HEADS = 8
DA_HD = 64
NSA_HEADS = 16
NSA_GROUPS = 2
NSA_HPG = NSA_HEADS // NSA_GROUPS
NSA_HD = 64
CMP_LEN = 32
CMP_STRIDE = 16
SEL_BLOCK = 64
SEL_TOP = 16
WINDOW = 512
X_HEADS = 4
X_HD = 64
X_W = X_HEADS * X_HD
EPS = 1e-6
NEG = -1e30
PICKED = -3e38
FORCE_BONUS = 1e6
LAM_INIT = 0.2
LANE = 128
PAGE = 128
VMEM_LIMIT = 56 * 1024 * 1024

C_DQ, C_DK, C_DV, C_NQ, C_GA, C_GB = 0, 1024, 2048, 3072, 4096, 5120
C_KC, C_VC, C_KS, C_VS, C_KW, C_VW, C_NG = 6144, 6272, 6400, 6528, 6656, 6784, 6912
ZP = 7040
ZP_TILE = 1408

NT_DIMS = (((1,), (1,)), ((), ()))
LOG2E = 1.4426950408889634
Q_SCALE = DA_HD ** -0.5 * LOG2E


def _cparams(sem):
    return pltpu.CompilerParams(dimension_semantics=sem, vmem_limit_bytes=VMEM_LIMIT)


def _smem_spec():
    return pl.BlockSpec(memory_space=pltpu.SMEM)


def _pages_per_step(n_pages, cap):
    return max(p for p in range(1, cap + 1) if n_pages % p == 0)


def _sigmoid(x):
    return 1.0 / (1.0 + jnp.exp(-x))


def _rms(x, g):
    return x * lax.rsqrt(jnp.mean(x * x, axis=-1, keepdims=True) + EPS) * g


def _inproj_kernel(x_ref, g_ref, w_ref, o_ref, ob_ref, xn_ref):
    @pl.when(pl.program_id(1) == 0)
    def _():
        xn_ref[...] = _rms(x_ref[...], g_ref[...]).astype(BF16)

    acc = jnp.dot(xn_ref[...], w_ref[...], preferred_element_type=F32)
    o_ref[...] = acc
    ob_ref[...] = acc.astype(BF16)


def _inproj(x, g, w_pad):
    m = x.shape[0]
    tm = min(m, 512)
    return pl.pallas_call(
        _inproj_kernel,
        grid=(m // tm, ZP // ZP_TILE),
        in_specs=[pl.BlockSpec((tm, D_MODEL), lambda i, j: (i, 0)),
                  pl.BlockSpec((1, D_MODEL), lambda i, j: (0, 0)),
                  pl.BlockSpec((D_MODEL, ZP_TILE), lambda i, j: (0, j))],
        out_specs=[pl.BlockSpec((tm, ZP_TILE), lambda i, j: (i, j)),
                   pl.BlockSpec((tm, ZP_TILE), lambda i, j: (i, j))],
        out_shape=[jax.ShapeDtypeStruct((m, ZP), F32), jax.ShapeDtypeStruct((m, ZP), BF16)],
        scratch_shapes=[pltpu.VMEM((tm, D_MODEL), BF16)],
        compiler_params=_cparams(("parallel", "arbitrary")),
        name="inproj",
    )(x, g.reshape(1, D_MODEL), w_pad)


def _prep_w_in(w_in):
    a = jnp.concatenate([w_in[:, :1024] * Q_SCALE, w_in[:, 1024:3072], w_in[:, 3072:4096] * Q_SCALE], axis=1)
    kv = w_in[:, 4096:4864]
    ng = w_in[:, 4864:4912]
    mg = w_in[:, 4912:]
    return jnp.concatenate([a, mg, kv, jnp.pad(ng, ((0, 0), (0, LANE - ng.shape[1])))], axis=1).astype(BF16)


def _matmul_kernel(x_ref, w_ref, o_ref):
    o_ref[...] = jnp.dot(x_ref[...].astype(BF16), w_ref[...], preferred_element_type=F32)


def _matmul(x, w_bf16):
    m, n = x.shape[0], w_bf16.shape[1]
    return pl.pallas_call(
        _matmul_kernel,
        out_shape=jax.ShapeDtypeStruct((m, n), F32),
        compiler_params=pltpu.CompilerParams(vmem_limit_bytes=VMEM_LIMIT),
        name="matmul",
    )(x, w_bf16)


def _da_post(o0, o1, lam, sub):
    o = o0 - lam * o1
    return _rms(o, sub) * (1.0 - LAM_INIT)


def _da_prompt_kernel(sc_ref, q_ref, k_ref, v_ref, sub_ref, o_ref, lb_ref, va_ref, sa_ref, sb_ref, m_ref, acc_ref,
                      *, tq, tkb):
    h = pl.program_id(0)
    qi = pl.program_id(1)
    lam = sc_ref[0]
    slope = sc_ref[1 + h]
    per_big = tkb // tq

    @pl.when(qi == 0)
    def _():
        i = lax.broadcasted_iota(jnp.int32, (2 * tq, tkb), 0) & (tq - 1)
        j = lax.broadcasted_iota(jnp.int32, (2 * tq, tkb), 1)
        lb_ref[...] = slope * (j - i).astype(F32)
        va_ref[:, :LANE] = v_ref[...]
        va_ref[:, LANE:] = jnp.ones((va_ref.shape[0], LANE), BF16)

    q = q_ref[...]
    lane = lax.broadcasted_iota(jnp.int32, (tq, 2 * DA_HD), 1)
    zero = jnp.zeros_like(q)
    qbd = jnp.concatenate([jnp.where(lane < DA_HD, q, zero), jnp.where(lane >= DA_HD, q, zero)], axis=0)
    m_ref[...] = jnp.full(m_ref.shape, NEG, F32)
    acc_ref[...] = jnp.zeros(acc_ref.shape, F32)

    def scores(idx):
        k = k_ref[pl.ds(pl.multiple_of(idx * tkb, tkb), tkb), :]
        return lax.dot_general(qbd, k, NT_DIMS, preferred_element_type=F32)

    def consume(idx, s_ref, masked):
        base = (qi * tq - idx * tkb).astype(F32)
        cc = slope * base
        bias = lb_ref[...]
        t = s_ref[...] + bias
        if masked:
            t = jnp.where(bias <= cc, t, NEG)
        va = va_ref[pl.ds(pl.multiple_of(idx * tkb, tkb), tkb), :]
        m_old = m_ref[...]
        m_new = jnp.maximum(m_old, jnp.max(t, axis=-1, keepdims=True) - cc)
        alpha = jnp.exp2(m_old - m_new)
        p = jnp.exp2(t - (m_new + cc))
        acc_ref[...] = alpha * acc_ref[...] + jnp.dot(p.astype(BF16), va, preferred_element_type=F32)
        m_ref[...] = m_new

    nbig = qi // per_big
    sa_ref[...] = scores(0)

    def half(idx, cur_ref, nxt_ref):
        @pl.when(idx < nbig)
        def _():
            nxt_ref[...] = scores(idx + 1)
            consume(idx, cur_ref, False)

        @pl.when(idx == nbig)
        def _():
            consume(idx, cur_ref, True)

    def pair(j, carry):
        half(2 * j, sa_ref, sb_ref)
        half(2 * j + 1, sb_ref, sa_ref)
        return carry

    lax.fori_loop(0, nbig // 2 + 1, pair, 0)

    o0 = acc_ref[0:tq, :LANE] / acc_ref[0:tq, LANE:]
    o1 = acc_ref[tq:2 * tq, :LANE] / acc_ref[tq:2 * tq, LANE:]
    o_ref[...] = _da_post(o0, o1, lam, sub_ref[...])


def _da_prompt(zb, sc, subln, tq=512, tkb=1024):
    s = zb.shape[0]
    tq = min(tq, s)
    tkb = min(tkb, s)
    kb, vb = C_DK // LANE, C_DV // LANE
    return pl.pallas_call(
        functools.partial(_da_prompt_kernel, tq=tq, tkb=tkb),
        grid=(DA_HEADS, s // tq),
        in_specs=[_smem_spec(),
                  pl.BlockSpec((tq, LANE), lambda h, i: (i, h)),
                  pl.BlockSpec((s, LANE), lambda h, i: (0, kb + h), pipeline_mode=pl.Buffered(1)),
                  pl.BlockSpec((s, LANE), lambda h, i: (0, vb + h), pipeline_mode=pl.Buffered(1)),
                  pl.BlockSpec((1, LANE), lambda h, i: (0, 0))],
        out_specs=pl.BlockSpec((tq, LANE), lambda h, i: (i, h)),
        out_shape=jax.ShapeDtypeStruct((s, DA_HEADS * LANE), F32),
        scratch_shapes=[pltpu.VMEM((2 * tq, tkb), F32),
                        pltpu.VMEM((s, 2 * LANE), BF16),
                        pltpu.VMEM((2 * tq, tkb), F32), pltpu.VMEM((2 * tq, tkb), F32),
                        pltpu.VMEM((2 * tq, 1), F32), pltpu.VMEM((2 * tq, 2 * LANE), F32)],
        compiler_params=_cparams(("arbitrary", "arbitrary")),
        name="da_prompt",
    )(sc, zb, zb, zb, subln.reshape(1, LANE))


def _compress_kernel(pt_ref, *refs, pp, nch):
    del pt_ref
    pages = refs[:pp]
    tail_ref, wbig_ref, w1_ref, pe_ref, w2_ref, out_ref, ab_ref = refs[pp:]
    s = pl.program_id(1)
    x = jnp.concatenate([p[0] for p in pages], axis=0).astype(BF16)
    rows = 8 * pp
    ab_ref[pl.ds(pl.multiple_of(s * rows, rows), rows), :] = jnp.dot(x, wbig_ref[...], preferred_element_type=F32)

    @pl.when(s == pl.num_programs(1) - 1)
    def _():
        ab_ref[nch:nch + 8, :] = jnp.dot(tail_ref[0].astype(BF16), wbig_ref[...], preferred_element_type=F32)
        hpe = jnp.dot(pe_ref[...].astype(BF16), w1_ref[...], preferred_element_type=F32)[0:1]
        outs = []
        hid = 2 * NSA_HD
        for g in range(NSA_GROUPS):
            a = ab_ref[0:nch, 2 * hid * g:2 * hid * g + hid]
            b = ab_ref[1:nch + 1, 2 * hid * g + hid:2 * hid * (g + 1)]
            hd = a + b + hpe
            act = hd * _sigmoid(hd)
            outs.append(jnp.dot(act.astype(BF16), w2_ref[...], preferred_element_type=F32))
        out_ref[0] = jnp.concatenate(outs, axis=-1)


def _compress(pool, page_table, tail, w1, pe, w2):
    b, n_pages = page_table.shape
    nch = n_pages * 8
    pp = _pages_per_step(n_pages, 16)
    ck = CMP_STRIDE * NSA_GROUPS * NSA_HD
    hid = 2 * NSA_HD
    w1r = w1.reshape(2, CMP_STRIDE, NSA_HD, hid)
    wbig = jnp.einsum('psdh,gk->sgdkph', w1r, jnp.eye(NSA_GROUPS, dtype=F32)).reshape(ck, 2 * NSA_GROUPS * hid).astype(BF16)
    pe8 = jnp.pad(pe.reshape(1, CMP_LEN * NSA_HD), ((0, 7), (0, 0)))

    def page_spec(k):
        return pl.BlockSpec((1, 8, ck), lambda bi, s, pt: (pt[bi * n_pages + s * pp + k], 0, 0))

    const = lambda shape: pl.BlockSpec(shape, lambda bi, s, pt: tuple(0 for _ in shape))
    grid_spec = pltpu.PrefetchScalarGridSpec(
        num_scalar_prefetch=1,
        grid=(b, n_pages // pp),
        in_specs=[page_spec(k) for k in range(pp)] + [
            pl.BlockSpec((1, 8, ck), lambda bi, s, pt: (bi, 0, 0)),
            const((ck, 2 * NSA_GROUPS * hid)), const((CMP_LEN * NSA_HD, hid)), const((8, CMP_LEN * NSA_HD)),
            const((hid, NSA_HD))],
        out_specs=pl.BlockSpec((1, nch, NSA_GROUPS * NSA_HD), lambda bi, s, pt: (bi, 0, 0)),
        scratch_shapes=[pltpu.VMEM((nch + 8, 2 * NSA_GROUPS * hid), F32)],
    )
    return pl.pallas_call(
        functools.partial(_compress_kernel, pp=pp, nch=nch),
        grid_spec=grid_spec,
        out_shape=jax.ShapeDtypeStruct((b, nch, NSA_GROUPS * NSA_HD), F32),
        compiler_params=_cparams(("arbitrary", "arbitrary")),
        name="compress",
    )(page_table.reshape(-1), *([pool] * pp), tail, wbig, w1.astype(BF16), pe8, w2.astype(BF16))


def _stack_group_queries(q, g, tq):
    lane = lax.broadcasted_iota(jnp.int32, (tq, 2 * NSA_HD), 1)
    mine = jnp.where(lane >= NSA_HD, 1, 0) == g
    parts = []
    for hh in range(NSA_HPG):
        qh = q[:, NSA_HD * hh:NSA_HD * (hh + 1)]
        parts.append(jnp.where(mine, jnp.concatenate([qh, qh], axis=1), jnp.zeros((tq, 2 * NSA_HD), q.dtype)))
    return jnp.concatenate(parts, axis=0)


def _group_half(x, g):
    return jnp.where(g == 0, x[:, :NSA_HD], x[:, NSA_HD:])


def _cmp_topk_kernel(sl_ref, q_ref, kc_ref, vc_ref, agg_ref, oc_ref, sel_ref, any_ref, *, tq, nb, nselp, qpos_base, topk):
    g = pl.program_id(1)
    t = pl.program_id(2)
    q0 = qpos_base + t * tq
    qpad = _stack_group_queries(q_ref[...], g, tq)
    kcb = kc_ref[0].astype(BF16)
    vcb = vc_ref[0].astype(BF16)
    s_all = lax.dot_general(qpad, kcb, NT_DIMS, preferred_element_type=F32)
    i = lax.broadcasted_iota(jnp.int32, (tq, nb), 0)
    n = lax.broadcasted_iota(jnp.int32, (tq, nb), 1)
    dist = (q0 + i - (CMP_STRIDE * n + (CMP_LEN - 1))).astype(F32)
    mask = dist >= 0
    psum = jnp.zeros((tq, nb), F32)
    for hh in range(NSA_HPG):
        slope = sl_ref[NSA_HPG * g + hh]
        tt = jnp.where(mask, s_all[hh * tq:(hh + 1) * tq] - slope * dist, NEG)
        m = jnp.max(tt, axis=-1, keepdims=True)
        e = jnp.where(mask, jnp.exp2(tt - m), 0.0)
        l = jnp.sum(e, axis=-1, keepdims=True)
        p = e / jnp.where(l == 0.0, 1.0, l)
        psum = psum + p
        o = jnp.dot(p.astype(BF16), vcb, preferred_element_type=F32)
        oc_ref[:, NSA_HD * hh:NSA_HD * (hh + 1)] = _group_half(o, g)

    p_hi = psum.astype(BF16)
    p_lo = (psum - p_hi.astype(F32)).astype(BF16)
    agg = agg_ref[...]
    imp = jnp.dot(p_hi, agg, preferred_element_type=F32) + jnp.dot(p_lo, agg, preferred_element_type=F32)
    jj = lax.broadcasted_iota(jnp.int32, (tq, nselp), 1)
    cur = (q0 + lax.broadcasted_iota(jnp.int32, (tq, nselp), 0)) // SEL_BLOCK
    valid = jj <= cur
    forced = jnp.where(valid, jnp.where(jj == 0, 1, jnp.where(jj >= cur - 1, 1, 0)), 0)
    score = jnp.where(valid, imp + jnp.where(forced == 1, FORCE_BONUS, 0.0), NEG)

    def pick(_, carry):
        score, sel = carry
        mx = jnp.max(score, axis=-1, keepdims=True)
        idx = jnp.min(jnp.where(score == mx, jj, nselp), axis=-1, keepdims=True)
        hit = jj == idx
        return jnp.where(hit, PICKED, score), jnp.where(hit, 1.0, sel)

    _, sel = lax.fori_loop(0, topk, pick, (score, jnp.zeros((tq, nselp), F32)))
    sel = jnp.where(valid, sel, 0.0)
    sel_ref[0, 0] = sel
    any_ref[0, 0, 0] = jnp.broadcast_to(jnp.max(sel, axis=0, keepdims=True), (8, nselp))


def _sel_agg_matrix(nblk, nsel, nb, nselp):
    m = np.zeros((nb, nselp), np.float32)
    j = np.arange(nsel)
    r, c = SEL_BLOCK // CMP_STRIDE, CMP_LEN // CMP_STRIDE
    for a in range(r):
        for b in range(c):
            i = r * j + a - b
            ok = (i >= 0) & (i < nblk)
            np.add.at(m, (i[ok], j[ok]), 1.0)
    return jnp.asarray(m, dtype=BF16)


def _cmp_topk(qarr, qcol0, kcb, vcb, slopes, *, batch, sq, tq, nblk, nsel, qpos_base):
    nb = kcb.shape[1]
    nselp = -(-nsel // LANE) * LANE
    nt = sq // tq
    agg = _sel_agg_matrix(nblk, nsel, nb, nselp)
    gw = NSA_HPG * NSA_HD
    return pl.pallas_call(
        functools.partial(_cmp_topk_kernel, tq=tq, nb=nb, nselp=nselp, qpos_base=qpos_base, topk=min(SEL_TOP, nsel)),
        grid=(batch, NSA_GROUPS, nt),
        in_specs=[_smem_spec(),
                  pl.BlockSpec((tq, gw), lambda b, g, t: (b * nt + t, qcol0 + g)),
                  pl.BlockSpec((1, nb, LANE), lambda b, g, t: (b, 0, 0)),
                  pl.BlockSpec((1, nb, LANE), lambda b, g, t: (b, 0, 0)),
                  pl.BlockSpec((nb, nselp), lambda b, g, t: (0, 0))],
        out_specs=[pl.BlockSpec((tq, gw), lambda b, g, t: (b * nt + t, g)),
                   pl.BlockSpec((1, 1, tq, nselp), lambda b, g, t: (b, g, t, 0)),
                   pl.BlockSpec((1, 1, 1, 8, nselp), lambda b, g, t: (b, g, t, 0, 0))],
        out_shape=[jax.ShapeDtypeStruct((batch * sq, NSA_GROUPS * gw), F32),
                   jax.ShapeDtypeStruct((batch, NSA_GROUPS, sq, nselp), F32),
                   jax.ShapeDtypeStruct((batch, NSA_GROUPS, nt, 8, nselp), F32)],
        compiler_params=_cparams(("arbitrary", "arbitrary", "arbitrary")),
        name="cmp_topk",
    )(slopes, qarr, kcb, vcb, agg)


def _nsa_sw_kernel(fl_ref, sl_ref, q_ref, ks_ref, vs_ref, kw_ref, vw_ref, sel_ref, oc_ref, ng_ref, out_ref,
                   ids_ref, m_ref, acc_ref, *, tq, nt, nselp, nwords, sb):
    g = pl.program_id(0)
    t = pl.program_id(1)
    rows = NSA_HPG * tq
    qpad = _stack_group_queries(q_ref[...], g, tq)
    slope_row = jnp.concatenate([jnp.full((tq, 1), sl_ref[NSA_HPG * g + hh], F32) for hh in range(NSA_HPG)], axis=0)
    i_loc = lax.broadcasted_iota(jnp.int32, (rows, LANE), 0) & (tq - 1)
    j_loc = lax.broadcasted_iota(jnp.int32, (rows, LANE), 1)
    dloc = (i_loc - j_loc).astype(F32)
    sl_dloc = slope_row * dloc
    selb = sel_ref[0, 0].astype(BF16)
    ones = jnp.ones((LANE, LANE), BF16)

    def chunk(ref, c):
        return ref[pl.ds(pl.multiple_of(jnp.maximum(c, 0) * LANE, LANE), LANE), :]

    def with_ones(v):
        return jnp.concatenate([v, jnp.concatenate([ones] * (v.shape[0] // LANE), axis=0)], axis=1)

    def sel_mask(c):
        jrow = lax.broadcasted_iota(jnp.int32, (nselp, LANE), 0)
        r = lax.broadcasted_iota(jnp.int32, (nselp, LANE), 1)
        expand = jnp.where(jrow == 2 * c + r // SEL_BLOCK, 1.0, 0.0).astype(BF16)
        return jnp.dot(selb, expand, preferred_element_type=F32)

    def per_head(mk, x):
        x3 = x.reshape(NSA_HPG, tq, LANE)
        return jnp.where(mk[None] > 0.5, x3, NEG).reshape(rows, LANE)

    s = lax.dot_general(qpad, chunk(ks_ref, t), NT_DIMS, preferred_element_type=F32)
    mk = jnp.where(dloc[:tq] >= 0.0, sel_mask(t), 0.0)
    tt = per_head(mk, s - sl_dloc)
    m0 = jnp.max(tt, axis=-1, keepdims=True)
    p = jnp.exp2(tt - m0)
    m_ref[...] = m0
    acc_ref[...] = jnp.dot(p.astype(BF16), with_ones(chunk(vs_ref, t)), preferred_element_type=F32)

    def scan(c, cnt):
        word = fl_ref[(g * nt + t) * nwords + c // 32]
        bit = lax.shift_right_logical(word, c % 32) & 1

        @pl.when(bit == 1)
        def _():
            ids_ref[cnt] = c

        return cnt + bit

    cnt = lax.fori_loop(0, t, scan, 0)
    for k in range(sb):
        ids_ref[cnt + k] = -1

    def sel_step(si, carry):
        cs = [ids_ref[si * sb + k] for k in range(sb)]
        kk = jnp.concatenate([chunk(ks_ref, c) for c in cs], axis=0)
        vv = with_ones(jnp.concatenate([chunk(vs_ref, c) for c in cs], axis=0))
        s = lax.dot_general(qpad, kk, NT_DIMS, preferred_element_type=F32)
        slabs = []
        for k, c in enumerate(cs):
            bias = sl_dloc + slope_row * ((t - c) * tq).astype(F32)
            slabs.append(per_head(sel_mask(c), s[:, k * LANE:(k + 1) * LANE] - bias))
        tt = jnp.concatenate(slabs, axis=1)
        m_old = m_ref[...]
        m_new = jnp.maximum(m_old, jnp.max(tt, axis=-1, keepdims=True))
        alpha = jnp.exp2(m_old - m_new)
        p = jnp.exp2(tt - m_new)
        acc_ref[...] = alpha * acc_ref[...] + jnp.dot(p.astype(BF16), vv, preferred_element_type=F32)
        m_ref[...] = m_new
        return carry

    lax.fori_loop(0, (cnt + sb - 1) // sb, sel_step, 0)
    o_s = acc_ref[:, :LANE] / acc_ref[:, LANE:]

    nback = WINDOW // tq
    c0 = jnp.maximum(t - nback, 0)
    wlen = (nback + 1) * LANE
    wstart = pl.multiple_of(c0 * LANE, LANE)
    s = lax.dot_general(qpad, kw_ref[pl.ds(wstart, wlen), :], NT_DIMS, preferred_element_type=F32)
    slabs = []
    for k in range(nback + 1):
        dist = dloc + ((t - (c0 + k)) * tq).astype(F32)
        x = s[:, k * LANE:(k + 1) * LANE] - slope_row * dist
        slabs.append(jnp.where(dist >= 0.0, jnp.where(dist < float(WINDOW), x, NEG), NEG))
    tt = jnp.concatenate(slabs, axis=1)
    p = jnp.exp2(tt - jnp.max(tt, axis=-1, keepdims=True))
    aw = jnp.dot(p.astype(BF16), with_ones(vw_ref[pl.ds(wstart, wlen), :]), preferred_element_type=F32)
    o_w = aw[:, :LANE] / aw[:, LANE:]

    gate = _sigmoid(ng_ref[...])
    glane = lax.broadcasted_iota(jnp.int32, (tq, LANE), 1)

    def gate_col(idx):
        return jnp.sum(jnp.where(glane == idx, gate, 0.0), axis=-1, keepdims=True)

    for hh in range(NSA_HPG):
        base = 3 * (NSA_HPG * g + hh)
        r0, r1 = hh * tq, (hh + 1) * tq
        o = (gate_col(base) * oc_ref[:, NSA_HD * hh:NSA_HD * (hh + 1)]
             + gate_col(base + 1) * _group_half(o_s[r0:r1], g)
             + gate_col(base + 2) * _group_half(o_w[r0:r1], g))
        out_ref[:, NSA_HD * hh:NSA_HD * (hh + 1)] = o


def _pack_chunk_flags(anyblk):
    g, nt, nselp = anyblk.shape
    chunk = jnp.max(anyblk.reshape(g, nt, nselp // 2, 2), axis=-1) > 0.5
    nchunk = nselp // 2
    nwords = -(-nchunk // 32)
    chunk = jnp.pad(chunk, ((0, 0), (0, 0), (0, nwords * 32 - nchunk)))
    bits = chunk.reshape(g, nt, nwords, 32).astype(jnp.uint32) << jnp.arange(32, dtype=jnp.uint32)
    words = jnp.sum(bits, axis=-1, dtype=jnp.uint32)
    return lax.bitcast_convert_type(words, jnp.int32).reshape(-1), nwords


def _nsa_sw_prompt(z, zb, sel, anyblk, oc, slopes, tq=128, sb=4):
    s = zb.shape[0]
    assert tq == LANE and s >= WINDOW + tq
    nt = s // tq
    nselp = sel.shape[-1]
    flags, nwords = _pack_chunk_flags(anyblk[0, :, :, 0, :])
    gw = NSA_HPG * NSA_HD
    rows = NSA_HPG * tq
    res = lambda col: pl.BlockSpec((s, LANE), lambda g, t, fl: (0, col // LANE), pipeline_mode=pl.Buffered(1))
    grid_spec = pltpu.PrefetchScalarGridSpec(
        num_scalar_prefetch=1,
        grid=(NSA_GROUPS, nt),
        in_specs=[_smem_spec(),
                  pl.BlockSpec((tq, gw), lambda g, t, fl: (t, C_NQ // gw + g)),
                  res(C_KS), res(C_VS), res(C_KW), res(C_VW),
                  pl.BlockSpec((1, 1, tq, nselp), lambda g, t, fl: (0, g, t, 0)),
                  pl.BlockSpec((tq, gw), lambda g, t, fl: (t, g)),
                  pl.BlockSpec((tq, LANE), lambda g, t, fl: (t, C_NG // LANE))],
        out_specs=pl.BlockSpec((tq, gw), lambda g, t, fl: (t, g)),
        scratch_shapes=[pltpu.SMEM((nt + sb,), jnp.int32), pltpu.VMEM((rows, 1), F32),
                        pltpu.VMEM((rows, 2 * LANE), F32)],
    )
    return pl.pallas_call(
        functools.partial(_nsa_sw_kernel, tq=tq, nt=nt, nselp=nselp, nwords=nwords, sb=sb),
        grid_spec=grid_spec,
        out_shape=jax.ShapeDtypeStruct((s, NSA_GROUPS * gw), F32),
        compiler_params=_cparams(("arbitrary", "arbitrary")),
        name="nsa_sel_win",
    )(flags, slopes, zb, zb, zb, zb, zb, sel, oc, z)


def _decode_kernel(pt_ref, *refs, pp, ncols, hk, pos0, qpos0, win, n_new, has_sel, nselp):
    del pt_ref
    kpages = refs[:pp]
    vpages = refs[pp:2 * pp]
    rest = refs[2 * pp:]
    if has_sel:
        wq_ref, ci_ref, kn_ref, vn_ref, sel_ref, o_ref, m_ref, s_ref, acc_ref = rest
    else:
        wq_ref, ci_ref, kn_ref, vn_ref, o_ref, m_ref, s_ref, acc_ref = rest
        sel_ref = None
    st = pl.program_id(1)

    @pl.when(st == 0)
    def _():
        m_ref[...] = jnp.full(m_ref.shape, NEG, F32)
        s_ref[...] = jnp.zeros(s_ref.shape, F32)
        acc_ref[...] = jnp.zeros(acc_ref.shape, F32)

    wq = wq_ref[0]
    slope = ci_ref[:, 0:1]
    qrel = ci_ref[:, 1:2]
    colhead = ci_ref[:, 2:3]

    def attend(k, v, kpos0, n_valid):
        nrows = k.shape[0]
        r = lax.broadcasted_iota(jnp.int32, (ncols, nrows), 1)
        s = lax.dot_general(wq, k.astype(BF16), NT_DIMS, preferred_element_type=F32)
        dist = (qrel + (qpos0 - kpos0).astype(F32)) - (r // hk).astype(F32)
        mask = jnp.where(dist >= 0.0, jnp.where(dist < win, 1.0, 0.0), 0.0)
        if hk > 1:
            mask = jnp.where((r % hk).astype(F32) == colhead, mask, 0.0)
        if n_valid is not None:
            mask = jnp.where(r < n_valid, mask, 0.0)
        if has_sel:
            jrow = lax.broadcasted_iota(jnp.int32, (nselp, nrows), 0)
            kp = kpos0 + lax.broadcasted_iota(jnp.int32, (nselp, nrows), 1)
            expand = jnp.where(jrow == kp // SEL_BLOCK, 1.0, 0.0).astype(BF16)
            mask = mask * jnp.dot(sel_ref[0].astype(BF16), expand, preferred_element_type=F32)
        ok = mask > 0.5
        tt = jnp.where(ok, s - slope * dist, NEG)
        m_old = m_ref[...]
        m_new = jnp.maximum(m_old, jnp.max(tt, axis=-1, keepdims=True))
        alpha = jnp.exp2(m_old - m_new)
        p = jnp.where(ok, jnp.exp2(tt - m_new), 0.0)
        s_ref[...] = alpha * s_ref[...] + jnp.sum(p, axis=-1, keepdims=True)
        acc_ref[...] = alpha * acc_ref[...] + jnp.dot(p.astype(BF16), v.astype(BF16), preferred_element_type=F32)
        m_ref[...] = m_new

    for k in range(pp):
        attend(kpages[k][0], vpages[k][0], pos0 + (st * pp + k) * PAGE, None)

    @pl.when(st == pl.num_programs(1) - 1)
    def _():
        attend(kn_ref[0], vn_ref[0], jnp.int32(qpos0), n_new * hk)
        l = s_ref[...]
        o_ref[0] = acc_ref[...] / jnp.where(l == 0.0, 1.0, l)


def _decode_attn(wq, colinfo, pool_k, pool_v, page_table, knew, vnew, sel, *, hk, pos0, qpos0, win, n_new, pp):
    b, n_pages = page_table.shape
    ncols, width = wq.shape[1], wq.shape[2]
    pp = _pages_per_step(n_pages, pp)
    has_sel = sel is not None
    nselp = sel.shape[-1] if has_sel else 0

    def page_spec(k):
        return pl.BlockSpec((1, PAGE * hk, width), lambda bi, s, pt: (pt[bi * n_pages + s * pp + k], 0, 0))

    per_batch = lambda shape: pl.BlockSpec((1,) + shape, lambda bi, s, pt: (bi, 0, 0))
    in_specs = ([page_spec(k) for k in range(pp)] * 2
                + [per_batch((ncols, width)), pl.BlockSpec((ncols, LANE), lambda bi, s, pt: (0, 0)),
                   per_batch((PAGE, width)), per_batch((PAGE, width))])
    args = [pool_k] * pp + [pool_v] * pp + [wq, colinfo, knew, vnew]
    if has_sel:
        in_specs.append(per_batch((ncols, nselp)))
        args.append(sel)
    grid_spec = pltpu.PrefetchScalarGridSpec(
        num_scalar_prefetch=1,
        grid=(b, n_pages // pp),
        in_specs=in_specs,
        out_specs=per_batch((ncols, width)),
        scratch_shapes=[pltpu.VMEM((ncols, 1), F32), pltpu.VMEM((ncols, 1), F32), pltpu.VMEM((ncols, width), F32)],
    )
    return pl.pallas_call(
        functools.partial(_decode_kernel, pp=pp, ncols=ncols, hk=hk, pos0=pos0, qpos0=qpos0, win=float(win),
                          n_new=n_new, has_sel=has_sel, nselp=nselp),
        grid_spec=grid_spec,
        out_shape=jax.ShapeDtypeStruct((b, ncols, width), F32),
        compiler_params=_cparams(("arbitrary", "arbitrary")),
        name="decode_attn",
    )(page_table.reshape(-1), *args)


def _da_post_kernel(sc_ref, o0_ref, o1_ref, sub_ref, o_ref):
    o_ref[...] = _da_post(o0_ref[...], o1_ref[...], sc_ref[0], sub_ref[...])


def _da_post_call(o0, o1, sc, subln):
    return pl.pallas_call(
        _da_post_kernel,
        in_specs=[_smem_spec(), pl.BlockSpec(o0.shape, lambda: (0, 0)), pl.BlockSpec(o0.shape, lambda: (0, 0)),
                  pl.BlockSpec((1, LANE), lambda: (0, 0))],
        out_specs=pl.BlockSpec(o0.shape, lambda: (0, 0)),
        out_shape=jax.ShapeDtypeStruct(o0.shape, F32),
        name="da_post",
    )(sc, o0, o1, subln.reshape(1, LANE))


def _gate3_kernel(ng_ref, e_ref, oc_ref, os_ref, ow_ref, o_ref):
    gate = _sigmoid(ng_ref[...])
    acc = jnp.zeros(o_ref.shape, F32)
    for br, ref in enumerate((oc_ref, os_ref, ow_ref)):
        ge = jnp.dot(gate, e_ref[br], preferred_element_type=F32, precision=lax.Precision.HIGHEST)
        acc = acc + ge * ref[...]
    o_ref[...] = acc


def _gate3(ng, oc, os_, ow):
    e = np.zeros((3, LANE, NSA_HEADS * NSA_HD), np.float32)
    for br in range(3):
        for h in range(NSA_HEADS):
            e[br, 3 * h + br, NSA_HD * h:NSA_HD * (h + 1)] = 1.0
    return pl.pallas_call(
        _gate3_kernel,
        out_shape=jax.ShapeDtypeStruct(oc.shape, F32),
        name="gate3",
    )(ng, jnp.asarray(e), oc, os_, ow)


def _tail_a_kernel(x_ref, da_ref, nsa_ref, ga_ref, gb_ref, wo_ref, nx_ref, wxq_ref, h_ref, q_ref):
    m = _sigmoid(ga_ref[...]) * da_ref[...] + _sigmoid(gb_ref[...]) * nsa_ref[...]
    h = x_ref[...] + jnp.dot(m.astype(BF16), wo_ref[...], preferred_element_type=F32)
    h_ref[...] = h
    xn = _rms(h, nx_ref[...]).astype(BF16)
    q_ref[...] = jnp.dot(xn, wxq_ref[...], preferred_element_type=F32)


def _tail_a(x, o_da, o_nsa, z, w_o, norm_x, w_xq):
    m = x.shape[0]
    tm = min(m, 512)
    row = lambda cb: pl.BlockSpec((tm, D_MODEL), lambda i: (i, cb))
    const = lambda shape: pl.BlockSpec(shape, lambda i: (0, 0), pipeline_mode=pl.Buffered(1))
    return pl.pallas_call(
        _tail_a_kernel,
        grid=(m // tm,),
        in_specs=[row(0), row(0), row(0), row(C_GA // D_MODEL), row(C_GB // D_MODEL),
                  const((D_MODEL, D_MODEL)), const((1, D_MODEL)), const((D_MODEL, X_W))],
        out_specs=[row(0), pl.BlockSpec((tm, X_W), lambda i: (i, 0))],
        out_shape=[jax.ShapeDtypeStruct((m, D_MODEL), F32), jax.ShapeDtypeStruct((m, X_W), F32)],
        compiler_params=_cparams(("parallel",)),
        name="tail_merge_wo",
    )(x, o_da, o_nsa, z, z, w_o.astype(BF16), norm_x.reshape(1, D_MODEL), w_xq.astype(BF16))


def _cross_kernel(q_ref, mk_ref, mv_ref, o_ref):
    q = (q_ref[0] * (X_HD ** -0.5)).astype(BF16)
    mk = mk_ref[0].astype(BF16)
    mv = mv_ref[0].astype(BF16)
    outs = []
    for h in range(X_HEADS):
        sl = slice(X_HD * h, X_HD * (h + 1))
        s = lax.dot_general(q[:, sl], mk[:, sl], NT_DIMS, preferred_element_type=F32)
        e = jnp.exp(s - jnp.max(s, axis=-1, keepdims=True))
        p = e / jnp.sum(e, axis=-1, keepdims=True)
        outs.append(jnp.dot(p.astype(BF16), mv[:, sl], preferred_element_type=F32))
    o_ref[0] = jnp.concatenate(outs, axis=-1)


def _cross(q, mk, mv):
    b, t, _ = q.shape
    mlen = mk.shape[1]
    tt = min(t, 512)
    return pl.pallas_call(
        _cross_kernel,
        grid=(b, t // tt),
        in_specs=[pl.BlockSpec((1, tt, X_W), lambda bi, i: (bi, i, 0)),
                  pl.BlockSpec((1, mlen, X_W), lambda bi, i: (bi, 0, 0)),
                  pl.BlockSpec((1, mlen, X_W), lambda bi, i: (bi, 0, 0))],
        out_specs=pl.BlockSpec((1, tt, X_W), lambda bi, i: (bi, i, 0)),
        out_shape=jax.ShapeDtypeStruct((b, t, X_W), F32),
        compiler_params=_cparams(("parallel", "parallel")),
        name="cross_attn",
    )(q, mk, mv)


def _tail_c_kernel(h_ref, ox_ref, wxo_ref, nf_ref, wg_ref, wu_ref, wd_ref, nfin_ref, y_ref, *, nchunk, chunk):
    h = h_ref[...] + jnp.dot(ox_ref[...].astype(BF16), wxo_ref[...], preferred_element_type=F32)
    xn = _rms(h, nf_ref[...]).astype(BF16)
    acc = jnp.zeros(h.shape, F32)
    for c in range(nchunk):
        sl = slice(c * chunk, (c + 1) * chunk)
        gt = jnp.dot(xn, wg_ref[:, sl], preferred_element_type=F32)
        up = jnp.dot(xn, wu_ref[:, sl], preferred_element_type=F32)
        act = (gt * _sigmoid(gt) * up).astype(BF16)
        acc = acc + jnp.dot(act, wd_ref[sl, :], preferred_element_type=F32)
    y_ref[...] = _rms(h + acc, nfin_ref[...])


def _tail_c(h, ox, w_xo, norm_ffn, w_gate_up, w_down, norm_final):
    m = h.shape[0]
    tm = min(m, 512)
    hid = w_down.shape[0]
    chunk = hid // 2
    const = lambda shape: pl.BlockSpec(shape, lambda i: (0, 0), pipeline_mode=pl.Buffered(1))
    return pl.pallas_call(
        functools.partial(_tail_c_kernel, nchunk=2, chunk=chunk),
        grid=(m // tm,),
        in_specs=[pl.BlockSpec((tm, D_MODEL), lambda i: (i, 0)), pl.BlockSpec((tm, X_W), lambda i: (i, 0)),
                  const((X_W, D_MODEL)), const((1, D_MODEL)), const((D_MODEL, hid)), const((D_MODEL, hid)),
                  const((hid, D_MODEL)), const((1, D_MODEL))],
        out_specs=pl.BlockSpec((tm, D_MODEL), lambda i: (i, 0)),
        out_shape=jax.ShapeDtypeStruct((m, D_MODEL), F32),
        compiler_params=_cparams(("parallel",)),
        name="tail_ffn",
    )(h, ox, w_xo.astype(BF16), norm_ffn.reshape(1, D_MODEL), w_gate_up[:, :hid].astype(BF16),
      w_gate_up[:, hid:].astype(BF16), w_down.astype(BF16), norm_final.reshape(1, D_MODEL))


def _alibi(n):
    return np.asarray(2.0 ** (-8.0 * np.arange(1, n + 1) / n) * LOG2E, dtype=np.float32)


def _finish(x, o_da, o_nsa, z, mk, mv, batch, w_o, norm_x, w_xq, w_xo, norm_ffn, w_gate_up, w_down, norm_final):
    m = x.shape[0]
    t = m // batch
    h1, qx = _tail_a(x, o_da, o_nsa, z, w_o, norm_x, w_xq)
    q3 = qx.reshape(batch, t, X_W)
    tpad = -(-t // 8) * 8
    if tpad != t:
        q3 = jnp.pad(q3, ((0, 0), (0, tpad - t), (0, 0)))
    ox = _cross(q3, mk, mv)[:, :t].reshape(m, X_W)
    return _tail_c(h1, ox, w_xo, norm_ffn, w_gate_up, w_down, norm_final)


def kernel(x_prompt, x_sample, cache_diff_k, cache_diff_v, cache_cmp_k, cache_cmp_v, cache_sel_k, cache_sel_v,
           cache_win_k, cache_win_v, cache_mem_k, cache_mem_v, page_table, mem_prompt,
           norm_mix, w_in, lam_q1, lam_k1, lam_q2, lam_k2, da_subln,
           w_cmp_k1, pe_cmp_k, w_cmp_k2, w_cmp_v1, pe_cmp_v, w_cmp_v2,
           w_o, norm_x, w_xq, w_mem_kv, w_xo, norm_ffn, w_gate_up, w_down, norm_final):
    batch, seq, _ = x_prompt.shape
    db, ds, _ = x_sample.shape
    assert batch == 1 and norm_mix.shape[0] == 1
    n_pages = page_table.shape[1]
    past = n_pages * PAGE
    wb = cache_win_k.shape[2]
    kvw = NSA_GROUPS * NSA_HD

    lam = (jnp.exp(jnp.sum(lam_q1[0] * lam_k1[0])) - jnp.exp(jnp.sum(lam_q2[0] * lam_k2[0])) + LAM_INIT).astype(F32)
    da_sc = jnp.concatenate([lam.reshape(1), jnp.asarray(_alibi(DA_HEADS))])
    nsa_sl = jnp.asarray(_alibi(NSA_HEADS))
    w_pad = _prep_w_in(w_in[0])
    tail_w = (w_o[0], norm_x[0], w_xq[0], w_xo[0], norm_ffn[0], w_gate_up[0], w_down[0], norm_final)
    cmp_k = (w_cmp_k1[0], pe_cmp_k[0], w_cmp_k2[0])
    cmp_v = (w_cmp_v1[0], pe_cmp_v[0], w_cmp_v2[0])

    xp = x_prompt.reshape(seq, D_MODEL)
    z, zb = _inproj(xp, norm_mix[0], w_pad)
    o_da = _da_prompt(zb, da_sc, da_subln[0])

    ident = jnp.arange(seq // PAGE, dtype=jnp.int32).reshape(1, -1)
    zero_tail = jnp.zeros((1, 8, CMP_STRIDE * kvw), F32)
    p_kc = z[:, C_KC:C_KC + kvw]
    p_vc = z[:, C_VC:C_VC + kvw]
    kcb = _compress(p_kc.reshape(seq // PAGE, 8, CMP_STRIDE * kvw), ident, zero_tail, *cmp_k)
    vcb = _compress(p_vc.reshape(seq // PAGE, 8, CMP_STRIDE * kvw), ident, zero_tail, *cmp_v)
    nch = seq // CMP_STRIDE
    oc, sel, anyblk = _cmp_topk(zb, C_NQ // (NSA_HPG * NSA_HD), kcb, vcb, nsa_sl, batch=1, sq=seq, tq=128,
                                nblk=nch - 1, nsel=seq // SEL_BLOCK, qpos_base=0)
    o_nsa = _nsa_sw_prompt(z, zb, sel, anyblk, oc, nsa_sl)

    mem_kv = _matmul(mem_prompt.reshape(-1, D_MODEL), w_mem_kv[0].astype(BF16))
    p_mk, p_mv = mem_kv[:, :X_W], mem_kv[:, X_W:]
    y_prompt = _finish(xp, o_da, o_nsa, z, p_mk[None], p_mv[None], 1, *tail_w)

    r5 = lambda a, h: a.reshape(1, 1, a.shape[0], h, -1)
    p_states = (r5(z[:, C_DK:C_DK + 1024], DA_HEADS), r5(z[:, C_DV:C_DV + 1024], DA_HEADS),
                r5(p_kc, NSA_GROUPS), r5(p_vc, NSA_GROUPS),
                r5(z[:, C_KS:C_KS + kvw], NSA_GROUPS), r5(z[:, C_VS:C_VS + kvw], NSA_GROUPS),
                r5(z[seq - min(WINDOW, seq):, C_KW:C_KW + kvw], NSA_GROUPS),
                r5(z[seq - min(WINDOW, seq):, C_VW:C_VW + kvw], NSA_GROUPS),
                r5(p_mk, X_HEADS), r5(p_mv, X_HEADS))

    ms = db * ds
    xs = x_sample.reshape(ms, D_MODEL)
    zs, zsb = _inproj(xs, norm_mix[0], w_pad)
    z3 = zs.reshape(db, ds, ZP)
    pad_rows = lambda a, n: jnp.pad(a, ((0, 0), (0, n - a.shape[1]), (0, 0)))

    dq = z3[:, :, C_DQ:C_DQ + 1024].reshape(db, ds, DA_HEADS, 2, DA_HD)
    wq_da = jnp.einsum('bqhmd,mn->bhmqnd', dq, jnp.eye(2, dtype=F32))
    wq_da = wq_da.reshape(db, DA_HEADS * 2 * ds, 2 * DA_HD).astype(BF16)
    ci = np.zeros((DA_HEADS * 2 * ds, LANE), np.float32)
    ci[:, 0] = np.repeat(_alibi(DA_HEADS), 2 * ds)
    ci[:, 1] = np.tile(np.arange(ds), DA_HEADS * 2)
    ci[:, 2] = np.repeat(np.arange(DA_HEADS), 2 * ds)
    s_dk, s_dv = z3[:, :, C_DK:C_DK + 1024], z3[:, :, C_DV:C_DV + 1024]
    head_rows = lambda a: pad_rows(a.reshape(db, ds * DA_HEADS, 2 * DA_HD), PAGE)
    o_pair = _decode_attn(wq_da, jnp.asarray(ci), cache_diff_k[0].reshape(-1, PAGE * DA_HEADS, 2 * DA_HD),
                          cache_diff_v[0].reshape(-1, PAGE * DA_HEADS, 2 * DA_HD), page_table, head_rows(s_dk),
                          head_rows(s_dv), None, hk=DA_HEADS, pos0=0, qpos0=past, win=1e9, n_new=ds, pp=8)
    o_pair = o_pair.reshape(db, DA_HEADS, 2, ds, 2 * DA_HD).transpose(2, 0, 3, 1, 4)
    o_da_s = _da_post_call(o_pair[0].reshape(ms * DA_HEADS, LANE), o_pair[1].reshape(ms * DA_HEADS, LANE),
                           da_sc, da_subln[0]).reshape(ms, DA_HEADS * LANE)

    ck = CMP_STRIDE * kvw
    s_kc, s_vc = z3[:, :, C_KC:C_KC + kvw], z3[:, :, C_VC:C_VC + kvw]
    tail_of = lambda a: jnp.pad(a.reshape(db, 1, ds * kvw), ((0, 0), (0, 7), (0, ck - ds * kvw)))
    kcb_s = _compress(cache_cmp_k[0].reshape(-1, 8, ck), page_table, tail_of(s_kc), *cmp_k)
    vcb_s = _compress(cache_cmp_v[0].reshape(-1, 8, ck), page_table, tail_of(s_vc), *cmp_v)
    tq_s = 16
    nq_pad = pad_rows(zsb.reshape(db, ds, ZP)[:, :, C_NQ:C_NQ + 1024], tq_s).reshape(db * tq_s, 1024)
    nsel_s = -(-(past + ds) // SEL_BLOCK)
    oc_s, sel_s, _ = _cmp_topk(nq_pad, 0, kcb_s, vcb_s, nsa_sl, batch=db, sq=tq_s, tq=tq_s,
                               nblk=(past + ds + CMP_STRIDE - 1) // CMP_STRIDE - 1, nsel=nsel_s, qpos_base=past)
    oc_s = oc_s.reshape(db, tq_s, 1024)[:, :ds].reshape(ms, 1024)

    nq = z3[:, :, C_NQ:C_NQ + 1024].reshape(db, ds, NSA_GROUPS, NSA_HPG, NSA_HD)
    wq_n = jnp.einsum('bqghd,gk->bghqkd', nq, jnp.eye(NSA_GROUPS, dtype=F32))
    wq_n = wq_n.reshape(db, NSA_HEADS * ds, kvw).astype(BF16)
    cn = np.zeros((NSA_HEADS * ds, LANE), np.float32)
    cn[:, 0] = np.repeat(_alibi(NSA_HEADS), ds)
    cn[:, 1] = np.tile(np.arange(ds), NSA_HEADS)
    cn = jnp.asarray(cn)
    sel_cols = jnp.repeat(sel_s[:, :, None, :ds, :], NSA_HPG, axis=2).reshape(db, NSA_HEADS * ds, -1)
    s_ks, s_vs = z3[:, :, C_KS:C_KS + kvw], z3[:, :, C_VS:C_VS + kvw]
    s_kw, s_vw = z3[:, :, C_KW:C_KW + kvw], z3[:, :, C_VW:C_VW + kvw]
    o_sel = _decode_attn(wq_n, cn, cache_sel_k[0].reshape(-1, PAGE, kvw), cache_sel_v[0].reshape(-1, PAGE, kvw),
                         page_table, pad_rows(s_ks, PAGE), pad_rows(s_vs, PAGE), sel_cols,
                         hk=1, pos0=0, qpos0=past, win=1e9, n_new=ds, pp=16)
    win_pages = wb // PAGE
    win_pt = jnp.arange(db * win_pages, dtype=jnp.int32).reshape(db, win_pages)
    o_win = _decode_attn(wq_n, cn, cache_win_k[0].reshape(-1, PAGE, kvw), cache_win_v[0].reshape(-1, PAGE, kvw),
                         win_pt, pad_rows(s_kw, PAGE), pad_rows(s_vw, PAGE), None,
                         hk=1, pos0=past - wb, qpos0=past, win=WINDOW, n_new=ds, pp=win_pages)

    def own_group(o):
        o = o.reshape(db, NSA_GROUPS, NSA_HPG, ds, NSA_GROUPS, NSA_HD)
        o = jnp.stack([o[:, g, :, :, g] for g in range(NSA_GROUPS)], axis=1)
        return o.transpose(0, 3, 1, 2, 4).reshape(ms, NSA_HEADS * NSA_HD)

    o_nsa_s = _gate3(zs[:, C_NG:C_NG + LANE], oc_s, own_group(o_sel), own_group(o_win))
    s_mk = cache_mem_k[0].reshape(db, -1, X_W)
    s_mv = cache_mem_v[0].reshape(db, -1, X_W)
    y_sample = _finish(xs, o_da_s, o_nsa_s, zs, s_mk, s_mv, db, *tail_w)

    s5 = lambda a, h: a.reshape(1, db, a.shape[1], h, -1)
    new_win = lambda c, a: jnp.concatenate([c[0].reshape(db, wb, kvw), a], axis=1)[:, ds:]
    s_states = (s5(s_dk, DA_HEADS), s5(s_dv, DA_HEADS), s5(s_kc, NSA_GROUPS), s5(s_vc, NSA_GROUPS),
                s5(s_ks, NSA_GROUPS), s5(s_vs, NSA_GROUPS),
                s5(new_win(cache_win_k, s_kw), NSA_GROUPS), s5(new_win(cache_win_v, s_vw), NSA_GROUPS))

    return (y_prompt.reshape(1, seq, D_MODEL), y_sample.reshape(db, ds, D_MODEL)) + p_states + s_states
```

```python
import functools

import numpy as np
import jax
import jax.numpy as jnp
from jax import lax
from jax.experimental import pallas as pl
from jax.experimental.pallas import tpu as pltpu

F32 = jnp.float32
BF16 = jnp.bfloat16

D_MODEL = 1024
DA_HEADS = 8
DA_HD = 64
NSA_HEADS = 16
NSA_GROUPS = 2
NSA_HPG = NSA_HEADS // NSA_GROUPS
NSA_HD = 64
CMP_LEN = 32
CMP_STRIDE = 16
SEL_BLOCK = 64
SEL_TOP = 16
WINDOW = 512
X_HEADS = 4
X_HD = 64
X_W = X_HEADS * X_HD
EPS = 1e-6
NEG = -1e30
PICKED = -3e38
FORCE_BONUS = 1e6
LAM_INIT = 0.2
LANE = 128
PAGE = 128
VMEM_LIMIT = 56 * 1024 * 1024

C_DQ, C_DK, C_DV, C_NQ, C_GA, C_GB = 0, 1024, 2048, 3072, 4096, 5120
C_KC, C_VC, C_KS, C_VS, C_KW, C_VW, C_NG = 6144, 6272, 6400, 6528, 6656, 6784, 6912
ZP = 7040
ZP_TILE = 1408

NT_DIMS = (((1,), (1,)), ((), ()))
LOG2E = 1.4426950408889634
Q_SCALE = DA_HD ** -0.5 * LOG2E


def _cparams(sem):
    return pltpu.CompilerParams(dimension_semantics=sem, vmem_limit_bytes=VMEM_LIMIT)


def _smem_spec():
    return pl.BlockSpec(memory_space=pltpu.SMEM)


def _pages_per_step(n_pages, cap):
    return max(p for p in range(1, cap + 1) if n_pages % p == 0)


def _sigmoid(x):
    return 1.0 / (1.0 + jnp.exp(-x))


def _rms(x, g):
    return x * lax.rsqrt(jnp.mean(x * x, axis=-1, keepdims=True) + EPS) * g


def _inproj_kernel(x_ref, g_ref, w_ref, o_ref, ob_ref, xn_ref):
    @pl.when(pl.program_id(1) == 0)
    def _():
        xn_ref[...] = _rms(x_ref[...], g_ref[...]).astype(BF16)

    acc = jnp.dot(xn_ref[...], w_ref[...], preferred_element_type=F32)
    o_ref[...] = acc
    ob_ref[...] = acc.astype(BF16)


def _inproj(x, g, w_pad):
    m = x.shape[0]
    tm = min(m, 512)
    return pl.pallas_call(
        _inproj_kernel,
        grid=(m // tm, ZP // ZP_TILE),
        in_specs=[pl.BlockSpec((tm, D_MODEL), lambda i, j: (i, 0)),
                  pl.BlockSpec((1, D_MODEL), lambda i, j: (0, 0)),
                  pl.BlockSpec((D_MODEL, ZP_TILE), lambda i, j: (0, j))],
        out_specs=[pl.BlockSpec((tm, ZP_TILE), lambda i, j: (i, j)),
                   pl.BlockSpec((tm, ZP_TILE), lambda i, j: (i, j))],
        out_shape=[jax.ShapeDtypeStruct((m, ZP), F32), jax.ShapeDtypeStruct((m, ZP), BF16)],
        scratch_shapes=[pltpu.VMEM((tm, D_MODEL), BF16)],
        compiler_params=_cparams(("parallel", "arbitrary")),
        name="inproj",
    )(x, g.reshape(1, D_MODEL), w_pad)


def _prep_w_in(w_in):
    a = jnp.concatenate([w_in[:, :1024] * Q_SCALE, w_in[:, 1024:3072], w_in[:, 3072:4096] * Q_SCALE], axis=1)
    kv = w_in[:, 4096:4864]
    ng = w_in[:, 4864:4912]
    mg = w_in[:, 4912:]
    return jnp.concatenate([a, mg, kv, jnp.pad(ng, ((0, 0), (0, LANE - ng.shape[1])))], axis=1).astype(BF16)


def _matmul_kernel(x_ref, w_ref, o_ref):
    o_ref[...] = jnp.dot(x_ref[...].astype(BF16), w_ref[...], preferred_element_type=F32)


def _matmul(x, w_bf16):
    m, n = x.shape[0], w_bf16.shape[1]
    return pl.pallas_call(
        _matmul_kernel,
        out_shape=jax.ShapeDtypeStruct((m, n), F32),
        compiler_params=pltpu.CompilerParams(vmem_limit_bytes=VMEM_LIMIT),
        name="matmul",
    )(x, w_bf16)


def _da_post(o0, o1, lam, sub):
    o = o0 - lam * o1
    return _rms(o, sub) * (1.0 - LAM_INIT)


def _da_prompt_kernel(sc_ref, q_ref, k_ref, v_ref, sub_ref, o_ref, lb_ref, va_ref, sa_ref, sb_ref, m_ref, acc_ref,
                      *, tq, tkb):
    h = pl.program_id(0)
    qi = pl.program_id(1)
    lam = sc_ref[0]
    slope = sc_ref[1 + h]
    per_big = tkb // tq

    @pl.when(qi == 0)
    def _():
        i = lax.broadcasted_iota(jnp.int32, (2 * tq, tkb), 0) & (tq - 1)
        j = lax.broadcasted_iota(jnp.int32, (2 * tq, tkb), 1)
        lb_ref[...] = slope * (j - i).astype(F32)
        va_ref[:, :LANE] = v_ref[...]
        va_ref[:, LANE:] = jnp.ones((va_ref.shape[0], LANE), BF16)

    q = q_ref[...]
    lane = lax.broadcasted_iota(jnp.int32, (tq, 2 * DA_HD), 1)
    zero = jnp.zeros_like(q)
    qbd = jnp.concatenate([jnp.where(lane < DA_HD, q, zero), jnp.where(lane >= DA_HD, q, zero)], axis=0)
    m_ref[...] = jnp.full(m_ref.shape, NEG, F32)
    acc_ref[...] = jnp.zeros(acc_ref.shape, F32)

    def scores(idx):
        k = k_ref[pl.ds(pl.multiple_of(idx * tkb, tkb), tkb), :]
        return lax.dot_general(qbd, k, NT_DIMS, preferred_element_type=F32)

    def consume(idx, s_ref, masked):
        base = (qi * tq - idx * tkb).astype(F32)
        cc = slope * base
        bias = lb_ref[...]
        t = s_ref[...] + bias
        if masked:
            t = jnp.where(bias <= cc, t, NEG)
        va = va_ref[pl.ds(pl.multiple_of(idx * tkb, tkb), tkb), :]
        m_old = m_ref[...]
        m_new = jnp.maximum(m_old, jnp.max(t, axis=-1, keepdims=True) - cc)
        alpha = jnp.exp2(m_old - m_new)
        p = jnp.exp2(t - (m_new + cc))
        acc_ref[...] = alpha * acc_ref[...] + jnp.dot(p.astype(BF16), va, preferred_element_type=F32)
        m_ref[...] = m_new

    nbig = qi // per_big
    sa_ref[...] = scores(0)

    def half(idx, cur_ref, nxt_ref):
        @pl.when(idx < nbig)
        def _():
            nxt_ref[...] = scores(idx + 1)
            consume(idx, cur_ref, False)

        @pl.when(idx == nbig)
        def _():
            consume(idx, cur_ref, True)

    def pair(j, carry):
        half(2 * j, sa_ref, sb_ref)
        half(2 * j + 1, sb_ref, sa_ref)
        return carry

    lax.fori_loop(0, nbig // 2 + 1, pair, 0)

    o0 = acc_ref[0:tq, :LANE] / acc_ref[0:tq, LANE:]
    o1 = acc_ref[tq:2 * tq, :LANE] / acc_ref[tq:2 * tq, LANE:]
    o_ref[...] = _da_post(o0, o1, lam, sub_ref[...])


def _da_prompt(zb, sc, subln, tq=512, tkb=1024):
    s = zb.shape[0]
    tq = min(tq, s)
    tkb = min(tkb, s)
    kb, vb = C_DK // LANE, C_DV // LANE
    return pl.pallas_call(
        functools.partial(_da_prompt_kernel, tq=tq, tkb=tkb),
        grid=(DA_HEADS, s // tq),
        in_specs=[_smem_spec(),
                  pl.BlockSpec((tq, LANE), lambda h, i: (i, h)),
                  pl.BlockSpec((s, LANE), lambda h, i: (0, kb + h), pipeline_mode=pl.Buffered(1)),
                  pl.BlockSpec((s, LANE), lambda h, i: (0, vb + h), pipeline_mode=pl.Buffered(1)),
                  pl.BlockSpec((1, LANE), lambda h, i: (0, 0))],
        out_specs=pl.BlockSpec((tq, LANE), lambda h, i: (i, h)),
        out_shape=jax.ShapeDtypeStruct((s, DA_HEADS * LANE), F32),
        scratch_shapes=[pltpu.VMEM((2 * tq, tkb), F32),
                        pltpu.VMEM((s, 2 * LANE), BF16),
                        pltpu.VMEM((2 * tq, tkb), F32), pltpu.VMEM((2 * tq, tkb), F32),
                        pltpu.VMEM((2 * tq, 1), F32), pltpu.VMEM((2 * tq, 2 * LANE), F32)],
        compiler_params=_cparams(("arbitrary", "arbitrary")),
        name="da_prompt",
    )(sc, zb, zb, zb, subln.reshape(1, LANE))


def _compress_kernel(pt_ref, *refs, pp, nch):
    del pt_ref
    pages = refs[:pp]
    tail_ref, wbig_ref, w1_ref, pe_ref, w2_ref, out_ref, ab_ref, rows_ref, x_ref = refs[pp:]
    s = pl.program_id(1)
    for k in range(pp):
        rows_ref[k] = pages[k][0].T
    for k in range(pp):
        for pos in range(CMP_STRIDE):
            x_ref[8 * k:8 * (k + 1), LANE * pos:LANE * (pos + 1)] = rows_ref[k, pl.ds(pos, 8, stride=CMP_STRIDE), :]
    x = x_ref[...].astype(BF16)
    rows = 8 * pp
    ab_ref[pl.ds(pl.multiple_of(s * rows, rows), rows), :] = jnp.dot(x, wbig_ref[...], preferred_element_type=F32)

    @pl.when(s == pl.num_programs(1) - 1)
    def _():
        ab_ref[nch:nch + 8, :] = jnp.dot(tail_ref[0].astype(BF16), wbig_ref[...], preferred_element_type=F32)
        hpe = jnp.dot(pe_ref[...].astype(BF16), w1_ref[...], preferred_element_type=F32)[0:1]
        outs = []
        hid = 2 * NSA_HD
        for g in range(NSA_GROUPS):
            a = ab_ref[0:nch, 2 * hid * g:2 * hid * g + hid]
            b = ab_ref[1:nch + 1, 2 * hid * g + hid:2 * hid * (g + 1)]
            hd = a + b + hpe
            act = hd * _sigmoid(hd)
            outs.append(jnp.dot(act.astype(BF16), w2_ref[...], preferred_element_type=F32))
        out_ref[0] = jnp.concatenate(outs, axis=-1)


def _compress(pool, page_table, tail, w1, pe, w2):
    b, n_pages = page_table.shape
    nch = n_pages * 8
    pp = _pages_per_step(n_pages, 16)
    ck = CMP_STRIDE * NSA_GROUPS * NSA_HD
    hid = 2 * NSA_HD
    w1r = w1.reshape(2, CMP_STRIDE, NSA_HD, hid)
    wbig = jnp.einsum('psdh,gk->sgdkph', w1r, jnp.eye(NSA_GROUPS, dtype=F32)).reshape(ck, 2 * NSA_GROUPS * hid).astype(BF16)
    pe8 = jnp.pad(pe.reshape(1, CMP_LEN * NSA_HD), ((0, 7), (0, 0)))

    def page_spec(k):
        return pl.BlockSpec((1, LANE, PAGE), lambda bi, s, pt: (pt[bi * n_pages + s * pp + k], 0, 0))

    const = lambda shape: pl.BlockSpec(shape, lambda bi, s, pt: tuple(0 for _ in shape))
    grid_spec = pltpu.PrefetchScalarGridSpec(
        num_scalar_prefetch=1,
        grid=(b, n_pages // pp),
        in_specs=[page_spec(k) for k in range(pp)] + [
            pl.BlockSpec((1, 8, ck), lambda bi, s, pt: (bi, 0, 0)),
            const((ck, 2 * NSA_GROUPS * hid)), const((CMP_LEN * NSA_HD, hid)), const((8, CMP_LEN * NSA_HD)),
            const((hid, NSA_HD))],
        out_specs=pl.BlockSpec((1, nch, NSA_GROUPS * NSA_HD), lambda bi, s, pt: (bi, 0, 0)),
        scratch_shapes=[pltpu.VMEM((nch + 8, 2 * NSA_GROUPS * hid), F32), pltpu.VMEM((pp, PAGE, LANE), F32),
                        pltpu.VMEM((8 * pp, ck), F32)],
    )
    return pl.pallas_call(
        functools.partial(_compress_kernel, pp=pp, nch=nch),
        grid_spec=grid_spec,
        out_shape=jax.ShapeDtypeStruct((b, nch, NSA_GROUPS * NSA_HD), F32),
        compiler_params=_cparams(("arbitrary", "arbitrary")),
        name="compress",
    )(page_table.reshape(-1), *([pool] * pp), tail, wbig, w1.astype(BF16), pe8, w2.astype(BF16))


def _stack_group_queries(q, g, tq):
    lane = lax.broadcasted_iota(jnp.int32, (tq, 2 * NSA_HD), 1)
    mine = jnp.where(lane >= NSA_HD, 1, 0) == g
    parts = []
    for hh in range(NSA_HPG):
        qh = q[:, NSA_HD * hh:NSA_HD * (hh + 1)]
        parts.append(jnp.where(mine, jnp.concatenate([qh, qh], axis=1), jnp.zeros((tq, 2 * NSA_HD), q.dtype)))
    return jnp.concatenate(parts, axis=0)


def _group_half(x, g):
    return jnp.where(g == 0, x[:, :NSA_HD], x[:, NSA_HD:])


def _cmp_topk_kernel(sl_ref, q_ref, kc_ref, vc_ref, agg_ref, oc_ref, sel_ref, any_ref, *, tq, nb, nselp, qpos_base, topk):
    g = pl.program_id(1)
    t = pl.program_id(2)
    q0 = qpos_base + t * tq
    qpad = _stack_group_queries(q_ref[...], g, tq)
    kcb = kc_ref[0].astype(BF16)
    vcb = vc_ref[0].astype(BF16)
    s_all = lax.dot_general(qpad, kcb, NT_DIMS, preferred_element_type=F32)
    i = lax.broadcasted_iota(jnp.int32, (tq, nb), 0)
    n = lax.broadcasted_iota(jnp.int32, (tq, nb), 1)
    dist = (q0 + i - (CMP_STRIDE * n + (CMP_LEN - 1))).astype(F32)
    mask = dist >= 0
    psum = jnp.zeros((tq, nb), F32)
    for hh in range(NSA_HPG):
        slope = sl_ref[NSA_HPG * g + hh]
        tt = jnp.where(mask, s_all[hh * tq:(hh + 1) * tq] - slope * dist, NEG)
        m = jnp.max(tt, axis=-1, keepdims=True)
        e = jnp.where(mask, jnp.exp2(tt - m), 0.0)
        l = jnp.sum(e, axis=-1, keepdims=True)
        p = e / jnp.where(l == 0.0, 1.0, l)
        psum = psum + p
        o = jnp.dot(p.astype(BF16), vcb, preferred_element_type=F32)
        oc_ref[:, NSA_HD * hh:NSA_HD * (hh + 1)] = _group_half(o, g)

    p_hi = psum.astype(BF16)
    p_lo = (psum - p_hi.astype(F32)).astype(BF16)
    agg = agg_ref[...]
    imp = jnp.dot(p_hi, agg, preferred_element_type=F32) + jnp.dot(p_lo, agg, preferred_element_type=F32)
    jj = lax.broadcasted_iota(jnp.int32, (tq, nselp), 1)
    cur = (q0 + lax.broadcasted_iota(jnp.int32, (tq, nselp), 0)) // SEL_BLOCK
    valid = jj <= cur
    forced = jnp.where(valid, jnp.where(jj == 0, 1, jnp.where(jj >= cur - 1, 1, 0)), 0)
    score = jnp.where(valid, imp + jnp.where(forced == 1, FORCE_BONUS, 0.0), NEG)

    flip = tq % LANE == 0
    ax = 0 if flip else 1
    cand = lax.broadcasted_iota(jnp.int32, (nselp, tq), 0) if flip else jj

    def pick(_, carry):
        sc, chosen = carry
        mx = jnp.max(sc, axis=ax, keepdims=True)
        idx = jnp.min(jnp.where(sc == mx, cand, nselp), axis=ax, keepdims=True)
        hit = cand == idx
        return jnp.where(hit, PICKED, sc), jnp.where(hit, 1.0, chosen)

    start = score.T if flip else score
    _, chosen = lax.fori_loop(0, topk, pick, (start, jnp.zeros(start.shape, F32)))
    sel = jnp.where(valid, chosen.T if flip else chosen, 0.0)
    sel_ref[0, 0] = sel
    any_ref[0, 0, 0] = jnp.broadcast_to(jnp.max(sel, axis=0, keepdims=True), (8, nselp))


def _sel_agg_matrix(nblk, nsel, nb, nselp):
    m = np.zeros((nb, nselp), np.float32)
    j = np.arange(nsel)
    r, c = SEL_BLOCK // CMP_STRIDE, CMP_LEN // CMP_STRIDE
    for a in range(r):
        for b in range(c):
            i = r * j + a - b
            ok = (i >= 0) & (i < nblk)
            np.add.at(m, (i[ok], j[ok]), 1.0)
    return jnp.asarray(m, dtype=BF16)


def _cmp_topk(qarr, qcol0, kcb, vcb, slopes, *, batch, sq, tq, nblk, nsel, qpos_base):
    nb = kcb.shape[1]
    nselp = -(-nsel // LANE) * LANE
    nt = sq // tq
    agg = _sel_agg_matrix(nblk, nsel, nb, nselp)
    gw = NSA_HPG * NSA_HD
    return pl.pallas_call(
        functools.partial(_cmp_topk_kernel, tq=tq, nb=nb, nselp=nselp, qpos_base=qpos_base, topk=min(SEL_TOP, nsel)),
        grid=(batch, NSA_GROUPS, nt),
        in_specs=[_smem_spec(),
                  pl.BlockSpec((tq, gw), lambda b, g, t: (b * nt + t, qcol0 + g)),
                  pl.BlockSpec((1, nb, LANE), lambda b, g, t: (b, 0, 0)),
                  pl.BlockSpec((1, nb, LANE), lambda b, g, t: (b, 0, 0)),
                  pl.BlockSpec((nb, nselp), lambda b, g, t: (0, 0))],
        out_specs=[pl.BlockSpec((tq, gw), lambda b, g, t: (b * nt + t, g)),
                   pl.BlockSpec((1, 1, tq, nselp), lambda b, g, t: (b, g, t, 0)),
                   pl.BlockSpec((1, 1, 1, 8, nselp), lambda b, g, t: (b, g, t, 0, 0))],
        out_shape=[jax.ShapeDtypeStruct((batch * sq, NSA_GROUPS * gw), F32),
                   jax.ShapeDtypeStruct((batch, NSA_GROUPS, sq, nselp), F32),
                   jax.ShapeDtypeStruct((batch, NSA_GROUPS, nt, 8, nselp), F32)],
        compiler_params=_cparams(("arbitrary", "arbitrary", "arbitrary")),
        name="cmp_topk",
    )(slopes, qarr, kcb, vcb, agg)


def _nsa_sw_kernel(fl_ref, sl_ref, q_ref, ks_ref, vs_ref, kw_ref, vw_ref, sel_ref, oc_ref, ng_ref, out_ref,
                   ids_ref, m_ref, acc_ref, *, tq, nt, nselp, nwords, sb):
    g = pl.program_id(0)
    t = pl.program_id(1)
    rows = NSA_HPG * tq
    qpad = _stack_group_queries(q_ref[...], g, tq)
    slope_row = jnp.concatenate([jnp.full((tq, 1), sl_ref[NSA_HPG * g + hh], F32) for hh in range(NSA_HPG)], axis=0)
    i_loc = lax.broadcasted_iota(jnp.int32, (rows, LANE), 0) & (tq - 1)
    j_loc = lax.broadcasted_iota(jnp.int32, (rows, LANE), 1)
    dloc = (i_loc - j_loc).astype(F32)
    sl_dloc = slope_row * dloc
    selb = sel_ref[0, 0].astype(BF16)
    ones = jnp.ones((LANE, LANE), BF16)

    def chunk(ref, c):
        return ref[pl.ds(pl.multiple_of(jnp.maximum(c, 0) * LANE, LANE), LANE), :]

    def with_ones(v):
        return jnp.concatenate([v, jnp.concatenate([ones] * (v.shape[0] // LANE), axis=0)], axis=1)

    def sel_mask(c):
        jrow = lax.broadcasted_iota(jnp.int32, (nselp, LANE), 0)
        r = lax.broadcasted_iota(jnp.int32, (nselp, LANE), 1)
        expand = jnp.where(jrow == 2 * c + r // SEL_BLOCK, 1.0, 0.0).astype(BF16)
        return jnp.dot(selb, expand, preferred_element_type=F32)

    def per_head(mk, x):
        x3 = x.reshape(NSA_HPG, tq, LANE)
        return jnp.where(mk[None] > 0.5, x3, NEG).reshape(rows, LANE)

    s = lax.dot_general(qpad, chunk(ks_ref, t), NT_DIMS, preferred_element_type=F32)
    mk = jnp.where(dloc[:tq] >= 0.0, sel_mask(t), 0.0)
    tt = per_head(mk, s - sl_dloc)
    m0 = jnp.max(tt, axis=-1, keepdims=True)
    p = jnp.exp2(tt - m0)
    m_ref[...] = m0
    acc_ref[...] = jnp.dot(p.astype(BF16), with_ones(chunk(vs_ref, t)), preferred_element_type=F32)

    def scan(c, cnt):
        word = fl_ref[(g * nt + t) * nwords + c // 32]
        bit = lax.shift_right_logical(word, c % 32) & 1

        @pl.when(bit == 1)
        def _():
            ids_ref[cnt] = c

        return cnt + bit

    cnt = lax.fori_loop(0, t, scan, 0)
    for k in range(sb):
        ids_ref[cnt + k] = -1

    def sel_step(si, carry):
        cs = [ids_ref[si * sb + k] for k in range(sb)]
        kk = jnp.concatenate([chunk(ks_ref, c) for c in cs], axis=0)
        vv = with_ones(jnp.concatenate([chunk(vs_ref, c) for c in cs], axis=0))
        s = lax.dot_general(qpad, kk, NT_DIMS, preferred_element_type=F32)
        slabs = []
        for k, c in enumerate(cs):
            bias = sl_dloc + slope_row * ((t - c) * tq).astype(F32)
            slabs.append(per_head(sel_mask(c), s[:, k * LANE:(k + 1) * LANE] - bias))
        tt = jnp.concatenate(slabs, axis=1)
        m_old = m_ref[...]
        m_new = jnp.maximum(m_old, jnp.max(tt, axis=-1, keepdims=True))
        alpha = jnp.exp2(m_old - m_new)
        p = jnp.exp2(tt - m_new)
        acc_ref[...] = alpha * acc_ref[...] + jnp.dot(p.astype(BF16), vv, preferred_element_type=F32)
        m_ref[...] = m_new
        return carry

    lax.fori_loop(0, (cnt + sb - 1) // sb, sel_step, 0)
    o_s = acc_ref[:, :LANE] / acc_ref[:, LANE:]

    nback = WINDOW // tq
    c0 = jnp.maximum(t - nback, 0)
    wlen = (nback + 1) * LANE
    wstart = pl.multiple_of(c0 * LANE, LANE)
    s = lax.dot_general(qpad, kw_ref[pl.ds(wstart, wlen), :], NT_DIMS, preferred_element_type=F32)
    slabs = []
    for k in range(nback + 1):
        dist = dloc + ((t - (c0 + k)) * tq).astype(F32)
        x = s[:, k * LANE:(k + 1) * LANE] - slope_row * dist
        slabs.append(jnp.where(dist >= 0.0, jnp.where(dist < float(WINDOW), x, NEG), NEG))
    tt = jnp.concatenate(slabs, axis=1)
    p = jnp.exp2(tt - jnp.max(tt, axis=-1, keepdims=True))
    aw = jnp.dot(p.astype(BF16), with_ones(vw_ref[pl.ds(wstart, wlen), :]), preferred_element_type=F32)
    o_w = aw[:, :LANE] / aw[:, LANE:]

    gate = _sigmoid(ng_ref[...])
    glane = lax.broadcasted_iota(jnp.int32, (tq, LANE), 1)

    def gate_col(idx):
        return jnp.sum(jnp.where(glane == idx, gate, 0.0), axis=-1, keepdims=True)

    for hh in range(NSA_HPG):
        base = 3 * (NSA_HPG * g + hh)
        r0, r1 = hh * tq, (hh + 1) * tq
        o = (gate_col(base) * oc_ref[:, NSA_HD * hh:NSA_HD * (hh + 1)]
             + gate_col(base + 1) * _group_half(o_s[r0:r1], g)
             + gate_col(base + 2) * _group_half(o_w[r0:r1], g))
        out_ref[:, NSA_HD * hh:NSA_HD * (hh + 1)] = o


def _pack_chunk_flags(anyblk):
    g, nt, nselp = anyblk.shape
    chunk = jnp.max(anyblk.reshape(g, nt, nselp // 2, 2), axis=-1) > 0.5
    nchunk = nselp // 2
    nwords = -(-nchunk // 32)
    chunk = jnp.pad(chunk, ((0, 0), (0, 0), (0, nwords * 32 - nchunk)))
    bits = chunk.reshape(g, nt, nwords, 32).astype(jnp.uint32) << jnp.arange(32, dtype=jnp.uint32)
    words = jnp.sum(bits, axis=-1, dtype=jnp.uint32)
    return lax.bitcast_convert_type(words, jnp.int32).reshape(-1), nwords


def _nsa_sw_prompt(z, zb, sel, anyblk, oc, slopes, tq=128, sb=4):
    s = zb.shape[0]
    assert tq == LANE and s >= WINDOW + tq
    nt = s // tq
    nselp = sel.shape[-1]
    flags, nwords = _pack_chunk_flags(anyblk[0, :, :, 0, :])
    gw = NSA_HPG * NSA_HD
    rows = NSA_HPG * tq
    res = lambda col: pl.BlockSpec((s, LANE), lambda g, t, fl: (0, col // LANE), pipeline_mode=pl.Buffered(1))
    grid_spec = pltpu.PrefetchScalarGridSpec(
        num_scalar_prefetch=1,
        grid=(NSA_GROUPS, nt),
        in_specs=[_smem_spec(),
                  pl.BlockSpec((tq, gw), lambda g, t, fl: (t, C_NQ // gw + g)),
                  res(C_KS), res(C_VS), res(C_KW), res(C_VW),
                  pl.BlockSpec((1, 1, tq, nselp), lambda g, t, fl: (0, g, t, 0)),
                  pl.BlockSpec((tq, gw), lambda g, t, fl: (t, g)),
                  pl.BlockSpec((tq, LANE), lambda g, t, fl: (t, C_NG // LANE))],
        out_specs=pl.BlockSpec((tq, gw), lambda g, t, fl: (t, g)),
        scratch_shapes=[pltpu.SMEM((nt + sb,), jnp.int32), pltpu.VMEM((rows, 1), F32),
                        pltpu.VMEM((rows, 2 * LANE), F32)],
    )
    return pl.pallas_call(
        functools.partial(_nsa_sw_kernel, tq=tq, nt=nt, nselp=nselp, nwords=nwords, sb=sb),
        grid_spec=grid_spec,
        out_shape=jax.ShapeDtypeStruct((s, NSA_GROUPS * gw), F32),
        compiler_params=_cparams(("arbitrary", "arbitrary")),
        name="nsa_sel_win",
    )(flags, slopes, zb, zb, zb, zb, zb, sel, oc, z)


def _decode_kernel(pt_ref, pos_ref, cnt_ref, *refs, pp, n_pages, ncols, hk, qpos0, win, n_new, has_sel, nselp,
                   feature_major):
    del pt_ref
    kpages = refs[:pp]
    vpages = refs[pp:2 * pp]
    rest = refs[2 * pp:]
    if has_sel:
        wq_ref, ci_ref, kn_ref, vn_ref, sel_ref, o_ref, m_ref, s_ref, acc_ref = rest
    else:
        wq_ref, ci_ref, kn_ref, vn_ref, o_ref, m_ref, s_ref, acc_ref = rest
        sel_ref = None
    bi = pl.program_id(0)
    st = pl.program_id(1)
    windowed = win < 1e8
    guarded = has_sel or windowed

    @pl.when(st == 0)
    def _():
        m_ref[...] = jnp.full(m_ref.shape, NEG, F32)
        s_ref[...] = jnp.zeros(s_ref.shape, F32)
        acc_ref[...] = jnp.zeros(acc_ref.shape, F32)

    wq = wq_ref[0]
    slope = ci_ref[:, 0:1]
    qrel = ci_ref[:, 1:2]
    colhead = ci_ref[:, 2:3]

    def geometry(nrows):
        r = lax.broadcasted_iota(jnp.int32, (ncols, nrows), 1)
        kidx = (r // hk).astype(F32)
        base = slope * kidx
        if hk > 1:
            base = jnp.where((r % hk).astype(F32) == colhead, base, NEG)
        return r, kidx, base

    def scores(k, kpos0, n_valid, geom, check_range, feature_major):
        r, kidx, base = geom
        if feature_major:
            s = jnp.dot(wq, k.astype(BF16), preferred_element_type=F32)
        else:
            s = lax.dot_general(wq, k.astype(BF16), NT_DIMS, preferred_element_type=F32)
        off = qrel + (qpos0 - kpos0).astype(F32)
        t = (s + base) - slope * off
        if not (check_range or has_sel or n_valid is not None):
            return t, None
        okf = jnp.ones(t.shape, F32)
        if check_range:
            dist = off - kidx
            okf = jnp.where(dist >= 0.0, jnp.where(dist < win, 1.0, 0.0), 0.0)
        if n_valid is not None:
            okf = jnp.where(r < n_valid, okf, 0.0)
        if has_sel:
            nrows = t.shape[1]
            jrow = lax.broadcasted_iota(jnp.int32, (nselp, nrows), 0)
            kp = kpos0 + lax.broadcasted_iota(jnp.int32, (nselp, nrows), 1)
            expand = jnp.where(jrow == kp // SEL_BLOCK, 1.0, 0.0).astype(BF16)
            okf = okf * jnp.dot(sel_ref[0].astype(BF16), expand, preferred_element_type=F32)
        ok = okf > 0.5
        return jnp.where(ok, t, NEG), ok

    def update(ts, oks, vs, feature_major):
        m_old = m_ref[...]
        m_new = m_old
        for t in ts:
            m_new = jnp.maximum(m_new, jnp.max(t, axis=-1, keepdims=True))
        alpha = jnp.exp2(m_old - m_new)
        l = alpha * s_ref[...]
        acc = alpha * acc_ref[...]
        for t, ok, v in zip(ts, oks, vs):
            p = jnp.exp2(t - m_new)
            if ok is not None:
                p = jnp.where(ok, p, 0.0)
            l = l + jnp.sum(p, axis=-1, keepdims=True)
            if feature_major:
                acc = acc + lax.dot_general(p.astype(BF16), v.astype(BF16), NT_DIMS, preferred_element_type=F32)
            else:
                acc = acc + jnp.dot(p.astype(BF16), v.astype(BF16), preferred_element_type=F32)
        s_ref[...] = l
        acc_ref[...] = acc
        m_ref[...] = m_new

    cnt = cnt_ref[bi]

    @pl.when(st * pp < cnt)
    def _():
        geom = geometry(PAGE * hk)
        ts, oks, vs = [], [], []
        for k in range(pp):
            slot = st * pp + k
            n_valid = jnp.where(slot < cnt, PAGE * hk, 0) if guarded else None
            t, ok = scores(kpages[k][0], pos_ref[bi * n_pages + slot], n_valid, geom, windowed, feature_major)
            ts.append(t)
            oks.append(ok)
            vs.append(vpages[k][0])
        update(ts, oks, vs, feature_major)

    @pl.when(st == pl.num_programs(1) - 1)
    def _():
        t, ok = scores(kn_ref[0], jnp.int32(qpos0), n_new * hk, geometry(PAGE), True, False)
        update([t], [ok], [vn_ref[0]], False)
        l = s_ref[...]
        o_ref[0] = acc_ref[...] / jnp.where(l == 0.0, 1.0, l)


def _decode_attn(wq, colinfo, pool_k, pool_v, page_ids, page_pos, page_cnt, knew, vnew, sel, *, hk, qpos0, win, n_new, pp,
                 feature_major=False):
    b, n_pages = page_ids.shape
    ncols, width = wq.shape[1], wq.shape[2]
    pp = _pages_per_step(n_pages, pp)
    has_sel = sel is not None
    nselp = sel.shape[-1] if has_sel else 0

    def page_spec(k):
        return pl.BlockSpec((1, width, PAGE) if feature_major else (1, PAGE * hk, width),
                            lambda bi, s, pt, pos, cnt: (pt[bi * n_pages + s * pp + k], 0, 0))

    per_batch = lambda shape: pl.BlockSpec((1,) + shape, lambda bi, s, pt, pos, cnt: (bi, 0, 0))
    in_specs = ([page_spec(k) for k in range(pp)] * 2
                + [per_batch((ncols, width)), pl.BlockSpec((ncols, LANE), lambda bi, s, pt, pos, cnt: (0, 0)),
                   per_batch((PAGE, width)), per_batch((PAGE, width))])
    args = [pool_k] * pp + [pool_v] * pp + [wq, colinfo, knew, vnew]
    if has_sel:
        in_specs.append(per_batch((ncols, nselp)))
        args.append(sel)
    grid_spec = pltpu.PrefetchScalarGridSpec(
        num_scalar_prefetch=3,
        grid=(b, n_pages // pp),
        in_specs=in_specs,
        out_specs=per_batch((ncols, width)),
        scratch_shapes=[pltpu.VMEM((ncols, 1), F32), pltpu.VMEM((ncols, 1), F32), pltpu.VMEM((ncols, width), F32)],
    )
    return pl.pallas_call(
        functools.partial(_decode_kernel, pp=pp, n_pages=n_pages, ncols=ncols, hk=hk, qpos0=qpos0, win=float(win),
                          n_new=n_new, has_sel=has_sel, nselp=nselp, feature_major=feature_major),
        grid_spec=grid_spec,
        out_shape=jax.ShapeDtypeStruct((b, ncols, width), F32),
        compiler_params=_cparams(("arbitrary", "arbitrary")),
        name="decode_attn",
    )(page_ids.reshape(-1), page_pos.reshape(-1), page_cnt, *args)


def _all_pages(page_table, pos0):
    b, n_pages = page_table.shape
    pos = jnp.broadcast_to(pos0 + PAGE * jnp.arange(n_pages, dtype=jnp.int32), (b, n_pages))
    return page_table, pos, jnp.full((b,), n_pages, jnp.int32)


def _selected_pages(page_table, sel_cols):
    b, n_pages = page_table.shape
    per_page = PAGE // SEL_BLOCK
    hit = jnp.max(sel_cols[:, :, :per_page * n_pages].reshape(b, -1, n_pages, per_page), axis=(1, 3)) > 0.5
    cnt = jnp.sum(hit, axis=1).astype(jnp.int32)
    order = jnp.argsort(jnp.logical_not(hit), axis=1, stable=True).astype(jnp.int32)
    keep = jnp.minimum(jnp.arange(n_pages, dtype=jnp.int32)[None], jnp.maximum(cnt - 1, 0)[:, None])
    order = jnp.take_along_axis(order, keep, axis=1)
    return jnp.take_along_axis(page_table, order, axis=1), order * PAGE, cnt


def _da_post_kernel(sc_ref, o0_ref, o1_ref, sub_ref, o_ref):
    o_ref[...] = _da_post(o0_ref[...], o1_ref[...], sc_ref[0], sub_ref[...])


def _da_post_call(o0, o1, sc, subln):
    return pl.pallas_call(
        _da_post_kernel,
        in_specs=[_smem_spec(), pl.BlockSpec(o0.shape, lambda: (0, 0)), pl.BlockSpec(o0.shape, lambda: (0, 0)),
                  pl.BlockSpec((1, LANE), lambda: (0, 0))],
        out_specs=pl.BlockSpec(o0.shape, lambda: (0, 0)),
        out_shape=jax.ShapeDtypeStruct(o0.shape, F32),
        name="da_post",
    )(sc, o0, o1, subln.reshape(1, LANE))


def _gate3_kernel(ng_ref, e_ref, oc_ref, os_ref, ow_ref, o_ref):
    gate = _sigmoid(ng_ref[...])
    acc = jnp.zeros(o_ref.shape, F32)
    for br, ref in enumerate((oc_ref, os_ref, ow_ref)):
        ge = jnp.dot(gate, e_ref[br], preferred_element_type=F32, precision=lax.Precision.HIGHEST)
        acc = acc + ge * ref[...]
    o_ref[...] = acc


def _gate3(ng, oc, os_, ow):
    e = np.zeros((3, LANE, NSA_HEADS * NSA_HD), np.float32)
    for br in range(3):
        for h in range(NSA_HEADS):
            e[br, 3 * h + br, NSA_HD * h:NSA_HD * (h + 1)] = 1.0
    return pl.pallas_call(
        _gate3_kernel,
        out_shape=jax.ShapeDtypeStruct(oc.shape, F32),
        name="gate3",
    )(ng, jnp.asarray(e), oc, os_, ow)


def _tail_a_kernel(x_ref, da_ref, nsa_ref, ga_ref, gb_ref, wo_ref, nx_ref, wxq_ref, h_ref, q_ref):
    m = _sigmoid(ga_ref[...]) * da_ref[...] + _sigmoid(gb_ref[...]) * nsa_ref[...]
    h = x_ref[...] + jnp.dot(m.astype(BF16), wo_ref[...], preferred_element_type=F32)
    h_ref[...] = h
    xn = _rms(h, nx_ref[...]).astype(BF16)
    q_ref[...] = jnp.dot(xn, wxq_ref[...], preferred_element_type=F32)


def _tail_a(x, o_da, o_nsa, z, w_o, norm_x, w_xq):
    m = x.shape[0]
    tm = min(m, 512)
    row = lambda cb: pl.BlockSpec((tm, D_MODEL), lambda i: (i, cb))
    const = lambda shape: pl.BlockSpec(shape, lambda i: (0, 0), pipeline_mode=pl.Buffered(1))
    return pl.pallas_call(
        _tail_a_kernel,
        grid=(m // tm,),
        in_specs=[row(0), row(0), row(0), row(C_GA // D_MODEL), row(C_GB // D_MODEL),
                  const((D_MODEL, D_MODEL)), const((1, D_MODEL)), const((D_MODEL, X_W))],
        out_specs=[row(0), pl.BlockSpec((tm, X_W), lambda i: (i, 0))],
        out_shape=[jax.ShapeDtypeStruct((m, D_MODEL), F32), jax.ShapeDtypeStruct((m, X_W), F32)],
        compiler_params=_cparams(("parallel",)),
        name="tail_merge_wo",
    )(x, o_da, o_nsa, z, z, w_o.astype(BF16), norm_x.reshape(1, D_MODEL), w_xq.astype(BF16))


def _cross_kernel(q_ref, mk_ref, mv_ref, o_ref):
    q = (q_ref[0] * (X_HD ** -0.5)).astype(BF16)
    mk = mk_ref[0].astype(BF16)
    mv = mv_ref[0].astype(BF16)
    outs = []
    for h in range(X_HEADS):
        sl = slice(X_HD * h, X_HD * (h + 1))
        s = lax.dot_general(q[:, sl], mk[:, sl], NT_DIMS, preferred_element_type=F32)
        e = jnp.exp(s - jnp.max(s, axis=-1, keepdims=True))
        p = e / jnp.sum(e, axis=-1, keepdims=True)
        outs.append(jnp.dot(p.astype(BF16), mv[:, sl], preferred_element_type=F32))
    o_ref[0] = jnp.concatenate(outs, axis=-1)


def _cross(q, mk, mv):
    b, t, _ = q.shape
    mlen = mk.shape[1]
    tt = min(t, 512)
    return pl.pallas_call(
        _cross_kernel,
        grid=(b, t // tt),
        in_specs=[pl.BlockSpec((1, tt, X_W), lambda bi, i: (bi, i, 0)),
                  pl.BlockSpec((1, mlen, X_W), lambda bi, i: (bi, 0, 0)),
                  pl.BlockSpec((1, mlen, X_W), lambda bi, i: (bi, 0, 0))],
        out_specs=pl.BlockSpec((1, tt, X_W), lambda bi, i: (bi, i, 0)),
        out_shape=jax.ShapeDtypeStruct((b, t, X_W), F32),
        compiler_params=_cparams(("parallel", "parallel")),
        name="cross_attn",
    )(q, mk, mv)


def _tail_c_kernel(h_ref, ox_ref, wxo_ref, nf_ref, wg_ref, wu_ref, wd_ref, nfin_ref, y_ref, *, nchunk, chunk):
    h = h_ref[...] + jnp.dot(ox_ref[...].astype(BF16), wxo_ref[...], preferred_element_type=F32)
    xn = _rms(h, nf_ref[...]).astype(BF16)
    acc = jnp.zeros(h.shape, F32)
    for c in range(nchunk):
        sl = slice(c * chunk, (c + 1) * chunk)
        gt = jnp.dot(xn, wg_ref[:, sl], preferred_element_type=F32)
        up = jnp.dot(xn, wu_ref[:, sl], preferred_element_type=F32)
        act = (gt * _sigmoid(gt) * up).astype(BF16)
        acc = acc + jnp.dot(act, wd_ref[sl, :], preferred_element_type=F32)
    y_ref[...] = _rms(h + acc, nfin_ref[...])


def _tail_c(h, ox, w_xo, norm_ffn, w_gate_up, w_down, norm_final):
    m = h.shape[0]
    tm = min(m, 512)
    hid = w_down.shape[0]
    chunk = hid // 2
    const = lambda shape: pl.BlockSpec(shape, lambda i: (0, 0), pipeline_mode=pl.Buffered(1))
    return pl.pallas_call(
        functools.partial(_tail_c_kernel, nchunk=2, chunk=chunk),
        grid=(m // tm,),
        in_specs=[pl.BlockSpec((tm, D_MODEL), lambda i: (i, 0)), pl.BlockSpec((tm, X_W), lambda i: (i, 0)),
                  const((X_W, D_MODEL)), const((1, D_MODEL)), const((D_MODEL, hid)), const((D_MODEL, hid)),
                  const((hid, D_MODEL)), const((1, D_MODEL))],
        out_specs=pl.BlockSpec((tm, D_MODEL), lambda i: (i, 0)),
        out_shape=jax.ShapeDtypeStruct((m, D_MODEL), F32),
        compiler_params=_cparams(("parallel",)),
        name="tail_ffn",
    )(h, ox, w_xo.astype(BF16), norm_ffn.reshape(1, D_MODEL), w_gate_up[:, :hid].astype(BF16),
      w_gate_up[:, hid:].astype(BF16), w_down.astype(BF16), norm_final.reshape(1, D_MODEL))


def _alibi(n):
    return np.asarray(2.0 ** (-8.0 * np.arange(1, n + 1) / n) * LOG2E, dtype=np.float32)


def _finish(x, o_da, o_nsa, z, mk, mv, batch, w_o, norm_x, w_xq, w_xo, norm_ffn, w_gate_up, w_down, norm_final):
    m = x.shape[0]
    t = m // batch
    h1, qx = _tail_a(x, o_da, o_nsa, z, w_o, norm_x, w_xq)
    q3 = qx.reshape(batch, t, X_W)
    tpad = -(-t // 8) * 8
    if tpad != t:
        q3 = jnp.pad(q3, ((0, 0), (0, tpad - t), (0, 0)))
    ox = _cross(q3, mk, mv)[:, :t].reshape(m, X_W)
    return _tail_c(h1, ox, w_xo, norm_ffn, w_gate_up, w_down, norm_final)


def kernel(x_prompt, x_sample, cache_diff_k, cache_diff_v, cache_cmp_k, cache_cmp_v, cache_sel_k, cache_sel_v,
           cache_win_k, cache_win_v, cache_mem_k, cache_mem_v, page_table, mem_prompt,
           norm_mix, w_in, lam_q1, lam_k1, lam_q2, lam_k2, da_subln,
           w_cmp_k1, pe_cmp_k, w_cmp_k2, w_cmp_v1, pe_cmp_v, w_cmp_v2,
           w_o, norm_x, w_xq, w_mem_kv, w_xo, norm_ffn, w_gate_up, w_down, norm_final):
    batch, seq, _ = x_prompt.shape
    db, ds, _ = x_sample.shape
    assert batch == 1 and norm_mix.shape[0] == 1
    n_pages = page_table.shape[1]
    past = n_pages * PAGE
    wb = cache_win_k.shape[2]
    kvw = NSA_GROUPS * NSA_HD

    lam = (jnp.exp(jnp.sum(lam_q1[0] * lam_k1[0])) - jnp.exp(jnp.sum(lam_q2[0] * lam_k2[0])) + LAM_INIT).astype(F32)
    da_sc = jnp.concatenate([lam.reshape(1), jnp.asarray(_alibi(DA_HEADS))])
    nsa_sl = jnp.asarray(_alibi(NSA_HEADS))
    w_pad = _prep_w_in(w_in[0])
    tail_w = (w_o[0], norm_x[0], w_xq[0], w_xo[0], norm_ffn[0], w_gate_up[0], w_down[0], norm_final)
    cmp_k = (w_cmp_k1[0], pe_cmp_k[0], w_cmp_k2[0])
    cmp_v = (w_cmp_v1[0], pe_cmp_v[0], w_cmp_v2[0])

    xp = x_prompt.reshape(seq, D_MODEL)
    z, zb = _inproj(xp, norm_mix[0], w_pad)
    o_da = _da_prompt(zb, da_sc, da_subln[0])

    ident = jnp.arange(seq // PAGE, dtype=jnp.int32).reshape(1, -1)
    zero_tail = jnp.zeros((1, 8, CMP_STRIDE * kvw), F32)
    p_kc = z[:, C_KC:C_KC + kvw]
    p_vc = z[:, C_VC:C_VC + kvw]
    as_pages = lambda a: a.reshape(seq // PAGE, PAGE, kvw).transpose(0, 2, 1)
    kcb = _compress(as_pages(p_kc), ident, zero_tail, *cmp_k)
    vcb = _compress(as_pages(p_vc), ident, zero_tail, *cmp_v)
    nch = seq // CMP_STRIDE
    oc, sel, anyblk = _cmp_topk(zb, C_NQ // (NSA_HPG * NSA_HD), kcb, vcb, nsa_sl, batch=1, sq=seq, tq=128,
                                nblk=nch - 1, nsel=seq // SEL_BLOCK, qpos_base=0)
    o_nsa = _nsa_sw_prompt(z, zb, sel, anyblk, oc, nsa_sl)

    mem_kv = _matmul(mem_prompt.reshape(-1, D_MODEL), w_mem_kv[0].astype(BF16))
    p_mk, p_mv = mem_kv[:, :X_W], mem_kv[:, X_W:]
    y_prompt = _finish(xp, o_da, o_nsa, z, p_mk[None], p_mv[None], 1, *tail_w)

    r5 = lambda a, h: a.reshape(1, 1, a.shape[0], h, -1)
    p_states = (r5(z[:, C_DK:C_DK + 1024], DA_HEADS), r5(z[:, C_DV:C_DV + 1024], DA_HEADS),
                r5(p_kc, NSA_GROUPS), r5(p_vc, NSA_GROUPS),
                r5(z[:, C_KS:C_KS + kvw], NSA_GROUPS), r5(z[:, C_VS:C_VS + kvw], NSA_GROUPS),
                r5(z[seq - min(WINDOW, seq):, C_KW:C_KW + kvw], NSA_GROUPS),
                r5(z[seq - min(WINDOW, seq):, C_VW:C_VW + kvw], NSA_GROUPS),
                r5(p_mk, X_HEADS), r5(p_mv, X_HEADS))

    ms = db * ds
    xs = x_sample.reshape(ms, D_MODEL)
    zs, zsb = _inproj(xs, norm_mix[0], w_pad)
    z3 = zs.reshape(db, ds, ZP)
    pad_rows = lambda a, n: jnp.pad(a, ((0, 0), (0, n - a.shape[1]), (0, 0)))

    dq = z3[:, :, C_DQ:C_DQ + 1024].reshape(db, ds, DA_HEADS, 2, DA_HD)
    wq_da = jnp.einsum('bqhmd,mn->bhmqnd', dq, jnp.eye(2, dtype=F32))
    wq_da = wq_da.reshape(db, DA_HEADS * 2 * ds, 2 * DA_HD).astype(BF16)
    ci = np.zeros((DA_HEADS * 2 * ds, LANE), np.float32)
    ci[:, 0] = np.repeat(_alibi(DA_HEADS), 2 * ds)
    ci[:, 1] = np.tile(np.arange(ds), DA_HEADS * 2)
    ci[:, 2] = np.repeat(np.arange(DA_HEADS), 2 * ds)
    s_dk, s_dv = z3[:, :, C_DK:C_DK + 1024], z3[:, :, C_DV:C_DV + 1024]
    head_rows = lambda a: pad_rows(a.reshape(db, ds * DA_HEADS, 2 * DA_HD), PAGE)
    o_pair = _decode_attn(wq_da, jnp.asarray(ci), cache_diff_k[0].reshape(-1, PAGE * DA_HEADS, 2 * DA_HD),
                          cache_diff_v[0].reshape(-1, PAGE * DA_HEADS, 2 * DA_HD), *_all_pages(page_table, 0),
                          head_rows(s_dk), head_rows(s_dv), None, hk=DA_HEADS, qpos0=past, win=1e9, n_new=ds, pp=8)
    o_pair = o_pair.reshape(db, DA_HEADS, 2, ds, 2 * DA_HD).transpose(2, 0, 3, 1, 4)
    o_da_s = _da_post_call(o_pair[0].reshape(ms * DA_HEADS, LANE), o_pair[1].reshape(ms * DA_HEADS, LANE),
                           da_sc, da_subln[0]).reshape(ms, DA_HEADS * LANE)

    ck = CMP_STRIDE * kvw
    s_kc, s_vc = z3[:, :, C_KC:C_KC + kvw], z3[:, :, C_VC:C_VC + kvw]
    tail_of = lambda a: jnp.pad(a.reshape(db, 1, ds * kvw), ((0, 0), (0, 7), (0, ck - ds * kvw)))
    feat_major = lambda c: c[0].transpose(0, 2, 3, 1).reshape(-1, kvw, PAGE)
    kcb_s = _compress(feat_major(cache_cmp_k), page_table, tail_of(s_kc), *cmp_k)
    vcb_s = _compress(feat_major(cache_cmp_v), page_table, tail_of(s_vc), *cmp_v)
    tq_s = 16
    nq_pad = pad_rows(zsb.reshape(db, ds, ZP)[:, :, C_NQ:C_NQ + 1024], tq_s).reshape(db * tq_s, 1024)
    nsel_s = -(-(past + ds) // SEL_BLOCK)
    oc_s, sel_s, _ = _cmp_topk(nq_pad, 0, kcb_s, vcb_s, nsa_sl, batch=db, sq=tq_s, tq=tq_s,
                               nblk=(past + ds + CMP_STRIDE - 1) // CMP_STRIDE - 1, nsel=nsel_s, qpos_base=past)
    oc_s = oc_s.reshape(db, tq_s, 1024)[:, :ds].reshape(ms, 1024)

    nq = z3[:, :, C_NQ:C_NQ + 1024].reshape(db, ds, NSA_GROUPS, NSA_HPG, NSA_HD)
    wq_n = jnp.einsum('bqghd,gk->bghqkd', nq, jnp.eye(NSA_GROUPS, dtype=F32))
    wq_n = wq_n.reshape(db, NSA_HEADS * ds, kvw).astype(BF16)
    cn = np.zeros((NSA_HEADS * ds, LANE), np.float32)
    cn[:, 0] = np.repeat(_alibi(NSA_HEADS), ds)
    cn[:, 1] = np.tile(np.arange(ds), NSA_HEADS)
    cn = jnp.asarray(cn)
    sel_cols = jnp.repeat(sel_s[:, :, None, :ds, :], NSA_HPG, axis=2).reshape(db, NSA_HEADS * ds, -1)
    s_ks, s_vs = z3[:, :, C_KS:C_KS + kvw], z3[:, :, C_VS:C_VS + kvw]
    s_kw, s_vw = z3[:, :, C_KW:C_KW + kvw], z3[:, :, C_VW:C_VW + kvw]
    o_sel = _decode_attn(wq_n, cn, feat_major(cache_sel_k), feat_major(cache_sel_v),
                         *_selected_pages(page_table, sel_cols), pad_rows(s_ks, PAGE), pad_rows(s_vs, PAGE), sel_cols,
                         hk=1, qpos0=past, win=1e9, n_new=ds, pp=16, feature_major=True)
    win_pages = wb // PAGE
    win_pt = jnp.arange(db * win_pages, dtype=jnp.int32).reshape(db, win_pages)
    o_win = _decode_attn(wq_n, cn, cache_win_k[0].reshape(-1, PAGE, kvw), cache_win_v[0].reshape(-1, PAGE, kvw),
                         *_all_pages(win_pt, past - wb), pad_rows(s_kw, PAGE), pad_rows(s_vw, PAGE), None,
                         hk=1, qpos0=past, win=WINDOW, n_new=ds, pp=win_pages)

    def own_group(o):
        o = o.reshape(db, NSA_GROUPS, NSA_HPG, ds, NSA_GROUPS, NSA_HD)
        o = jnp.stack([o[:, g, :, :, g] for g in range(NSA_GROUPS)], axis=1)
        return o.transpose(0, 3, 1, 2, 4).reshape(ms, NSA_HEADS * NSA_HD)

    o_nsa_s = _gate3(zs[:, C_NG:C_NG + LANE], oc_s, own_group(o_sel), own_group(o_win))
    s_mk = cache_mem_k[0].reshape(db, -1, X_W)
    s_mv = cache_mem_v[0].reshape(db, -1, X_W)
    y_sample = _finish(xs, o_da_s, o_nsa_s, zs, s_mk, s_mv, db, *tail_w)

    s5 = lambda a, h: a.reshape(1, db, a.shape[1], h, -1)
    new_win = lambda c, a: jnp.concatenate([c[0].reshape(db, wb, kvw), a], axis=1)[:, ds:]
    s_states = (s5(s_dk, DA_HEADS), s5(s_dv, DA_HEADS), s5(s_kc, NSA_GROUPS), s5(s_vc, NSA_GROUPS),
                s5(s_ks, NSA_GROUPS), s5(s_vs, NSA_GROUPS),
                s5(new_win(cache_win_k, s_kw), NSA_GROUPS), s5(new_win(cache_win_v, s_vw), NSA_GROUPS))

    return (y_prompt.reshape(1, seq, D_MODEL), y_sample.reshape(db, ds, D_MODEL)) + p_states + s_states
```

```python
import functools

import numpy as np
import jax
import jax.numpy as jnp
from jax import lax
from jax.experimental import pallas as pl
from jax.experimental.pallas import tpu as pltpu

F32 = jnp.float32
BF16 = jnp.bfloat16

D_MODEL = 1024
DA_HEADS = 8
DA_HD = 64
NSA_HEADS = 16
NSA_GROUPS = 2
NSA_HPG = NSA_HEADS // NSA_GROUPS
NSA_HD = 64
CMP_LEN = 32
CMP_STRIDE = 16
SEL_BLOCK = 64
SEL_TOP = 16
WINDOW = 512
X_HEADS = 4
X_HD = 64
X_W = X_HEADS * X_HD
EPS = 1e-6
NEG = -1e30
PICKED = -3e38
FORCE_BONUS = 1e6
LAM_INIT = 0.2
LANE = 128
PAGE = 128
VMEM_LIMIT = 56 * 1024 * 1024

C_DQ, C_DK, C_DV, C_NQ, C_GA, C_GB = 0, 1024, 2048, 3072, 4096, 5120
C_KC, C_VC, C_KS, C_VS, C_KW, C_VW, C_NG = 6144, 6272, 6400, 6528, 6656, 6784, 6912
ZP = 7040
ZP_TILE = 1408

NT_DIMS = (((1,), (1,)), ((), ()))
LOG2E = 1.4426950408889634
Q_SCALE = DA_HD ** -0.5 * LOG2E


def _cparams(sem):
    return pltpu.CompilerParams(dimension_semantics=sem, vmem_limit_bytes=VMEM_LIMIT)


def _smem_spec():
    return pl.BlockSpec(memory_space=pltpu.SMEM)


def _pages_per_step(n_pages, cap):
    return max(p for p in range(1, cap + 1) if n_pages % p == 0)


def _sigmoid(x):
    return 1.0 / (1.0 + jnp.exp(-x))


def _rms(x, g):
    return x * lax.rsqrt(jnp.mean(x * x, axis=-1, keepdims=True) + EPS) * g


def _inproj_kernel(x_ref, g_ref, w_ref, o_ref, ob_ref, xn_ref):
    @pl.when(pl.program_id(1) == 0)
    def _():
        xn_ref[...] = _rms(x_ref[...], g_ref[...]).astype(BF16)

    acc = jnp.dot(xn_ref[...], w_ref[...], preferred_element_type=F32)
    o_ref[...] = acc
    ob_ref[...] = acc.astype(BF16)


def _inproj(x, g, w_pad):
    m = x.shape[0]
    tm = min(m, 512)
    return pl.pallas_call(
        _inproj_kernel,
        grid=(m // tm, ZP // ZP_TILE),
        in_specs=[pl.BlockSpec((tm, D_MODEL), lambda i, j: (i, 0)),
                  pl.BlockSpec((1, D_MODEL), lambda i, j: (0, 0)),
                  pl.BlockSpec((D_MODEL, ZP_TILE), lambda i, j: (0, j))],
        out_specs=[pl.BlockSpec((tm, ZP_TILE), lambda i, j: (i, j)),
                   pl.BlockSpec((tm, ZP_TILE), lambda i, j: (i, j))],
        out_shape=[jax.ShapeDtypeStruct((m, ZP), F32), jax.ShapeDtypeStruct((m, ZP), BF16)],
        scratch_shapes=[pltpu.VMEM((tm, D_MODEL), BF16)],
        compiler_params=_cparams(("parallel", "arbitrary")),
        name="inproj",
    )(x, g.reshape(1, D_MODEL), w_pad)


def _prep_w_in(w_in):
    a = jnp.concatenate([w_in[:, :1024] * Q_SCALE, w_in[:, 1024:3072], w_in[:, 3072:4096] * Q_SCALE], axis=1)
    kv = w_in[:, 4096:4864]
    ng = w_in[:, 4864:4912]
    mg = w_in[:, 4912:]
    return jnp.concatenate([a, mg, kv, jnp.pad(ng, ((0, 0), (0, LANE - ng.shape[1])))], axis=1).astype(BF16)


def _matmul_kernel(x_ref, w_ref, o_ref):
    o_ref[...] = jnp.dot(x_ref[...].astype(BF16), w_ref[...], preferred_element_type=F32)


def _matmul(x, w_bf16):
    m, n = x.shape[0], w_bf16.shape[1]
    return pl.pallas_call(
        _matmul_kernel,
        out_shape=jax.ShapeDtypeStruct((m, n), F32),
        compiler_params=pltpu.CompilerParams(vmem_limit_bytes=VMEM_LIMIT),
        name="matmul",
    )(x, w_bf16)


def _da_post(o0, o1, lam, sub):
    o = o0 - lam * o1
    return _rms(o, sub) * (1.0 - LAM_INIT)


def _split3(x):
    a = x.astype(BF16).astype(F32)
    r = x - a
    b = r.astype(BF16).astype(F32)
    return a, b, (r - b).astype(BF16).astype(F32)


def _thirds(lane, pieces):
    which = lane % 3
    return jnp.where(which == 0, pieces[0], jnp.where(which == 1, pieces[1], pieces[2]))


def _da_prompt_kernel(sc_ref, q_ref, k_ref, v_ref, sub_ref, o_ref, ka_ref, va_ref, sa_ref, sb_ref, m_ref, acc_ref,
                      *, tq, tkb):
    h = pl.program_id(0)
    qi = pl.program_id(1)
    lam = sc_ref[0]
    slope = sc_ref[1 + h]
    per_big = tkb // tq
    seq = ka_ref.shape[0]

    @pl.when(qi == 0)
    def _():
        j = lax.broadcasted_iota(jnp.int32, (tkb, LANE), 0)
        grp = lax.broadcasted_iota(jnp.int32, (tkb, LANE), 1) // 3
        kfeat = jnp.where(grp == 0, (j // 32) * 32, jnp.where(grp == 1, j % 32, jnp.where(grp == 2, 1, 0)))
        kfeat = kfeat.astype(F32).astype(BF16)
        ka_ref[:, :LANE] = k_ref[...]
        for c in range(seq // tkb):
            ka_ref[c * tkb:(c + 1) * tkb, LANE:] = kfeat
        va_ref[:, :LANE] = v_ref[...]
        va_ref[:, LANE:] = jnp.ones((seq, LANE), BF16)

    q = q_ref[...]
    lane = lax.broadcasted_iota(jnp.int32, (tq, LANE), 1)
    zero = jnp.zeros_like(q)
    qbd = jnp.concatenate([jnp.where(lane < DA_HD, q, zero), jnp.where(lane >= DA_HD, q, zero)], axis=0)
    lane2 = lax.broadcasted_iota(jnp.int32, (2 * tq, LANE), 1)
    i_loc = (lax.broadcasted_iota(jnp.int32, (2 * tq, LANE), 0) & (tq - 1)).astype(F32)
    sl_pieces = _thirds(lane2, _split3(jnp.full((2 * tq, LANE), slope, F32)))
    row_pieces = _thirds(lane2, _split3(-slope * i_loc))
    qfeat = jnp.where(lane2 < 6, sl_pieces, jnp.where(lane2 < 9, row_pieces, 0.0))
    qaug = jnp.concatenate([qbd, qfeat.astype(BF16)], axis=1)
    m_ref[...] = jnp.full(m_ref.shape, NEG, F32)
    acc_ref[...] = jnp.zeros(acc_ref.shape, F32)

    def scores(idx):
        k = ka_ref[pl.ds(pl.multiple_of(idx * tkb, tkb), tkb), :]
        return lax.dot_general(qaug, k, NT_DIMS, preferred_element_type=F32)

    def consume(idx, s_ref, masked):
        base = qi * tq - idx * tkb
        cc = slope * base.astype(F32)
        va = va_ref[pl.ds(pl.multiple_of(idx * tkb, tkb), tkb), :]
        m_old = m_ref[...]
        if masked:
            ii = lax.broadcasted_iota(jnp.int32, s_ref.shape, 0) & (tq - 1)
            jj = lax.broadcasted_iota(jnp.int32, s_ref.shape, 1)
            t = jnp.where(jj - ii <= base, s_ref[...], NEG)
            m_new = jnp.maximum(m_old, jnp.max(t, axis=-1, keepdims=True) - cc)
            p = jnp.exp2(t - (m_new + cc))
        else:
            m_new = jnp.maximum(m_old, jnp.max(s_ref[...], axis=-1, keepdims=True) - cc)
            p = jnp.exp2(s_ref[...] - (m_new + cc))
        alpha = jnp.exp2(m_old - m_new)
        acc_ref[...] = alpha * acc_ref[...] + jnp.dot(p.astype(BF16), va, preferred_element_type=F32)
        m_ref[...] = m_new

    nbig = qi // per_big
    sa_ref[...] = scores(0)

    def half(idx, cur_ref, nxt_ref):
        @pl.when(idx < nbig)
        def _():
            nxt_ref[...] = scores(idx + 1)
            consume(idx, cur_ref, False)

        @pl.when(idx == nbig)
        def _():
            consume(idx, cur_ref, True)

    def pair(j, carry):
        half(2 * j, sa_ref, sb_ref)
        half(2 * j + 1, sb_ref, sa_ref)
        return carry

    lax.fori_loop(0, nbig // 2 + 1, pair, 0)

    o0 = acc_ref[0:tq, :LANE] / acc_ref[0:tq, LANE:]
    o1 = acc_ref[tq:2 * tq, :LANE] / acc_ref[tq:2 * tq, LANE:]
    o_ref[...] = _da_post(o0, o1, lam, sub_ref[...])


def _da_prompt(zb, sc, subln, tq=512, tkb=1024):
    s = zb.shape[0]
    tq = min(tq, s)
    tkb = min(tkb, s)
    kb, vb = C_DK // LANE, C_DV // LANE
    return pl.pallas_call(
        functools.partial(_da_prompt_kernel, tq=tq, tkb=tkb),
        grid=(DA_HEADS, s // tq),
        in_specs=[_smem_spec(),
                  pl.BlockSpec((tq, LANE), lambda h, i: (i, h)),
                  pl.BlockSpec((s, LANE), lambda h, i: (0, kb + h), pipeline_mode=pl.Buffered(1)),
                  pl.BlockSpec((s, LANE), lambda h, i: (0, vb + h), pipeline_mode=pl.Buffered(1)),
                  pl.BlockSpec((1, LANE), lambda h, i: (0, 0))],
        out_specs=pl.BlockSpec((tq, LANE), lambda h, i: (i, h)),
        out_shape=jax.ShapeDtypeStruct((s, DA_HEADS * LANE), F32),
        scratch_shapes=[pltpu.VMEM((s, 2 * LANE), BF16),
                        pltpu.VMEM((s, 2 * LANE), BF16),
                        pltpu.VMEM((2 * tq, tkb), F32), pltpu.VMEM((2 * tq, tkb), F32),
                        pltpu.VMEM((2 * tq, 1), F32), pltpu.VMEM((2 * tq, 2 * LANE), F32)],
        compiler_params=_cparams(("arbitrary", "arbitrary")),
        name="da_prompt",
    )(sc, zb, zb, zb, subln.reshape(1, LANE))


def _compress_kernel(pt_ref, *refs, pp, nch):
    del pt_ref
    pages = refs[:pp]
    tail_ref, wbig_ref, w1_ref, pe_ref, w2_ref, out_ref, ab_ref, rows_ref, x_ref = refs[pp:]
    s = pl.program_id(1)
    for k in range(pp):
        rows_ref[k] = pages[k][0].T
    for k in range(pp):
        for pos in range(CMP_STRIDE):
            x_ref[8 * k:8 * (k + 1), LANE * pos:LANE * (pos + 1)] = rows_ref[k, pl.ds(pos, 8, stride=CMP_STRIDE), :]
    x = x_ref[...].astype(BF16)
    rows = 8 * pp
    ab_ref[pl.ds(pl.multiple_of(s * rows, rows), rows), :] = jnp.dot(x, wbig_ref[...], preferred_element_type=F32)

    @pl.when(s == pl.num_programs(1) - 1)
    def _():
        ab_ref[nch:nch + 8, :] = jnp.dot(tail_ref[0].astype(BF16), wbig_ref[...], preferred_element_type=F32)
        hpe = jnp.dot(pe_ref[...].astype(BF16), w1_ref[...], preferred_element_type=F32)[0:1]
        outs = []
        hid = 2 * NSA_HD
        for g in range(NSA_GROUPS):
            a = ab_ref[0:nch, 2 * hid * g:2 * hid * g + hid]
            b = ab_ref[1:nch + 1, 2 * hid * g + hid:2 * hid * (g + 1)]
            hd = a + b + hpe
            act = hd * _sigmoid(hd)
            outs.append(jnp.dot(act.astype(BF16), w2_ref[...], preferred_element_type=F32))
        out_ref[0] = jnp.concatenate(outs, axis=-1)


def _compress(pool, page_table, tail, w1, pe, w2):
    b, n_pages = page_table.shape
    nch = n_pages * 8
    pp = _pages_per_step(n_pages, 16)
    ck = CMP_STRIDE * NSA_GROUPS * NSA_HD
    hid = 2 * NSA_HD
    w1r = w1.reshape(2, CMP_STRIDE, NSA_HD, hid)
    wbig = jnp.einsum('psdh,gk->sgdkph', w1r, jnp.eye(NSA_GROUPS, dtype=F32)).reshape(ck, 2 * NSA_GROUPS * hid).astype(BF16)
    pe8 = jnp.pad(pe.reshape(1, CMP_LEN * NSA_HD), ((0, 7), (0, 0)))

    def page_spec(k):
        return pl.BlockSpec((1, LANE, PAGE), lambda bi, s, pt: (pt[bi * n_pages + s * pp + k], 0, 0))

    const = lambda shape: pl.BlockSpec(shape, lambda bi, s, pt: tuple(0 for _ in shape))
    grid_spec = pltpu.PrefetchScalarGridSpec(
        num_scalar_prefetch=1,
        grid=(b, n_pages // pp),
        in_specs=[page_spec(k) for k in range(pp)] + [
            pl.BlockSpec((1, 8, ck), lambda bi, s, pt: (bi, 0, 0)),
            const((ck, 2 * NSA_GROUPS * hid)), const((CMP_LEN * NSA_HD, hid)), const((8, CMP_LEN * NSA_HD)),
            const((hid, NSA_HD))],
        out_specs=pl.BlockSpec((1, nch, NSA_GROUPS * NSA_HD), lambda bi, s, pt: (bi, 0, 0)),
        scratch_shapes=[pltpu.VMEM((nch + 8, 2 * NSA_GROUPS * hid), F32), pltpu.VMEM((pp, PAGE, LANE), F32),
                        pltpu.VMEM((8 * pp, ck), F32)],
    )
    return pl.pallas_call(
        functools.partial(_compress_kernel, pp=pp, nch=nch),
        grid_spec=grid_spec,
        out_shape=jax.ShapeDtypeStruct((b, nch, NSA_GROUPS * NSA_HD), F32),
        compiler_params=_cparams(("arbitrary", "arbitrary")),
        name="compress",
    )(page_table.reshape(-1), *([pool] * pp), tail, wbig, w1.astype(BF16), pe8, w2.astype(BF16))


def _stack_group_queries(q, g, tq):
    lane = lax.broadcasted_iota(jnp.int32, (tq, 2 * NSA_HD), 1)
    mine = jnp.where(lane >= NSA_HD, 1, 0) == g
    parts = []
    for hh in range(NSA_HPG):
        qh = q[:, NSA_HD * hh:NSA_HD * (hh + 1)]
        parts.append(jnp.where(mine, jnp.concatenate([qh, qh], axis=1), jnp.zeros((tq, 2 * NSA_HD), q.dtype)))
    return jnp.concatenate(parts, axis=0)


def _group_half(x, g):
    return jnp.where(g == 0, x[:, :NSA_HD], x[:, NSA_HD:])


def _cmp_topk_kernel(sl_ref, q_ref, kc_ref, vc_ref, agg_ref, oc_ref, sel_ref, any_ref, *, tq, nb, nselp, qpos_base, topk,
                     levels):
    g = pl.program_id(1)
    t = pl.program_id(2)
    q0 = qpos_base + t * tq
    qpad = _stack_group_queries(q_ref[...], g, tq)

    def work(nbw, nsw):
        kcb = kc_ref[0, :nbw, :].astype(BF16)
        vcb = vc_ref[0, :nbw, :].astype(BF16)
        s_all = lax.dot_general(qpad, kcb, NT_DIMS, preferred_element_type=F32)
        i = lax.broadcasted_iota(jnp.int32, (tq, nbw), 0)
        n = lax.broadcasted_iota(jnp.int32, (tq, nbw), 1)
        dist = (q0 + i - (CMP_STRIDE * n + (CMP_LEN - 1))).astype(F32)
        mask = dist >= 0
        row_ok = dist[:, 0:1] >= 0.0
        psum = jnp.zeros((tq, nbw), F32)
        for hh in range(NSA_HPG):
            slope = sl_ref[NSA_HPG * g + hh]
            tt = jnp.where(mask, s_all[hh * tq:(hh + 1) * tq] - slope * dist, NEG)
            m = jnp.max(tt, axis=-1, keepdims=True)
            e = jnp.exp2(tt - m)
            l = jnp.sum(e, axis=-1, keepdims=True)
            p = e * jnp.where(row_ok, 1.0 / l, 0.0)
            psum = psum + p
            o = jnp.dot(p.astype(BF16), vcb, preferred_element_type=F32)
            oc_ref[:, NSA_HD * hh:NSA_HD * (hh + 1)] = _group_half(o, g)

        p_hi = psum.astype(BF16)
        p_lo = (psum - p_hi.astype(F32)).astype(BF16)
        agg = agg_ref[:nbw, :nsw]
        imp = jnp.dot(p_hi, agg, preferred_element_type=F32) + jnp.dot(p_lo, agg, preferred_element_type=F32)
        jj = lax.broadcasted_iota(jnp.int32, (tq, nsw), 1)
        cur = (q0 + lax.broadcasted_iota(jnp.int32, (tq, nsw), 0)) // SEL_BLOCK
        valid = jj <= cur
        forced = jnp.where(valid, jnp.where(jj == 0, 1, jnp.where(jj >= cur - 1, 1, 0)), 0)
        score = jnp.where(valid, imp + jnp.where(forced == 1, FORCE_BONUS, 0.0), NEG)

        flip = tq % LANE == 0
        ax = 0 if flip else 1
        cand = lax.broadcasted_iota(jnp.int32, (nsw, tq), 0) if flip else jj

        def pick(_, carry):
            sc, chosen = carry
            mx = jnp.max(sc, axis=ax, keepdims=True)
            idx = jnp.min(jnp.where(sc == mx, cand, nsw), axis=ax, keepdims=True)
            hit = cand == idx
            return jnp.where(hit, PICKED, sc), jnp.where(hit, 1.0, chosen)

        start = score.T if flip else score
        _, chosen = lax.fori_loop(0, topk, pick, (start, jnp.zeros(start.shape, F32)))
        sel = jnp.where(valid, chosen.T if flip else chosen, 0.0)
        if nsw < nselp:
            sel = jnp.concatenate([sel, jnp.zeros((tq, nselp - nsw), F32)], axis=1)
        sel_ref[0, 0] = sel
        any_ref[0, 0, 0] = jnp.broadcast_to(jnp.max(sel, axis=0, keepdims=True), (8, nselp))

    if levels == 1:
        work(nb, nselp)
    else:
        unit = nselp // levels
        need = (q0 + tq - 1) // SEL_BLOCK + 1
        lvl = (need + unit - 1) // unit
        for lv in range(1, levels + 1):
            @pl.when(lvl == lv)
            def _():
                work(nb * lv // levels, unit * lv)


def _sel_agg_matrix(nblk, nsel, nb, nselp):
    m = np.zeros((nb, nselp), np.float32)
    j = np.arange(nsel)
    r, c = SEL_BLOCK // CMP_STRIDE, CMP_LEN // CMP_STRIDE
    for a in range(r):
        for b in range(c):
            i = r * j + a - b
            ok = (i >= 0) & (i < nblk)
            np.add.at(m, (i[ok], j[ok]), 1.0)
    return jnp.asarray(m, dtype=BF16)


def _cmp_topk(qarr, qcol0, kcb, vcb, slopes, *, batch, sq, tq, nblk, nsel, qpos_base):
    nb = kcb.shape[1]
    nselp = -(-nsel // LANE) * LANE
    nt = sq // tq
    agg = _sel_agg_matrix(nblk, nsel, nb, nselp)
    gw = NSA_HPG * NSA_HD
    ratio = SEL_BLOCK // CMP_STRIDE
    levels = 4 if (qpos_base == 0 and nb == ratio * nselp and nb % (4 * LANE) == 0 and sq >= SEL_BLOCK * nselp) else 1
    return pl.pallas_call(
        functools.partial(_cmp_topk_kernel, tq=tq, nb=nb, nselp=nselp, qpos_base=qpos_base, topk=min(SEL_TOP, nsel),
                          levels=levels),
        grid=(batch, NSA_GROUPS, nt),
        in_specs=[_smem_spec(),
                  pl.BlockSpec((tq, gw), lambda b, g, t: (b * nt + t, qcol0 + g)),
                  pl.BlockSpec((1, nb, LANE), lambda b, g, t: (b, 0, 0)),
                  pl.BlockSpec((1, nb, LANE), lambda b, g, t: (b, 0, 0)),
                  pl.BlockSpec((nb, nselp), lambda b, g, t: (0, 0))],
        out_specs=[pl.BlockSpec((tq, gw), lambda b, g, t: (b * nt + t, g)),
                   pl.BlockSpec((1, 1, tq, nselp), lambda b, g, t: (b, g, t, 0)),
                   pl.BlockSpec((1, 1, 1, 8, nselp), lambda b, g, t: (b, g, t, 0, 0))],
        out_shape=[jax.ShapeDtypeStruct((batch * sq, NSA_GROUPS * gw), F32),
                   jax.ShapeDtypeStruct((batch, NSA_GROUPS, sq, nselp), F32),
                   jax.ShapeDtypeStruct((batch, NSA_GROUPS, nt, 8, nselp), F32)],
        compiler_params=_cparams(("arbitrary", "arbitrary", "arbitrary")),
        name="cmp_topk",
    )(slopes, qarr, kcb, vcb, agg)


def _nsa_sw_kernel(fl_ref, sl_ref, q_ref, ks_ref, vs_ref, kw_ref, vw_ref, sel_ref, oc_ref, ng_ref, out_ref,
                   ids_ref, m_ref, acc_ref, *, tq, nt, nselp, nwords, sb):
    g = pl.program_id(0)
    t = pl.program_id(1)
    rows = NSA_HPG * tq
    qpad = _stack_group_queries(q_ref[...], g, tq)
    slope_row = jnp.concatenate([jnp.full((tq, 1), sl_ref[NSA_HPG * g + hh], F32) for hh in range(NSA_HPG)], axis=0)
    i_loc = lax.broadcasted_iota(jnp.int32, (rows, LANE), 0) & (tq - 1)
    j_loc = lax.broadcasted_iota(jnp.int32, (rows, LANE), 1)
    dloc = (i_loc - j_loc).astype(F32)
    sl_dloc = slope_row * dloc
    selb = sel_ref[0, 0].astype(BF16)
    ones = jnp.ones((LANE, LANE), BF16)

    def chunk(ref, c):
        return ref[pl.ds(pl.multiple_of(jnp.maximum(c, 0) * LANE, LANE), LANE), :]

    def with_ones(v):
        return jnp.concatenate([v, jnp.concatenate([ones] * (v.shape[0] // LANE), axis=0)], axis=1)

    def sel_mask(c):
        jrow = lax.broadcasted_iota(jnp.int32, (nselp, LANE), 0)
        r = lax.broadcasted_iota(jnp.int32, (nselp, LANE), 1)
        expand = jnp.where(jrow == 2 * c + r // SEL_BLOCK, 1.0, 0.0).astype(BF16)
        return jnp.dot(selb, expand, preferred_element_type=F32)

    def per_head(mk, x):
        x3 = x.reshape(NSA_HPG, tq, LANE)
        return jnp.where(mk[None] > 0.5, x3, NEG).reshape(rows, LANE)

    s = lax.dot_general(qpad, chunk(ks_ref, t), NT_DIMS, preferred_element_type=F32)
    mk = jnp.where(dloc[:tq] >= 0.0, sel_mask(t), 0.0)
    tt = per_head(mk, s - sl_dloc)
    m0 = jnp.max(tt, axis=-1, keepdims=True)
    p = jnp.exp2(tt - m0)
    m_ref[...] = m0
    acc_ref[...] = jnp.dot(p.astype(BF16), with_ones(chunk(vs_ref, t)), preferred_element_type=F32)

    def scan(c, cnt):
        word = fl_ref[(g * nt + t) * nwords + c // 32]
        bit = lax.shift_right_logical(word, c % 32) & 1

        @pl.when(bit == 1)
        def _():
            ids_ref[cnt] = c

        return cnt + bit

    cnt = lax.fori_loop(0, t, scan, 0)
    for k in range(sb):
        ids_ref[cnt + k] = -1

    def sel_step(si, carry):
        cs = [ids_ref[si * sb + k] for k in range(sb)]
        kk = jnp.concatenate([chunk(ks_ref, c) for c in cs], axis=0)
        vv = with_ones(jnp.concatenate([chunk(vs_ref, c) for c in cs], axis=0))
        s = lax.dot_general(qpad, kk, NT_DIMS, preferred_element_type=F32)
        slabs = []
        for k, c in enumerate(cs):
            bias = sl_dloc + slope_row * ((t - c) * tq).astype(F32)
            slabs.append(per_head(sel_mask(c), s[:, k * LANE:(k + 1) * LANE] - bias))
        tt = jnp.concatenate(slabs, axis=1)
        m_old = m_ref[...]
        m_new = jnp.maximum(m_old, jnp.max(tt, axis=-1, keepdims=True))
        alpha = jnp.exp2(m_old - m_new)
        p = jnp.exp2(tt - m_new)
        acc_ref[...] = alpha * acc_ref[...] + jnp.dot(p.astype(BF16), vv, preferred_element_type=F32)
        m_ref[...] = m_new
        return carry

    lax.fori_loop(0, (cnt + sb - 1) // sb, sel_step, 0)
    o_s = acc_ref[:, :LANE] / acc_ref[:, LANE:]

    nback = WINDOW // tq
    c0 = jnp.maximum(t - nback, 0)
    wlen = (nback + 1) * LANE
    wstart = pl.multiple_of(c0 * LANE, LANE)
    s = lax.dot_general(qpad, kw_ref[pl.ds(wstart, wlen), :], NT_DIMS, preferred_element_type=F32)
    slabs = []
    for k in range(nback + 1):
        dist = dloc + ((t - (c0 + k)) * tq).astype(F32)
        x = s[:, k * LANE:(k + 1) * LANE] - slope_row * dist
        slabs.append(jnp.where(dist >= 0.0, jnp.where(dist < float(WINDOW), x, NEG), NEG))
    tt = jnp.concatenate(slabs, axis=1)
    p = jnp.exp2(tt - jnp.max(tt, axis=-1, keepdims=True))
    aw = jnp.dot(p.astype(BF16), with_ones(vw_ref[pl.ds(wstart, wlen), :]), preferred_element_type=F32)
    o_w = aw[:, :LANE] / aw[:, LANE:]

    gate = _sigmoid(ng_ref[...])
    glane = lax.broadcasted_iota(jnp.int32, (tq, LANE), 1)

    def gate_col(idx):
        return jnp.sum(jnp.where(glane == idx, gate, 0.0), axis=-1, keepdims=True)

    for hh in range(NSA_HPG):
        base = 3 * (NSA_HPG * g + hh)
        r0, r1 = hh * tq, (hh + 1) * tq
        o = (gate_col(base) * oc_ref[:, NSA_HD * hh:NSA_HD * (hh + 1)]
             + gate_col(base + 1) * _group_half(o_s[r0:r1], g)
             + gate_col(base + 2) * _group_half(o_w[r0:r1], g))
        out_ref[:, NSA_HD * hh:NSA_HD * (hh + 1)] = o


def _pack_chunk_flags(anyblk):
    g, nt, nselp = anyblk.shape
    chunk = jnp.max(anyblk.reshape(g, nt, nselp // 2, 2), axis=-1) > 0.5
    nchunk = nselp // 2
    nwords = -(-nchunk // 32)
    chunk = jnp.pad(chunk, ((0, 0), (0, 0), (0, nwords * 32 - nchunk)))
    bits = chunk.reshape(g, nt, nwords, 32).astype(jnp.uint32) << jnp.arange(32, dtype=jnp.uint32)
    words = jnp.sum(bits, axis=-1, dtype=jnp.uint32)
    return lax.bitcast_convert_type(words, jnp.int32).reshape(-1), nwords


def _nsa_sw_prompt(z, zb, sel, anyblk, oc, slopes, tq=128, sb=4):
    s = zb.shape[0]
    assert tq == LANE and s >= WINDOW + tq
    nt = s // tq
    nselp = sel.shape[-1]
    flags, nwords = _pack_chunk_flags(anyblk[0, :, :, 0, :])
    gw = NSA_HPG * NSA_HD
    rows = NSA_HPG * tq
    res = lambda col: pl.BlockSpec((s, LANE), lambda g, t, fl: (0, col // LANE), pipeline_mode=pl.Buffered(1))
    grid_spec = pltpu.PrefetchScalarGridSpec(
        num_scalar_prefetch=1,
        grid=(NSA_GROUPS, nt),
        in_specs=[_smem_spec(),
                  pl.BlockSpec((tq, gw), lambda g, t, fl: (t, C_NQ // gw + g)),
                  res(C_KS), res(C_VS), res(C_KW), res(C_VW),
                  pl.BlockSpec((1, 1, tq, nselp), lambda g, t, fl: (0, g, t, 0)),
                  pl.BlockSpec((tq, gw), lambda g, t, fl: (t, g)),
                  pl.BlockSpec((tq, LANE), lambda g, t, fl: (t, C_NG // LANE))],
        out_specs=pl.BlockSpec((tq, gw), lambda g, t, fl: (t, g)),
        scratch_shapes=[pltpu.SMEM((nt + sb,), jnp.int32), pltpu.VMEM((rows, 1), F32),
                        pltpu.VMEM((rows, 2 * LANE), F32)],
    )
    return pl.pallas_call(
        functools.partial(_nsa_sw_kernel, tq=tq, nt=nt, nselp=nselp, nwords=nwords, sb=sb),
        grid_spec=grid_spec,
        out_shape=jax.ShapeDtypeStruct((s, NSA_GROUPS * gw), F32),
        compiler_params=_cparams(("arbitrary", "arbitrary")),
        name="nsa_sel_win",
    )(flags, slopes, zb, zb, zb, zb, zb, sel, oc, z)


def _decode_kernel(pt_ref, pos_ref, cnt_ref, *refs, pp, n_pages, ncols, hk, qpos0, win, n_new, has_sel, nselp,
                   feature_major):
    del pt_ref
    kpages = refs[:pp]
    vpages = refs[pp:2 * pp]
    rest = refs[2 * pp:]
    if has_sel:
        wq_ref, ci_ref, kn_ref, vn_ref, sel_ref, o_ref, m_ref, s_ref, acc_ref = rest
    else:
        wq_ref, ci_ref, kn_ref, vn_ref, o_ref, m_ref, s_ref, acc_ref = rest
        sel_ref = None
    bi = pl.program_id(0)
    st = pl.program_id(1)
    windowed = win < 1e8
    guarded = has_sel or windowed

    @pl.when(st == 0)
    def _():
        m_ref[...] = jnp.full(m_ref.shape, NEG, F32)
        s_ref[...] = jnp.zeros(s_ref.shape, F32)
        acc_ref[...] = jnp.zeros(acc_ref.shape, F32)

    wq = wq_ref[0]
    slope = ci_ref[:, 0:1]
    qrel = ci_ref[:, 1:2]
    colhead = ci_ref[:, 2:3]

    def geometry(nrows):
        r = lax.broadcasted_iota(jnp.int32, (ncols, nrows), 1)
        kidx = (r // hk).astype(F32)
        base = slope * kidx
        if hk > 1:
            base = jnp.where((r % hk).astype(F32) == colhead, base, NEG)
        return r, kidx, base

    def scores(k, kpos0, n_valid, geom, check_range, feature_major):
        r, kidx, base = geom
        if feature_major:
            s = jnp.dot(wq, k.astype(BF16), preferred_element_type=F32)
        else:
            s = lax.dot_general(wq, k.astype(BF16), NT_DIMS, preferred_element_type=F32)
        off = qrel + (qpos0 - kpos0).astype(F32)
        t = (s + base) - slope * off
        if not (check_range or has_sel or n_valid is not None):
            return t, None
        okf = jnp.ones(t.shape, F32)
        if check_range:
            dist = off - kidx
            okf = jnp.where(dist >= 0.0, jnp.where(dist < win, 1.0, 0.0), 0.0)
        if n_valid is not None:
            okf = jnp.where(r < n_valid, okf, 0.0)
        if has_sel:
            nrows = t.shape[1]
            jrow = lax.broadcasted_iota(jnp.int32, (nselp, nrows), 0)
            kp = kpos0 + lax.broadcasted_iota(jnp.int32, (nselp, nrows), 1)
            expand = jnp.where(jrow == kp // SEL_BLOCK, 1.0, 0.0).astype(BF16)
            okf = okf * jnp.dot(sel_ref[0].astype(BF16), expand, preferred_element_type=F32)
        ok = okf > 0.5
        return jnp.where(ok, t, NEG), ok

    def update(ts, oks, vs, feature_major):
        m_old = m_ref[...]
        m_new = m_old
        for t in ts:
            m_new = jnp.maximum(m_new, jnp.max(t, axis=-1, keepdims=True))
        alpha = jnp.exp2(m_old - m_new)
        l = alpha * s_ref[...]
        acc = alpha * acc_ref[...]
        for t, ok, v in zip(ts, oks, vs):
            p = jnp.exp2(t - m_new)
            if ok is not None:
                p = jnp.where(ok, p, 0.0)
            l = l + jnp.sum(p, axis=-1, keepdims=True)
            if feature_major:
                acc = acc + lax.dot_general(p.astype(BF16), v.astype(BF16), NT_DIMS, preferred_element_type=F32)
            else:
                acc = acc + jnp.dot(p.astype(BF16), v.astype(BF16), preferred_element_type=F32)
        s_ref[...] = l
        acc_ref[...] = acc
        m_ref[...] = m_new

    cnt = cnt_ref[bi]

    @pl.when(st * pp < cnt)
    def _():
        geom = geometry(PAGE * hk)
        ts, oks, vs = [], [], []
        for k in range(pp):
            slot = st * pp + k
            n_valid = jnp.where(slot < cnt, PAGE * hk, 0) if guarded else None
            t, ok = scores(kpages[k][0], pos_ref[bi * n_pages + slot], n_valid, geom, windowed, feature_major)
            ts.append(t)
            oks.append(ok)
            vs.append(vpages[k][0])
        update(ts, oks, vs, feature_major)

    @pl.when(st == pl.num_programs(1) - 1)
    def _():
        t, ok = scores(kn_ref[0], jnp.int32(qpos0), n_new * hk, geometry(PAGE), True, False)
        update([t], [ok], [vn_ref[0]], False)
        l = s_ref[...]
        o_ref[0] = acc_ref[...] / jnp.where(l == 0.0, 1.0, l)


def _decode_attn(wq, colinfo, pool_k, pool_v, page_ids, page_pos, page_cnt, knew, vnew, sel, *, hk, qpos0, win, n_new, pp,
                 feature_major=False):
    b, n_pages = page_ids.shape
    ncols, width = wq.shape[1], wq.shape[2]
    pp = _pages_per_step(n_pages, pp)
    has_sel = sel is not None
    nselp = sel.shape[-1] if has_sel else 0

    def page_spec(k):
        return pl.BlockSpec((1, width, PAGE) if feature_major else (1, PAGE * hk, width),
                            lambda bi, s, pt, pos, cnt: (pt[bi * n_pages + s * pp + k], 0, 0))

    per_batch = lambda shape: pl.BlockSpec((1,) + shape, lambda bi, s, pt, pos, cnt: (bi, 0, 0))
    in_specs = ([page_spec(k) for k in range(pp)] * 2
                + [per_batch((ncols, width)), pl.BlockSpec((ncols, LANE), lambda bi, s, pt, pos, cnt: (0, 0)),
                   per_batch((PAGE, width)), per_batch((PAGE, width))])
    args = [pool_k] * pp + [pool_v] * pp + [wq, colinfo, knew, vnew]
    if has_sel:
        in_specs.append(per_batch((ncols, nselp)))
        args.append(sel)
    grid_spec = pltpu.PrefetchScalarGridSpec(
        num_scalar_prefetch=3,
        grid=(b, n_pages // pp),
        in_specs=in_specs,
        out_specs=per_batch((ncols, width)),
        scratch_shapes=[pltpu.VMEM((ncols, 1), F32), pltpu.VMEM((ncols, 1), F32), pltpu.VMEM((ncols, width), F32)],
    )
    return pl.pallas_call(
        functools.partial(_decode_kernel, pp=pp, n_pages=n_pages, ncols=ncols, hk=hk, qpos0=qpos0, win=float(win),
                          n_new=n_new, has_sel=has_sel, nselp=nselp, feature_major=feature_major),
        grid_spec=grid_spec,
        out_shape=jax.ShapeDtypeStruct((b, ncols, width), F32),
        compiler_params=_cparams(("arbitrary", "arbitrary")),
        name="decode_attn",
    )(page_ids.reshape(-1), page_pos.reshape(-1), page_cnt, *args)


def _all_pages(page_table, pos0):
    b, n_pages = page_table.shape
    pos = jnp.broadcast_to(pos0 + PAGE * jnp.arange(n_pages, dtype=jnp.int32), (b, n_pages))
    return page_table, pos, jnp.full((b,), n_pages, jnp.int32)


def _selected_pages(page_table, sel_cols):
    b, n_pages = page_table.shape
    per_page = PAGE // SEL_BLOCK
    hit = jnp.max(sel_cols[:, :, :per_page * n_pages].reshape(b, -1, n_pages, per_page), axis=(1, 3)) > 0.5
    cnt = jnp.sum(hit, axis=1).astype(jnp.int32)
    order = jnp.argsort(jnp.logical_not(hit), axis=1, stable=True).astype(jnp.int32)
    keep = jnp.minimum(jnp.arange(n_pages, dtype=jnp.int32)[None], jnp.maximum(cnt - 1, 0)[:, None])
    order = jnp.take_along_axis(order, keep, axis=1)
    return jnp.take_along_axis(page_table, order, axis=1), order * PAGE, cnt


def _da_post_kernel(sc_ref, o0_ref, o1_ref, sub_ref, o_ref):
    o_ref[...] = _da_post(o0_ref[...], o1_ref[...], sc_ref[0], sub_ref[...])


def _da_post_call(o0, o1, sc, subln):
    return pl.pallas_call(
        _da_post_kernel,
        in_specs=[_smem_spec(), pl.BlockSpec(o0.shape, lambda: (0, 0)), pl.BlockSpec(o0.shape, lambda: (0, 0)),
                  pl.BlockSpec((1, LANE), lambda: (0, 0))],
        out_specs=pl.BlockSpec(o0.shape, lambda: (0, 0)),
        out_shape=jax.ShapeDtypeStruct(o0.shape, F32),
        name="da_post",
    )(sc, o0, o1, subln.reshape(1, LANE))


def _gate3_kernel(ng_ref, e_ref, oc_ref, os_ref, ow_ref, o_ref):
    gate = _sigmoid(ng_ref[...])
    acc = jnp.zeros(o_ref.shape, F32)
    for br, ref in enumerate((oc_ref, os_ref, ow_ref)):
        ge = jnp.dot(gate, e_ref[br], preferred_element_type=F32, precision=lax.Precision.HIGHEST)
        acc = acc + ge * ref[...]
    o_ref[...] = acc


def _gate3(ng, oc, os_, ow):
    e = np.zeros((3, LANE, NSA_HEADS * NSA_HD), np.float32)
    for br in range(3):
        for h in range(NSA_HEADS):
            e[br, 3 * h + br, NSA_HD * h:NSA_HD * (h + 1)] = 1.0
    return pl.pallas_call(
        _gate3_kernel,
        out_shape=jax.ShapeDtypeStruct(oc.shape, F32),
        name="gate3",
    )(ng, jnp.asarray(e), oc, os_, ow)


def _tail_a_kernel(x_ref, da_ref, nsa_ref, ga_ref, gb_ref, wo_ref, nx_ref, wxq_ref, h_ref, q_ref):
    m = _sigmoid(ga_ref[...]) * da_ref[...] + _sigmoid(gb_ref[...]) * nsa_ref[...]
    h = x_ref[...] + jnp.dot(m.astype(BF16), wo_ref[...], preferred_element_type=F32)
    h_ref[...] = h
    xn = _rms(h, nx_ref[...]).astype(BF16)
    q_ref[...] = jnp.dot(xn, wxq_ref[...], preferred_element_type=F32)


def _tail_a(x, o_da, o_nsa, z, w_o, norm_x, w_xq):
    m = x.shape[0]
    tm = min(m, 512)
    row = lambda cb: pl.BlockSpec((tm, D_MODEL), lambda i: (i, cb))
    const = lambda shape: pl.BlockSpec(shape, lambda i: (0, 0), pipeline_mode=pl.Buffered(1))
    return pl.pallas_call(
        _tail_a_kernel,
        grid=(m // tm,),
        in_specs=[row(0), row(0), row(0), row(C_GA // D_MODEL), row(C_GB // D_MODEL),
                  const((D_MODEL, D_MODEL)), const((1, D_MODEL)), const((D_MODEL, X_W))],
        out_specs=[row(0), pl.BlockSpec((tm, X_W), lambda i: (i, 0))],
        out_shape=[jax.ShapeDtypeStruct((m, D_MODEL), F32), jax.ShapeDtypeStruct((m, X_W), F32)],
        compiler_params=_cparams(("parallel",)),
        name="tail_merge_wo",
    )(x, o_da, o_nsa, z, z, w_o.astype(BF16), norm_x.reshape(1, D_MODEL), w_xq.astype(BF16))


def _cross_kernel(q_ref, mk_ref, mv_ref, o_ref):
    q = (q_ref[0] * (X_HD ** -0.5)).astype(BF16)
    mk = mk_ref[0].astype(BF16)
    mv = mv_ref[0].astype(BF16)
    outs = []
    for h in range(X_HEADS):
        sl = slice(X_HD * h, X_HD * (h + 1))
        s = lax.dot_general(q[:, sl], mk[:, sl], NT_DIMS, preferred_element_type=F32)
        e = jnp.exp(s - jnp.max(s, axis=-1, keepdims=True))
        p = e / jnp.sum(e, axis=-1, keepdims=True)
        outs.append(jnp.dot(p.astype(BF16), mv[:, sl], preferred_element_type=F32))
    o_ref[0] = jnp.concatenate(outs, axis=-1)


def _cross(q, mk, mv):
    b, t, _ = q.shape
    mlen = mk.shape[1]
    tt = min(t, 512)
    return pl.pallas_call(
        _cross_kernel,
        grid=(b, t // tt),
        in_specs=[pl.BlockSpec((1, tt, X_W), lambda bi, i: (bi, i, 0)),
                  pl.BlockSpec((1, mlen, X_W), lambda bi, i: (bi, 0, 0)),
                  pl.BlockSpec((1, mlen, X_W), lambda bi, i: (bi, 0, 0))],
        out_specs=pl.BlockSpec((1, tt, X_W), lambda bi, i: (bi, i, 0)),
        out_shape=jax.ShapeDtypeStruct((b, t, X_W), F32),
        compiler_params=_cparams(("parallel", "parallel")),
        name="cross_attn",
    )(q, mk, mv)


def _tail_c_kernel(h_ref, ox_ref, wxo_ref, nf_ref, wg_ref, wu_ref, wd_ref, nfin_ref, y_ref, *, nchunk, chunk):
    h = h_ref[...] + jnp.dot(ox_ref[...].astype(BF16), wxo_ref[...], preferred_element_type=F32)
    xn = _rms(h, nf_ref[...]).astype(BF16)
    acc = jnp.zeros(h.shape, F32)
    for c in range(nchunk):
        sl = slice(c * chunk, (c + 1) * chunk)
        gt = jnp.dot(xn, wg_ref[:, sl], preferred_element_type=F32)
        up = jnp.dot(xn, wu_ref[:, sl], preferred_element_type=F32)
        act = (gt * _sigmoid(gt) * up).astype(BF16)
        acc = acc + jnp.dot(act, wd_ref[sl, :], preferred_element_type=F32)
    y_ref[...] = _rms(h + acc, nfin_ref[...])


def _tail_c(h, ox, w_xo, norm_ffn, w_gate_up, w_down, norm_final):
    m = h.shape[0]
    tm = min(m, 512)
    hid = w_down.shape[0]
    chunk = hid // 2
    const = lambda shape: pl.BlockSpec(shape, lambda i: (0, 0), pipeline_mode=pl.Buffered(1))
    return pl.pallas_call(
        functools.partial(_tail_c_kernel, nchunk=2, chunk=chunk),
        grid=(m // tm,),
        in_specs=[pl.BlockSpec((tm, D_MODEL), lambda i: (i, 0)), pl.BlockSpec((tm, X_W), lambda i: (i, 0)),
                  const((X_W, D_MODEL)), const((1, D_MODEL)), const((D_MODEL, hid)), const((D_MODEL, hid)),
                  const((hid, D_MODEL)), const((1, D_MODEL))],
        out_specs=pl.BlockSpec((tm, D_MODEL), lambda i: (i, 0)),
        out_shape=jax.ShapeDtypeStruct((m, D_MODEL), F32),
        compiler_params=_cparams(("parallel",)),
        name="tail_ffn",
    )(h, ox, w_xo.astype(BF16), norm_ffn.reshape(1, D_MODEL), w_gate_up[:, :hid].astype(BF16),
      w_gate_up[:, hid:].astype(BF16), w_down.astype(BF16), norm_final.reshape(1, D_MODEL))


def _alibi(n):
    return np.asarray(2.0 ** (-8.0 * np.arange(1, n + 1) / n) * LOG2E, dtype=np.float32)


def _finish(x, o_da, o_nsa, z, mk, mv, batch, w_o, norm_x, w_xq, w_xo, norm_ffn, w_gate_up, w_down, norm_final):
    m = x.shape[0]
    t = m // batch
    h1, qx = _tail_a(x, o_da, o_nsa, z, w_o, norm_x, w_xq)
    q3 = qx.reshape(batch, t, X_W)
    tpad = -(-t // 8) * 8
    if tpad != t:
        q3 = jnp.pad(q3, ((0, 0), (0, tpad - t), (0, 0)))
    ox = _cross(q3, mk, mv)[:, :t].reshape(m, X_W)
    return _tail_c(h1, ox, w_xo, norm_ffn, w_gate_up, w_down, norm_final)


def kernel(x_prompt, x_sample, cache_diff_k, cache_diff_v, cache_cmp_k, cache_cmp_v, cache_sel_k, cache_sel_v,
           cache_win_k, cache_win_v, cache_mem_k, cache_mem_v, page_table, mem_prompt,
           norm_mix, w_in, lam_q1, lam_k1, lam_q2, lam_k2, da_subln,
           w_cmp_k1, pe_cmp_k, w_cmp_k2, w_cmp_v1, pe_cmp_v, w_cmp_v2,
           w_o, norm_x, w_xq, w_mem_kv, w_xo, norm_ffn, w_gate_up, w_down, norm_final):
    batch, seq, _ = x_prompt.shape
    db, ds, _ = x_sample.shape
    assert batch == 1 and norm_mix.shape[0] == 1
    n_pages = page_table.shape[1]
    past = n_pages * PAGE
    wb = cache_win_k.shape[2]
    kvw = NSA_GROUPS * NSA_HD

    lam = (jnp.exp(jnp.sum(lam_q1[0] * lam_k1[0])) - jnp.exp(jnp.sum(lam_q2[0] * lam_k2[0])) + LAM_INIT).astype(F32)
    da_sc = jnp.concatenate([lam.reshape(1), jnp.asarray(_alibi(DA_HEADS))])
    nsa_sl = jnp.asarray(_alibi(NSA_HEADS))
    w_pad = _prep_w_in(w_in[0])
    tail_w = (w_o[0], norm_x[0], w_xq[0], w_xo[0], norm_ffn[0], w_gate_up[0], w_down[0], norm_final)
    cmp_k = (w_cmp_k1[0], pe_cmp_k[0], w_cmp_k2[0])
    cmp_v = (w_cmp_v1[0], pe_cmp_v[0], w_cmp_v2[0])

    xp = x_prompt.reshape(seq, D_MODEL)
    z, zb = _inproj(xp, norm_mix[0], w_pad)
    o_da = _da_prompt(zb, da_sc, da_subln[0])

    ident = jnp.arange(seq // PAGE, dtype=jnp.int32).reshape(1, -1)
    zero_tail = jnp.zeros((1, 8, CMP_STRIDE * kvw), F32)
    p_kc = z[:, C_KC:C_KC + kvw]
    p_vc = z[:, C_VC:C_VC + kvw]
    as_pages = lambda a: a.reshape(seq // PAGE, PAGE, kvw).transpose(0, 2, 1)
    kcb = _compress(as_pages(p_kc), ident, zero_tail, *cmp_k)
    vcb = _compress(as_pages(p_vc), ident, zero_tail, *cmp_v)
    nch = seq // CMP_STRIDE
    oc, sel, anyblk = _cmp_topk(zb, C_NQ // (NSA_HPG * NSA_HD), kcb, vcb, nsa_sl, batch=1, sq=seq, tq=128,
                                nblk=nch - 1, nsel=seq // SEL_BLOCK, qpos_base=0)
    o_nsa = _nsa_sw_prompt(z, zb, sel, anyblk, oc, nsa_sl)

    mem_kv = _matmul(mem_prompt.reshape(-1, D_MODEL), w_mem_kv[0].astype(BF16))
    p_mk, p_mv = mem_kv[:, :X_W], mem_kv[:, X_W:]
    y_prompt = _finish(xp, o_da, o_nsa, z, p_mk[None], p_mv[None], 1, *tail_w)

    r5 = lambda a, h: a.reshape(1, 1, a.shape[0], h, -1)
    p_states = (r5(z[:, C_DK:C_DK + 1024], DA_HEADS), r5(z[:, C_DV:C_DV + 1024], DA_HEADS),
                r5(p_kc, NSA_GROUPS), r5(p_vc, NSA_GROUPS),
                r5(z[:, C_KS:C_KS + kvw], NSA_GROUPS), r5(z[:, C_VS:C_VS + kvw], NSA_GROUPS),
                r5(z[seq - min(WINDOW, seq):, C_KW:C_KW + kvw], NSA_GROUPS),
                r5(z[seq - min(WINDOW, seq):, C_VW:C_VW + kvw], NSA_GROUPS),
                r5(p_mk, X_HEADS), r5(p_mv, X_HEADS))

    ms = db * ds
    xs = x_sample.reshape(ms, D_MODEL)
    zs, zsb = _inproj(xs, norm_mix[0], w_pad)
    z3 = zs.reshape(db, ds, ZP)
    pad_rows = lambda a, n: jnp.pad(a, ((0, 0), (0, n - a.shape[1]), (0, 0)))

    dq = z3[:, :, C_DQ:C_DQ + 1024].reshape(db, ds, DA_HEADS, 2, DA_HD)
    wq_da = jnp.einsum('bqhmd,mn->bhmqnd', dq, jnp.eye(2, dtype=F32))
    wq_da = wq_da.reshape(db, DA_HEADS * 2 * ds, 2 * DA_HD).astype(BF16)
    ci = np.zeros((DA_HEADS * 2 * ds, LANE), np.float32)
    ci[:, 0] = np.repeat(_alibi(DA_HEADS), 2 * ds)
    ci[:, 1] = np.tile(np.arange(ds), DA_HEADS * 2)
    ci[:, 2] = np.repeat(np.arange(DA_HEADS), 2 * ds)
    s_dk, s_dv = z3[:, :, C_DK:C_DK + 1024], z3[:, :, C_DV:C_DV + 1024]
    head_rows = lambda a: pad_rows(a.reshape(db, ds * DA_HEADS, 2 * DA_HD), PAGE)
    o_pair = _decode_attn(wq_da, jnp.asarray(ci), cache_diff_k[0].reshape(-1, PAGE * DA_HEADS, 2 * DA_HD),
                          cache_diff_v[0].reshape(-1, PAGE * DA_HEADS, 2 * DA_HD), *_all_pages(page_table, 0),
                          head_rows(s_dk), head_rows(s_dv), None, hk=DA_HEADS, qpos0=past, win=1e9, n_new=ds, pp=8)
    o_pair = o_pair.reshape(db, DA_HEADS, 2, ds, 2 * DA_HD).transpose(2, 0, 3, 1, 4)
    o_da_s = _da_post_call(o_pair[0].reshape(ms * DA_HEADS, LANE), o_pair[1].reshape(ms * DA_HEADS, LANE),
                           da_sc, da_subln[0]).reshape(ms, DA_HEADS * LANE)

    ck = CMP_STRIDE * kvw
    s_kc, s_vc = z3[:, :, C_KC:C_KC + kvw], z3[:, :, C_VC:C_VC + kvw]
    tail_of = lambda a: jnp.pad(a.reshape(db, 1, ds * kvw), ((0, 0), (0, 7), (0, ck - ds * kvw)))
    feat_major = lambda c: c[0].transpose(0, 2, 3, 1).reshape(-1, kvw, PAGE)
    kcb_s = _compress(feat_major(cache_cmp_k), page_table, tail_of(s_kc), *cmp_k)
    vcb_s = _compress(feat_major(cache_cmp_v), page_table, tail_of(s_vc), *cmp_v)
    tq_s = 16
    nq_pad = pad_rows(zsb.reshape(db, ds, ZP)[:, :, C_NQ:C_NQ + 1024], tq_s).reshape(db * tq_s, 1024)
    nsel_s = -(-(past + ds) // SEL_BLOCK)
    oc_s, sel_s, _ = _cmp_topk(nq_pad, 0, kcb_s, vcb_s, nsa_sl, batch=db, sq=tq_s, tq=tq_s,
                               nblk=(past + ds + CMP_STRIDE - 1) // CMP_STRIDE - 1, nsel=nsel_s, qpos_base=past)
    oc_s = oc_s.reshape(db, tq_s, 1024)[:, :ds].reshape(ms, 1024)

    nq = z3[:, :, C_NQ:C_NQ + 1024].reshape(db, ds, NSA_GROUPS, NSA_HPG, NSA_HD)
    wq_n = jnp.einsum('bqghd,gk->bghqkd', nq, jnp.eye(NSA_GROUPS, dtype=F32))
    wq_n = wq_n.reshape(db, NSA_HEADS * ds, kvw).astype(BF16)
    cn = np.zeros((NSA_HEADS * ds, LANE), np.float32)
    cn[:, 0] = np.repeat(_alibi(NSA_HEADS), ds)
    cn[:, 1] = np.tile(np.arange(ds), NSA_HEADS)
    cn = jnp.asarray(cn)
    sel_cols = jnp.repeat(sel_s[:, :, None, :ds, :], NSA_HPG, axis=2).reshape(db, NSA_HEADS * ds, -1)
    s_ks, s_vs = z3[:, :, C_KS:C_KS + kvw], z3[:, :, C_VS:C_VS + kvw]
    s_kw, s_vw = z3[:, :, C_KW:C_KW + kvw], z3[:, :, C_VW:C_VW + kvw]
    o_sel = _decode_attn(wq_n, cn, feat_major(cache_sel_k), feat_major(cache_sel_v),
                         *_selected_pages(page_table, sel_cols), pad_rows(s_ks, PAGE), pad_rows(s_vs, PAGE), sel_cols,
                         hk=1, qpos0=past, win=1e9, n_new=ds, pp=16, feature_major=True)
    win_pages = wb // PAGE
    win_pt = jnp.arange(db * win_pages, dtype=jnp.int32).reshape(db, win_pages)
    o_win = _decode_attn(wq_n, cn, cache_win_k[0].reshape(-1, PAGE, kvw), cache_win_v[0].reshape(-1, PAGE, kvw),
                         *_all_pages(win_pt, past - wb), pad_rows(s_kw, PAGE), pad_rows(s_vw, PAGE), None,
                         hk=1, qpos0=past, win=WINDOW, n_new=ds, pp=win_pages)

    def own_group(o):
        o = o.reshape(db, NSA_GROUPS, NSA_HPG, ds, NSA_GROUPS, NSA_HD)
        o = jnp.stack([o[:, g, :, :, g] for g in range(NSA_GROUPS)], axis=1)
        return o.transpose(0, 3, 1, 2, 4).reshape(ms, NSA_HEADS * NSA_HD)

    o_nsa_s = _gate3(zs[:, C_NG:C_NG + LANE], oc_s, own_group(o_sel), own_group(o_win))
    s_mk = cache_mem_k[0].reshape(db, -1, X_W)
    s_mv = cache_mem_v[0].reshape(db, -1, X_W)
    y_sample = _finish(xs, o_da_s, o_nsa_s, zs, s_mk, s_mv, db, *tail_w)

    s5 = lambda a, h: a.reshape(1, db, a.shape[1], h, -1)
    new_win = lambda c, a: jnp.concatenate([c[0].reshape(db, wb, kvw), a], axis=1)[:, ds:]
    s_states = (s5(s_dk, DA_HEADS), s5(s_dv, DA_HEADS), s5(s_kc, NSA_GROUPS), s5(s_vc, NSA_GROUPS),
                s5(s_ks, NSA_GROUPS), s5(s_vs, NSA_GROUPS),
                s5(new_win(cache_win_k, s_kw), NSA_GROUPS), s5(new_win(cache_win_v, s_vw), NSA_GROUPS))

    return (y_prompt.reshape(1, seq, D_MODEL), y_sample.reshape(db, ds, D_MODEL)) + p_states + s_states
```

```python
import functools

import numpy as np
import jax
import jax.numpy as jnp
from jax import lax
from jax.experimental import pallas as pl
from jax.experimental.pallas import tpu as pltpu

F32 = jnp.float32
BF16 = jnp.bfloat16

D_MODEL = 1024
DA_HEADS = 8
DA_HD = 64
NSA_HEADS = 16
NSA_GROUPS = 2
NSA_HPG = NSA_HEADS // NSA_GROUPS
NSA_HD = 64
CMP_LEN = 32
CMP_STRIDE = 16
SEL_BLOCK = 64
SEL_TOP = 16
WINDOW = 512
X_HEADS = 4
X_HD = 64
X_W = X_HEADS * X_HD
EPS = 1e-6
NEG = -1e30
PICKED = -3e38
FORCE_BONUS = 1e6
LAM_INIT = 0.2
LANE = 128
PAGE = 128
VMEM_LIMIT = 56 * 1024 * 1024

C_DQ, C_DK, C_DV, C_NQ, C_GA, C_GB = 0, 1024, 2048, 3072, 4096, 5120
C_KC, C_VC, C_KS, C_VS, C_KW, C_VW, C_NG = 6144, 6272, 6400, 6528, 6656, 6784, 6912
ZP = 7040
ZP_TILE = 1408

NT_DIMS = (((1,), (1,)), ((), ()))
LOG2E = 1.4426950408889634
Q_SCALE = DA_HD ** -0.5 * LOG2E


def _cparams(sem):
    return pltpu.CompilerParams(dimension_semantics=sem, vmem_limit_bytes=VMEM_LIMIT)


def _smem_spec():
    return pl.BlockSpec(memory_space=pltpu.SMEM)


def _pages_per_step(n_pages, cap):
    return max(p for p in range(1, cap + 1) if n_pages % p == 0)


def _sigmoid(x):
    return 1.0 / (1.0 + jnp.exp(-x))


def _rms(x, g):
    return x * lax.rsqrt(jnp.mean(x * x, axis=-1, keepdims=True) + EPS) * g


def _inproj_kernel(x_ref, g_ref, w_ref, o_ref, ob_ref, xn_ref):
    @pl.when(pl.program_id(1) == 0)
    def _():
        xn_ref[...] = _rms(x_ref[...], g_ref[...]).astype(BF16)

    acc = jnp.dot(xn_ref[...], w_ref[...], preferred_element_type=F32)
    o_ref[...] = acc
    ob_ref[...] = acc.astype(BF16)


def _inproj(x, g, w_pad):
    m = x.shape[0]
    tm = min(m, 512)
    return pl.pallas_call(
        _inproj_kernel,
        grid=(m // tm, ZP // ZP_TILE),
        in_specs=[pl.BlockSpec((tm, D_MODEL), lambda i, j: (i, 0)),
                  pl.BlockSpec((1, D_MODEL), lambda i, j: (0, 0)),
                  pl.BlockSpec((D_MODEL, ZP_TILE), lambda i, j: (0, j))],
        out_specs=[pl.BlockSpec((tm, ZP_TILE), lambda i, j: (i, j)),
                   pl.BlockSpec((tm, ZP_TILE), lambda i, j: (i, j))],
        out_shape=[jax.ShapeDtypeStruct((m, ZP), F32), jax.ShapeDtypeStruct((m, ZP), BF16)],
        scratch_shapes=[pltpu.VMEM((tm, D_MODEL), BF16)],
        compiler_params=_cparams(("parallel", "arbitrary")),
        name="inproj",
    )(x, g.reshape(1, D_MODEL), w_pad)


def _prep_w_in(w_in):
    a = jnp.concatenate([w_in[:, :1024] * Q_SCALE, w_in[:, 1024:3072], w_in[:, 3072:4096] * Q_SCALE], axis=1)
    kv = w_in[:, 4096:4864]
    ng = w_in[:, 4864:4912]
    mg = w_in[:, 4912:]
    return jnp.concatenate([a, mg, kv, jnp.pad(ng, ((0, 0), (0, LANE - ng.shape[1])))], axis=1).astype(BF16)


def _matmul_kernel(x_ref, w_ref, o_ref):
    o_ref[...] = jnp.dot(x_ref[...].astype(BF16), w_ref[...], preferred_element_type=F32)


def _matmul(x, w_bf16):
    m, n = x.shape[0], w_bf16.shape[1]
    return pl.pallas_call(
        _matmul_kernel,
        out_shape=jax.ShapeDtypeStruct((m, n), F32),
        compiler_params=pltpu.CompilerParams(vmem_limit_bytes=VMEM_LIMIT),
        name="matmul",
    )(x, w_bf16)


def _da_post(o0, o1, lam, sub):
    o = o0 - lam * o1
    return _rms(o, sub) * (1.0 - LAM_INIT)


def _split3(x):
    a = x.astype(BF16).astype(F32)
    r = x - a
    b = r.astype(BF16).astype(F32)
    return a, b, (r - b).astype(BF16).astype(F32)


def _thirds(lane, pieces):
    which = lane % 3
    return jnp.where(which == 0, pieces[0], jnp.where(which == 1, pieces[1], pieces[2]))


def _da_prompt_kernel(sc_ref, q_ref, k_ref, v_ref, sub_ref, o_ref, ka_ref, va_ref, sa_ref, sb_ref, m_ref, acc_ref,
                      *, tq, tkb):
    h = pl.program_id(0)
    qi = pl.program_id(1)
    lam = sc_ref[0]
    slope = sc_ref[1 + h]
    per_big = tkb // tq
    seq = ka_ref.shape[0]

    @pl.when(qi == 0)
    def _():
        j = lax.broadcasted_iota(jnp.int32, (tkb, LANE), 0)
        grp = lax.broadcasted_iota(jnp.int32, (tkb, LANE), 1) // 3
        kfeat = jnp.where(grp == 0, (j // 32) * 32, jnp.where(grp == 1, j % 32, jnp.where(grp == 2, 1, 0)))
        kfeat = kfeat.astype(F32).astype(BF16)
        ka_ref[:, :LANE] = k_ref[...]
        for c in range(seq // tkb):
            ka_ref[c * tkb:(c + 1) * tkb, LANE:] = kfeat
        va_ref[:, :LANE] = v_ref[...]
        va_ref[:, LANE:] = jnp.ones((seq, LANE), BF16)

    q = q_ref[...]
    lane = lax.broadcasted_iota(jnp.int32, (tq, LANE), 1)
    zero = jnp.zeros_like(q)
    qbd = jnp.concatenate([jnp.where(lane < DA_HD, q, zero), jnp.where(lane >= DA_HD, q, zero)], axis=0)
    lane2 = lax.broadcasted_iota(jnp.int32, (2 * tq, LANE), 1)
    i_loc = (lax.broadcasted_iota(jnp.int32, (2 * tq, LANE), 0) & (tq - 1)).astype(F32)
    sl_pieces = _thirds(lane2, _split3(jnp.full((2 * tq, LANE), slope, F32)))
    row_pieces = _thirds(lane2, _split3(-slope * i_loc))
    qfeat = jnp.where(lane2 < 6, sl_pieces, jnp.where(lane2 < 9, row_pieces, 0.0))
    qaug = jnp.concatenate([qbd, qfeat.astype(BF16)], axis=1)
    m_ref[...] = jnp.full(m_ref.shape, NEG, F32)
    acc_ref[...] = jnp.zeros(acc_ref.shape, F32)

    def scores(idx):
        k = ka_ref[pl.ds(pl.multiple_of(idx * tkb, tkb), tkb), :]
        return lax.dot_general(qaug, k, NT_DIMS, preferred_element_type=F32)

    def consume(idx, s_ref, masked):
        base = qi * tq - idx * tkb
        cc = slope * base.astype(F32)
        va = va_ref[pl.ds(pl.multiple_of(idx * tkb, tkb), tkb), :]
        m_old = m_ref[...]
        if masked:
            ii = lax.broadcasted_iota(jnp.int32, s_ref.shape, 0) & (tq - 1)
            jj = lax.broadcasted_iota(jnp.int32, s_ref.shape, 1)
            t = jnp.where(jj - ii <= base, s_ref[...], NEG)
            m_new = jnp.maximum(m_old, jnp.max(t, axis=-1, keepdims=True) - cc)
            p = jnp.exp2(t - (m_new + cc))
        else:
            m_new = jnp.maximum(m_old, jnp.max(s_ref[...], axis=-1, keepdims=True) - cc)
            p = jnp.exp2(s_ref[...] - (m_new + cc))
        alpha = jnp.exp2(m_old - m_new)
        acc_ref[...] = alpha * acc_ref[...] + jnp.dot(p.astype(BF16), va, preferred_element_type=F32)
        m_ref[...] = m_new

    nbig = qi // per_big
    sa_ref[...] = scores(0)

    def half(idx, cur_ref, nxt_ref):
        @pl.when(idx < nbig)
        def _():
            nxt_ref[...] = scores(idx + 1)
            consume(idx, cur_ref, False)

        @pl.when(idx == nbig)
        def _():
            consume(idx, cur_ref, True)

    def pair(j, carry):
        half(2 * j, sa_ref, sb_ref)
        half(2 * j + 1, sb_ref, sa_ref)
        return carry

    lax.fori_loop(0, nbig // 2 + 1, pair, 0)

    o0 = acc_ref[0:tq, :LANE] / acc_ref[0:tq, LANE:]
    o1 = acc_ref[tq:2 * tq, :LANE] / acc_ref[tq:2 * tq, LANE:]
    o_ref[...] = _da_post(o0, o1, lam, sub_ref[...])


def _da_prompt(zb, sc, subln, tq=512, tkb=1024):
    s = zb.shape[0]
    tq = min(tq, s)
    tkb = min(tkb, s)
    kb, vb = C_DK // LANE, C_DV // LANE
    return pl.pallas_call(
        functools.partial(_da_prompt_kernel, tq=tq, tkb=tkb),
        grid=(DA_HEADS, s // tq),
        in_specs=[_smem_spec(),
                  pl.BlockSpec((tq, LANE), lambda h, i: (i, h)),
                  pl.BlockSpec((s, LANE), lambda h, i: (0, kb + h), pipeline_mode=pl.Buffered(1)),
                  pl.BlockSpec((s, LANE), lambda h, i: (0, vb + h), pipeline_mode=pl.Buffered(1)),
                  pl.BlockSpec((1, LANE), lambda h, i: (0, 0))],
        out_specs=pl.BlockSpec((tq, LANE), lambda h, i: (i, h)),
        out_shape=jax.ShapeDtypeStruct((s, DA_HEADS * LANE), F32),
        scratch_shapes=[pltpu.VMEM((s, 2 * LANE), BF16),
                        pltpu.VMEM((s, 2 * LANE), BF16),
                        pltpu.VMEM((2 * tq, tkb), F32), pltpu.VMEM((2 * tq, tkb), F32),
                        pltpu.VMEM((2 * tq, 1), F32), pltpu.VMEM((2 * tq, 2 * LANE), F32)],
        compiler_params=_cparams(("arbitrary", "arbitrary")),
        name="da_prompt",
    )(sc, zb, zb, zb, subln.reshape(1, LANE))


def _compress_kernel(pt_ref, *refs, pp, nch):
    del pt_ref
    pages = refs[:pp]
    tail_ref, wbig_ref, w1_ref, pe_ref, w2_ref, out_ref, ab_ref, rows_ref, x_ref = refs[pp:]
    s = pl.program_id(1)
    for k in range(pp):
        rows_ref[k] = pages[k][0].T
    for k in range(pp):
        for pos in range(CMP_STRIDE):
            x_ref[8 * k:8 * (k + 1), LANE * pos:LANE * (pos + 1)] = rows_ref[k, pl.ds(pos, 8, stride=CMP_STRIDE), :]
    x = x_ref[...].astype(BF16)
    rows = 8 * pp
    ab_ref[pl.ds(pl.multiple_of(s * rows, rows), rows), :] = jnp.dot(x, wbig_ref[...], preferred_element_type=F32)

    @pl.when(s == pl.num_programs(1) - 1)
    def _():
        ab_ref[nch:nch + 8, :] = jnp.dot(tail_ref[0].astype(BF16), wbig_ref[...], preferred_element_type=F32)
        hpe = jnp.dot(pe_ref[...].astype(BF16), w1_ref[...], preferred_element_type=F32)[0:1]
        outs = []
        hid = 2 * NSA_HD
        for g in range(NSA_GROUPS):
            a = ab_ref[0:nch, 2 * hid * g:2 * hid * g + hid]
            b = ab_ref[1:nch + 1, 2 * hid * g + hid:2 * hid * (g + 1)]
            hd = a + b + hpe
            act = hd * _sigmoid(hd)
            outs.append(jnp.dot(act.astype(BF16), w2_ref[...], preferred_element_type=F32))
        out_ref[0] = jnp.concatenate(outs, axis=-1)


def _compress(pool, page_table, tail, w1, pe, w2):
    b, n_pages = page_table.shape
    nch = n_pages * 8
    pp = _pages_per_step(n_pages, 32)
    ck = CMP_STRIDE * NSA_GROUPS * NSA_HD
    hid = 2 * NSA_HD
    w1r = w1.reshape(2, CMP_STRIDE, NSA_HD, hid)
    wbig = jnp.einsum('psdh,gk->sgdkph', w1r, jnp.eye(NSA_GROUPS, dtype=F32)).reshape(ck, 2 * NSA_GROUPS * hid).astype(BF16)
    pe8 = jnp.pad(pe.reshape(1, CMP_LEN * NSA_HD), ((0, 7), (0, 0)))

    def page_spec(k):
        return pl.BlockSpec((1, LANE, PAGE), lambda bi, s, pt: (pt[bi * n_pages + s * pp + k], 0, 0))

    const = lambda shape: pl.BlockSpec(shape, lambda bi, s, pt: tuple(0 for _ in shape))
    grid_spec = pltpu.PrefetchScalarGridSpec(
        num_scalar_prefetch=1,
        grid=(b, n_pages // pp),
        in_specs=[page_spec(k) for k in range(pp)] + [
            pl.BlockSpec((1, 8, ck), lambda bi, s, pt: (bi, 0, 0)),
            const((ck, 2 * NSA_GROUPS * hid)), const((CMP_LEN * NSA_HD, hid)), const((8, CMP_LEN * NSA_HD)),
            const((hid, NSA_HD))],
        out_specs=pl.BlockSpec((1, nch, NSA_GROUPS * NSA_HD), lambda bi, s, pt: (bi, 0, 0)),
        scratch_shapes=[pltpu.VMEM((nch + 8, 2 * NSA_GROUPS * hid), F32), pltpu.VMEM((pp, PAGE, LANE), F32),
                        pltpu.VMEM((8 * pp, ck), F32)],
    )
    return pl.pallas_call(
        functools.partial(_compress_kernel, pp=pp, nch=nch),
        grid_spec=grid_spec,
        out_shape=jax.ShapeDtypeStruct((b, nch, NSA_GROUPS * NSA_HD), F32),
        compiler_params=_cparams(("arbitrary", "arbitrary")),
        name="compress",
    )(page_table.reshape(-1), *([pool] * pp), tail, wbig, w1.astype(BF16), pe8, w2.astype(BF16))


def _stack_group_queries(q, g, tq):
    lane = lax.broadcasted_iota(jnp.int32, (tq, 2 * NSA_HD), 1)
    mine = jnp.where(lane >= NSA_HD, 1, 0) == g
    parts = []
    for hh in range(NSA_HPG):
        qh = q[:, NSA_HD * hh:NSA_HD * (hh + 1)]
        parts.append(jnp.where(mine, jnp.concatenate([qh, qh], axis=1), jnp.zeros((tq, 2 * NSA_HD), q.dtype)))
    return jnp.concatenate(parts, axis=0)


def _group_half(x, g):
    return jnp.where(g == 0, x[:, :NSA_HD], x[:, NSA_HD:])


def _cmp_topk_kernel(sl_ref, q_ref, kc_ref, vc_ref, agg_ref, oc_ref, sel_ref, any_ref, *, tq, nb, nselp, qpos_base, topk,
                     levels):
    g = pl.program_id(1)
    t = pl.program_id(2)
    q0 = qpos_base + t * tq
    qpad = _stack_group_queries(q_ref[...], g, tq)

    def work(nbw, nsw):
        kcb = kc_ref[0, :nbw, :].astype(BF16)
        vcb = vc_ref[0, :nbw, :].astype(BF16)
        s_all = lax.dot_general(qpad, kcb, NT_DIMS, preferred_element_type=F32)
        i = lax.broadcasted_iota(jnp.int32, (tq, nbw), 0)
        n = lax.broadcasted_iota(jnp.int32, (tq, nbw), 1)
        dist = (q0 + i - (CMP_STRIDE * n + (CMP_LEN - 1))).astype(F32)
        mask = dist >= 0
        row_ok = dist[:, 0:1] >= 0.0
        psum = jnp.zeros((tq, nbw), F32)
        for hh in range(NSA_HPG):
            slope = sl_ref[NSA_HPG * g + hh]
            tt = jnp.where(mask, s_all[hh * tq:(hh + 1) * tq] - slope * dist, NEG)
            m = jnp.max(tt, axis=-1, keepdims=True)
            e = jnp.exp2(tt - m)
            l = jnp.sum(e, axis=-1, keepdims=True)
            p = e * jnp.where(row_ok, 1.0 / l, 0.0)
            psum = psum + p
            o = jnp.dot(p.astype(BF16), vcb, preferred_element_type=F32)
            oc_ref[:, NSA_HD * hh:NSA_HD * (hh + 1)] = _group_half(o, g)

        p_hi = psum.astype(BF16)
        p_lo = (psum - p_hi.astype(F32)).astype(BF16)
        agg = agg_ref[:nbw, :nsw]
        imp = jnp.dot(p_hi, agg, preferred_element_type=F32) + jnp.dot(p_lo, agg, preferred_element_type=F32)
        jj = lax.broadcasted_iota(jnp.int32, (tq, nsw), 1)
        cur = (q0 + lax.broadcasted_iota(jnp.int32, (tq, nsw), 0)) // SEL_BLOCK
        valid = jj <= cur
        forced = jnp.where(valid, jnp.where(jj == 0, 1, jnp.where(jj >= cur - 1, 1, 0)), 0)
        score = jnp.where(valid, imp + jnp.where(forced == 1, FORCE_BONUS, 0.0), NEG)

        tqp = -(-tq // LANE) * LANE
        if tqp > tq:
            score = jnp.concatenate([score, jnp.full((tqp - tq, nsw), NEG, F32)], axis=0)
        cand = lax.broadcasted_iota(jnp.int32, (nsw, tqp), 0)

        def pick(_, carry):
            sc, chosen = carry
            mx = jnp.max(sc, axis=0, keepdims=True)
            idx = jnp.min(jnp.where(sc == mx, cand, nsw), axis=0, keepdims=True)
            hit = cand == idx
            return jnp.where(hit, PICKED, sc), jnp.where(hit, 1.0, chosen)

        _, chosen = lax.fori_loop(0, topk, pick, (score.T, jnp.zeros((nsw, tqp), F32)))
        sel = jnp.where(valid, chosen.T[:tq], 0.0)
        if nsw < nselp:
            sel = jnp.concatenate([sel, jnp.zeros((tq, nselp - nsw), F32)], axis=1)
        sel_ref[0, 0] = sel
        any_ref[0, 0, 0] = jnp.broadcast_to(jnp.max(sel, axis=0, keepdims=True), (8, nselp))

    if levels == 1:
        work(nb, nselp)
    else:
        unit = nselp // levels
        need = (q0 + tq - 1) // SEL_BLOCK + 1
        lvl = (need + unit - 1) // unit
        for lv in range(1, levels + 1):
            @pl.when(lvl == lv)
            def _():
                work(nb * lv // levels, unit * lv)


def _sel_agg_matrix(nblk, nsel, nb, nselp):
    m = np.zeros((nb, nselp), np.float32)
    j = np.arange(nsel)
    r, c = SEL_BLOCK // CMP_STRIDE, CMP_LEN // CMP_STRIDE
    for a in range(r):
        for b in range(c):
            i = r * j + a - b
            ok = (i >= 0) & (i < nblk)
            np.add.at(m, (i[ok], j[ok]), 1.0)
    return jnp.asarray(m, dtype=BF16)


def _cmp_topk(qarr, qcol0, kcb, vcb, slopes, *, batch, sq, tq, nblk, nsel, qpos_base):
    nb = kcb.shape[1]
    nselp = -(-nsel // LANE) * LANE
    nt = sq // tq
    agg = _sel_agg_matrix(nblk, nsel, nb, nselp)
    gw = NSA_HPG * NSA_HD
    ratio = SEL_BLOCK // CMP_STRIDE
    levels = 4 if (qpos_base == 0 and nb == ratio * nselp and nb % (4 * LANE) == 0 and sq >= SEL_BLOCK * nselp) else 1
    return pl.pallas_call(
        functools.partial(_cmp_topk_kernel, tq=tq, nb=nb, nselp=nselp, qpos_base=qpos_base, topk=min(SEL_TOP, nsel),
                          levels=levels),
        grid=(batch, NSA_GROUPS, nt),
        in_specs=[_smem_spec(),
                  pl.BlockSpec((tq, gw), lambda b, g, t: (b * nt + t, qcol0 + g)),
                  pl.BlockSpec((1, nb, LANE), lambda b, g, t: (b, 0, 0)),
                  pl.BlockSpec((1, nb, LANE), lambda b, g, t: (b, 0, 0)),
                  pl.BlockSpec((nb, nselp), lambda b, g, t: (0, 0))],
        out_specs=[pl.BlockSpec((tq, gw), lambda b, g, t: (b * nt + t, g)),
                   pl.BlockSpec((1, 1, tq, nselp), lambda b, g, t: (b, g, t, 0)),
                   pl.BlockSpec((1, 1, 1, 8, nselp), lambda b, g, t: (b, g, t, 0, 0))],
        out_shape=[jax.ShapeDtypeStruct((batch * sq, NSA_GROUPS * gw), F32),
                   jax.ShapeDtypeStruct((batch, NSA_GROUPS, sq, nselp), F32),
                   jax.ShapeDtypeStruct((batch, NSA_GROUPS, nt, 8, nselp), F32)],
        compiler_params=_cparams(("arbitrary", "arbitrary", "arbitrary")),
        name="cmp_topk",
    )(slopes, qarr, kcb, vcb, agg)


def _nsa_sw_kernel(fl_ref, sl_ref, q_ref, ks_ref, vs_ref, kw_ref, vw_ref, sel_ref, oc_ref, ng_ref, out_ref,
                   ids_ref, m_ref, acc_ref, *, tq, nt, nselp, nwords, sb):
    g = pl.program_id(0)
    t = pl.program_id(1)
    rows = NSA_HPG * tq
    qpad = _stack_group_queries(q_ref[...], g, tq)
    slope_row = jnp.concatenate([jnp.full((tq, 1), sl_ref[NSA_HPG * g + hh], F32) for hh in range(NSA_HPG)], axis=0)
    i_loc = lax.broadcasted_iota(jnp.int32, (rows, LANE), 0) & (tq - 1)
    j_loc = lax.broadcasted_iota(jnp.int32, (rows, LANE), 1)
    dloc = (i_loc - j_loc).astype(F32)
    sl_dloc = slope_row * dloc
    selb = sel_ref[0, 0].astype(BF16)
    ones = jnp.ones((LANE, LANE), BF16)

    def chunk(ref, c):
        return ref[pl.ds(pl.multiple_of(jnp.maximum(c, 0) * LANE, LANE), LANE), :]

    def with_ones(v):
        return jnp.concatenate([v, jnp.concatenate([ones] * (v.shape[0] // LANE), axis=0)], axis=1)

    def sel_mask(c):
        jrow = lax.broadcasted_iota(jnp.int32, (nselp, LANE), 0)
        r = lax.broadcasted_iota(jnp.int32, (nselp, LANE), 1)
        expand = jnp.where(jrow == 2 * c + r // SEL_BLOCK, 1.0, 0.0).astype(BF16)
        return jnp.dot(selb, expand, preferred_element_type=F32)

    def per_head(mk, x):
        x3 = x.reshape(NSA_HPG, tq, LANE)
        return jnp.where(mk[None] > 0.5, x3, NEG).reshape(rows, LANE)

    s = lax.dot_general(qpad, chunk(ks_ref, t), NT_DIMS, preferred_element_type=F32)
    mk = jnp.where(dloc[:tq] >= 0.0, sel_mask(t), 0.0)
    tt = per_head(mk, s - sl_dloc)
    m0 = jnp.max(tt, axis=-1, keepdims=True)
    p = jnp.exp2(tt - m0)
    m_ref[...] = m0
    acc_ref[...] = jnp.dot(p.astype(BF16), with_ones(chunk(vs_ref, t)), preferred_element_type=F32)

    def scan(c, cnt):
        word = fl_ref[(g * nt + t) * nwords + c // 32]
        bit = lax.shift_right_logical(word, c % 32) & 1

        @pl.when(bit == 1)
        def _():
            ids_ref[cnt] = c

        return cnt + bit

    cnt = lax.fori_loop(0, t, scan, 0)
    for k in range(sb):
        ids_ref[cnt + k] = -1

    def sel_step(si, carry):
        cs = [ids_ref[si * sb + k] for k in range(sb)]
        kk = jnp.concatenate([chunk(ks_ref, c) for c in cs], axis=0)
        vv = with_ones(jnp.concatenate([chunk(vs_ref, c) for c in cs], axis=0))
        keep = [sel_mask(c) > 0.5 for c in cs]
        offs = [((t - c) * tq).astype(F32) for c in cs]
        half = rows // 2
        for r0 in (0, half):
            rs = slice(r0, r0 + half)
            s = lax.dot_general(qpad[rs], kk, NT_DIMS, preferred_element_type=F32)
            slabs = []
            for k in range(sb):
                x = s[:, k * LANE:(k + 1) * LANE] - (sl_dloc[rs] + slope_row[rs] * offs[k])
                x = jnp.where(keep[k][None], x.reshape(NSA_HPG // 2, tq, LANE), NEG)
                slabs.append(x.reshape(half, LANE))
            tt = jnp.concatenate(slabs, axis=1)
            m_old = m_ref[rs, :]
            m_new = jnp.maximum(m_old, jnp.max(tt, axis=-1, keepdims=True))
            alpha = jnp.exp2(m_old - m_new)
            p = jnp.exp2(tt - m_new)
            acc_ref[rs, :] = alpha * acc_ref[rs, :] + jnp.dot(p.astype(BF16), vv, preferred_element_type=F32)
            m_ref[rs, :] = m_new
        return carry

    lax.fori_loop(0, (cnt + sb - 1) // sb, sel_step, 0)
    o_s = acc_ref[:, :LANE] / acc_ref[:, LANE:]

    nback = WINDOW // tq
    c0 = jnp.maximum(t - nback, 0)
    wlen = (nback + 1) * LANE
    wstart = pl.multiple_of(c0 * LANE, LANE)
    s = lax.dot_general(qpad, kw_ref[pl.ds(wstart, wlen), :], NT_DIMS, preferred_element_type=F32)
    slabs = []
    for k in range(nback + 1):
        dist = dloc + ((t - (c0 + k)) * tq).astype(F32)
        x = s[:, k * LANE:(k + 1) * LANE] - slope_row * dist
        slabs.append(jnp.where(dist >= 0.0, jnp.where(dist < float(WINDOW), x, NEG), NEG))
    tt = jnp.concatenate(slabs, axis=1)
    p = jnp.exp2(tt - jnp.max(tt, axis=-1, keepdims=True))
    aw = jnp.dot(p.astype(BF16), with_ones(vw_ref[pl.ds(wstart, wlen), :]), preferred_element_type=F32)
    o_w = aw[:, :LANE] / aw[:, LANE:]

    gate = _sigmoid(ng_ref[...])
    glane = lax.broadcasted_iota(jnp.int32, (tq, LANE), 1)

    def gate_col(idx):
        return jnp.sum(jnp.where(glane == idx, gate, 0.0), axis=-1, keepdims=True)

    for hh in range(NSA_HPG):
        base = 3 * (NSA_HPG * g + hh)
        r0, r1 = hh * tq, (hh + 1) * tq
        o = (gate_col(base) * oc_ref[:, NSA_HD * hh:NSA_HD * (hh + 1)]
             + gate_col(base + 1) * _group_half(o_s[r0:r1], g)
             + gate_col(base + 2) * _group_half(o_w[r0:r1], g))
        out_ref[:, NSA_HD * hh:NSA_HD * (hh + 1)] = o


def _pack_chunk_flags(anyblk):
    g, nt, nselp = anyblk.shape
    chunk = jnp.max(anyblk.reshape(g, nt, nselp // 2, 2), axis=-1) > 0.5
    nchunk = nselp // 2
    nwords = -(-nchunk // 32)
    chunk = jnp.pad(chunk, ((0, 0), (0, 0), (0, nwords * 32 - nchunk)))
    bits = chunk.reshape(g, nt, nwords, 32).astype(jnp.uint32) << jnp.arange(32, dtype=jnp.uint32)
    words = jnp.sum(bits, axis=-1, dtype=jnp.uint32)
    return lax.bitcast_convert_type(words, jnp.int32).reshape(-1), nwords


def _nsa_sw_prompt(z, zb, sel, anyblk, oc, slopes, tq=128, sb=4):
    s = zb.shape[0]
    assert tq == LANE and s >= WINDOW + tq
    nt = s // tq
    nselp = sel.shape[-1]
    flags, nwords = _pack_chunk_flags(anyblk[0, :, :, 0, :])
    gw = NSA_HPG * NSA_HD
    rows = NSA_HPG * tq
    res = lambda col: pl.BlockSpec((s, LANE), lambda g, t, fl: (0, col // LANE), pipeline_mode=pl.Buffered(1))
    grid_spec = pltpu.PrefetchScalarGridSpec(
        num_scalar_prefetch=1,
        grid=(NSA_GROUPS, nt),
        in_specs=[_smem_spec(),
                  pl.BlockSpec((tq, gw), lambda g, t, fl: (t, C_NQ // gw + g)),
                  res(C_KS), res(C_VS), res(C_KW), res(C_VW),
                  pl.BlockSpec((1, 1, tq, nselp), lambda g, t, fl: (0, g, t, 0)),
                  pl.BlockSpec((tq, gw), lambda g, t, fl: (t, g)),
                  pl.BlockSpec((tq, LANE), lambda g, t, fl: (t, C_NG // LANE))],
        out_specs=pl.BlockSpec((tq, gw), lambda g, t, fl: (t, g)),
        scratch_shapes=[pltpu.SMEM((nt + sb,), jnp.int32), pltpu.VMEM((rows, 1), F32),
                        pltpu.VMEM((rows, 2 * LANE), F32)],
    )
    return pl.pallas_call(
        functools.partial(_nsa_sw_kernel, tq=tq, nt=nt, nselp=nselp, nwords=nwords, sb=sb),
        grid_spec=grid_spec,
        out_shape=jax.ShapeDtypeStruct((s, NSA_GROUPS * gw), F32),
        compiler_params=_cparams(("arbitrary", "arbitrary")),
        name="nsa_sel_win",
    )(flags, slopes, zb, zb, zb, zb, zb, sel, oc, z)


def _decode_kernel(pt_ref, pos_ref, cnt_ref, *refs, pp, n_pages, ncols, hk, qpos0, win, n_new, has_sel, nselp,
                   feature_major):
    del pt_ref
    kpages = refs[:pp]
    vpages = refs[pp:2 * pp]
    rest = refs[2 * pp:]
    if has_sel:
        wq_ref, ci_ref, kn_ref, vn_ref, sel_ref, o_ref, m_ref, s_ref, acc_ref = rest
    else:
        wq_ref, ci_ref, kn_ref, vn_ref, o_ref, m_ref, s_ref, acc_ref = rest
        sel_ref = None
    bi = pl.program_id(0)
    st = pl.program_id(1)
    windowed = win < 1e8
    guarded = has_sel or windowed

    @pl.when(st == 0)
    def _():
        m_ref[...] = jnp.full(m_ref.shape, NEG, F32)
        s_ref[...] = jnp.zeros(s_ref.shape, F32)
        acc_ref[...] = jnp.zeros(acc_ref.shape, F32)

    wq = wq_ref[0]
    slope = ci_ref[:, 0:1]
    qrel = ci_ref[:, 1:2]
    colhead = ci_ref[:, 2:3]

    def geometry(nrows):
        r = lax.broadcasted_iota(jnp.int32, (ncols, nrows), 1)
        kidx = (r // hk).astype(F32)
        base = slope * kidx
        if hk > 1:
            base = jnp.where((r % hk).astype(F32) == colhead, base, NEG)
        return r, kidx, base

    def scores(k, kpos0, n_valid, geom, check_range, feature_major):
        r, kidx, base = geom
        if feature_major:
            s = jnp.dot(wq, k.astype(BF16), preferred_element_type=F32)
        else:
            s = lax.dot_general(wq, k.astype(BF16), NT_DIMS, preferred_element_type=F32)
        off = qrel + (qpos0 - kpos0).astype(F32)
        t = (s + base) - slope * off
        if not (check_range or has_sel or n_valid is not None):
            return t, None
        okf = jnp.ones(t.shape, F32)
        if check_range:
            dist = off - kidx
            okf = jnp.where(dist >= 0.0, jnp.where(dist < win, 1.0, 0.0), 0.0)
        if n_valid is not None:
            okf = jnp.where(r < n_valid, okf, 0.0)
        if has_sel:
            nrows = t.shape[1]
            jrow = lax.broadcasted_iota(jnp.int32, (nselp, nrows), 0)
            kp = kpos0 + lax.broadcasted_iota(jnp.int32, (nselp, nrows), 1)
            expand = jnp.where(jrow == kp // SEL_BLOCK, 1.0, 0.0).astype(BF16)
            okf = okf * jnp.dot(sel_ref[0].astype(BF16), expand, preferred_element_type=F32)
        ok = okf > 0.5
        return jnp.where(ok, t, NEG), ok

    def update(ts, oks, vs, feature_major):
        m_old = m_ref[...]
        m_new = m_old
        for t in ts:
            m_new = jnp.maximum(m_new, jnp.max(t, axis=-1, keepdims=True))
        alpha = jnp.exp2(m_old - m_new)
        l = alpha * s_ref[...]
        acc = alpha * acc_ref[...]
        for t, ok, v in zip(ts, oks, vs):
            p = jnp.exp2(t - m_new)
            if ok is not None:
                p = jnp.where(ok, p, 0.0)
            l = l + jnp.sum(p, axis=-1, keepdims=True)
            if feature_major:
                acc = acc + lax.dot_general(p.astype(BF16), v.astype(BF16), NT_DIMS, preferred_element_type=F32)
            else:
                acc = acc + jnp.dot(p.astype(BF16), v.astype(BF16), preferred_element_type=F32)
        s_ref[...] = l
        acc_ref[...] = acc
        m_ref[...] = m_new

    cnt = cnt_ref[bi]

    @pl.when(st * pp < cnt)
    def _():
        geom = geometry(PAGE * hk)
        ts, oks, vs = [], [], []
        for k in range(pp):
            slot = st * pp + k
            n_valid = jnp.where(slot < cnt, PAGE * hk, 0) if guarded else None
            t, ok = scores(kpages[k][0], pos_ref[bi * n_pages + slot], n_valid, geom, windowed, feature_major)
            ts.append(t)
            oks.append(ok)
            vs.append(vpages[k][0])
        update(ts, oks, vs, feature_major)

    @pl.when(st == pl.num_programs(1) - 1)
    def _():
        t, ok = scores(kn_ref[0], jnp.int32(qpos0), n_new * hk, geometry(PAGE), True, False)
        update([t], [ok], [vn_ref[0]], False)
        l = s_ref[...]
        o_ref[0] = acc_ref[...] / jnp.where(l == 0.0, 1.0, l)


def _decode_attn(wq, colinfo, pool_k, pool_v, page_ids, page_pos, page_cnt, knew, vnew, sel, *, hk, qpos0, win, n_new, pp,
                 feature_major=False):
    b, n_pages = page_ids.shape
    ncols, width = wq.shape[1], wq.shape[2]
    pp = _pages_per_step(n_pages, pp)
    has_sel = sel is not None
    nselp = sel.shape[-1] if has_sel else 0

    def page_spec(k):
        return pl.BlockSpec((1, width, PAGE) if feature_major else (1, PAGE * hk, width),
                            lambda bi, s, pt, pos, cnt: (pt[bi * n_pages + s * pp + k], 0, 0))

    per_batch = lambda shape: pl.BlockSpec((1,) + shape, lambda bi, s, pt, pos, cnt: (bi, 0, 0))
    in_specs = ([page_spec(k) for k in range(pp)] * 2
                + [per_batch((ncols, width)), pl.BlockSpec((ncols, LANE), lambda bi, s, pt, pos, cnt: (0, 0)),
                   per_batch((PAGE, width)), per_batch((PAGE, width))])
    args = [pool_k] * pp + [pool_v] * pp + [wq, colinfo, knew, vnew]
    if has_sel:
        in_specs.append(per_batch((ncols, nselp)))
        args.append(sel)
    grid_spec = pltpu.PrefetchScalarGridSpec(
        num_scalar_prefetch=3,
        grid=(b, n_pages // pp),
        in_specs=in_specs,
        out_specs=per_batch((ncols, width)),
        scratch_shapes=[pltpu.VMEM((ncols, 1), F32), pltpu.VMEM((ncols, 1), F32), pltpu.VMEM((ncols, width), F32)],
    )
    return pl.pallas_call(
        functools.partial(_decode_kernel, pp=pp, n_pages=n_pages, ncols=ncols, hk=hk, qpos0=qpos0, win=float(win),
                          n_new=n_new, has_sel=has_sel, nselp=nselp, feature_major=feature_major),
        grid_spec=grid_spec,
        out_shape=jax.ShapeDtypeStruct((b, ncols, width), F32),
        compiler_params=_cparams(("arbitrary", "arbitrary")),
        name="decode_attn",
    )(page_ids.reshape(-1), page_pos.reshape(-1), page_cnt, *args)


def _all_pages(page_table, pos0):
    b, n_pages = page_table.shape
    pos = jnp.broadcast_to(pos0 + PAGE * jnp.arange(n_pages, dtype=jnp.int32), (b, n_pages))
    return page_table, pos, jnp.full((b,), n_pages, jnp.int32)


def _selected_pages(page_table, sel_cols):
    b, n_pages = page_table.shape
    per_page = PAGE // SEL_BLOCK
    hit = jnp.max(sel_cols[:, :, :per_page * n_pages].reshape(b, -1, n_pages, per_page), axis=(1, 3)) > 0.5
    cnt = jnp.sum(hit, axis=1).astype(jnp.int32)
    order = jnp.argsort(jnp.logical_not(hit), axis=1, stable=True).astype(jnp.int32)
    keep = jnp.minimum(jnp.arange(n_pages, dtype=jnp.int32)[None], jnp.maximum(cnt - 1, 0)[:, None])
    order = jnp.take_along_axis(order, keep, axis=1)
    return jnp.take_along_axis(page_table, order, axis=1), order * PAGE, cnt


def _da_post_kernel(sc_ref, o0_ref, o1_ref, sub_ref, o_ref):
    o_ref[...] = _da_post(o0_ref[...], o1_ref[...], sc_ref[0], sub_ref[...])


def _da_post_call(o0, o1, sc, subln):
    return pl.pallas_call(
        _da_post_kernel,
        in_specs=[_smem_spec(), pl.BlockSpec(o0.shape, lambda: (0, 0)), pl.BlockSpec(o0.shape, lambda: (0, 0)),
                  pl.BlockSpec((1, LANE), lambda: (0, 0))],
        out_specs=pl.BlockSpec(o0.shape, lambda: (0, 0)),
        out_shape=jax.ShapeDtypeStruct(o0.shape, F32),
        name="da_post",
    )(sc, o0, o1, subln.reshape(1, LANE))


def _gate3_kernel(ng_ref, e_ref, oc_ref, os_ref, ow_ref, o_ref):
    gate = _sigmoid(ng_ref[...])
    acc = jnp.zeros(o_ref.shape, F32)
    for br, ref in enumerate((oc_ref, os_ref, ow_ref)):
        ge = jnp.dot(gate, e_ref[br], preferred_element_type=F32, precision=lax.Precision.HIGHEST)
        acc = acc + ge * ref[...]
    o_ref[...] = acc


def _gate3(ng, oc, os_, ow):
    e = np.zeros((3, LANE, NSA_HEADS * NSA_HD), np.float32)
    for br in range(3):
        for h in range(NSA_HEADS):
            e[br, 3 * h + br, NSA_HD * h:NSA_HD * (h + 1)] = 1.0
    return pl.pallas_call(
        _gate3_kernel,
        out_shape=jax.ShapeDtypeStruct(oc.shape, F32),
        name="gate3",
    )(ng, jnp.asarray(e), oc, os_, ow)


def _tail_a_kernel(x_ref, da_ref, nsa_ref, ga_ref, gb_ref, wo_ref, nx_ref, wxq_ref, h_ref, q_ref):
    m = _sigmoid(ga_ref[...]) * da_ref[...] + _sigmoid(gb_ref[...]) * nsa_ref[...]
    h = x_ref[...] + jnp.dot(m.astype(BF16), wo_ref[...], preferred_element_type=F32)
    h_ref[...] = h
    xn = _rms(h, nx_ref[...]).astype(BF16)
    q_ref[...] = jnp.dot(xn, wxq_ref[...], preferred_element_type=F32)


def _tail_a(x, o_da, o_nsa, z, w_o, norm_x, w_xq):
    m = x.shape[0]
    tm = min(m, 512)
    row = lambda cb: pl.BlockSpec((tm, D_MODEL), lambda i: (i, cb))
    const = lambda shape: pl.BlockSpec(shape, lambda i: (0, 0), pipeline_mode=pl.Buffered(1))
    return pl.pallas_call(
        _tail_a_kernel,
        grid=(m // tm,),
        in_specs=[row(0), row(0), row(0), row(C_GA // D_MODEL), row(C_GB // D_MODEL),
                  const((D_MODEL, D_MODEL)), const((1, D_MODEL)), const((D_MODEL, X_W))],
        out_specs=[row(0), pl.BlockSpec((tm, X_W), lambda i: (i, 0))],
        out_shape=[jax.ShapeDtypeStruct((m, D_MODEL), F32), jax.ShapeDtypeStruct((m, X_W), F32)],
        compiler_params=_cparams(("parallel",)),
        name="tail_merge_wo",
    )(x, o_da, o_nsa, z, z, w_o.astype(BF16), norm_x.reshape(1, D_MODEL), w_xq.astype(BF16))


def _cross_kernel(q_ref, mk_ref, mv_ref, o_ref):
    q = (q_ref[0] * (X_HD ** -0.5)).astype(BF16)
    mk = mk_ref[0].astype(BF16)
    mv = mv_ref[0].astype(BF16)
    outs = []
    for h in range(X_HEADS):
        sl = slice(X_HD * h, X_HD * (h + 1))
        s = lax.dot_general(q[:, sl], mk[:, sl], NT_DIMS, preferred_element_type=F32)
        e = jnp.exp(s - jnp.max(s, axis=-1, keepdims=True))
        p = e / jnp.sum(e, axis=-1, keepdims=True)
        outs.append(jnp.dot(p.astype(BF16), mv[:, sl], preferred_element_type=F32))
    o_ref[0] = jnp.concatenate(outs, axis=-1)


def _cross(q, mk, mv):
    b, t, _ = q.shape
    mlen = mk.shape[1]
    tt = min(t, 512)
    return pl.pallas_call(
        _cross_kernel,
        grid=(b, t // tt),
        in_specs=[pl.BlockSpec((1, tt, X_W), lambda bi, i: (bi, i, 0)),
                  pl.BlockSpec((1, mlen, X_W), lambda bi, i: (bi, 0, 0)),
                  pl.BlockSpec((1, mlen, X_W), lambda bi, i: (bi, 0, 0))],
        out_specs=pl.BlockSpec((1, tt, X_W), lambda bi, i: (bi, i, 0)),
        out_shape=jax.ShapeDtypeStruct((b, t, X_W), F32),
        compiler_params=_cparams(("parallel", "parallel")),
        name="cross_attn",
    )(q, mk, mv)


def _tail_c_kernel(h_ref, ox_ref, wxo_ref, nf_ref, wg_ref, wu_ref, wd_ref, nfin_ref, y_ref, *, nchunk, chunk):
    h = h_ref[...] + jnp.dot(ox_ref[...].astype(BF16), wxo_ref[...], preferred_element_type=F32)
    xn = _rms(h, nf_ref[...]).astype(BF16)
    acc = jnp.zeros(h.shape, F32)
    for c in range(nchunk):
        sl = slice(c * chunk, (c + 1) * chunk)
        gt = jnp.dot(xn, wg_ref[:, sl], preferred_element_type=F32)
        up = jnp.dot(xn, wu_ref[:, sl], preferred_element_type=F32)
        act = (gt * _sigmoid(gt) * up).astype(BF16)
        acc = acc + jnp.dot(act, wd_ref[sl, :], preferred_element_type=F32)
    y_ref[...] = _rms(h + acc, nfin_ref[...])


def _tail_c(h, ox, w_xo, norm_ffn, w_gate_up, w_down, norm_final):
    m = h.shape[0]
    tm = min(m, 512)
    hid = w_down.shape[0]
    chunk = hid // 2
    const = lambda shape: pl.BlockSpec(shape, lambda i: (0, 0), pipeline_mode=pl.Buffered(1))
    return pl.pallas_call(
        functools.partial(_tail_c_kernel, nchunk=2, chunk=chunk),
        grid=(m // tm,),
        in_specs=[pl.BlockSpec((tm, D_MODEL), lambda i: (i, 0)), pl.BlockSpec((tm, X_W), lambda i: (i, 0)),
                  const((X_W, D_MODEL)), const((1, D_MODEL)), const((D_MODEL, hid)), const((D_MODEL, hid)),
                  const((hid, D_MODEL)), const((1, D_MODEL))],
        out_specs=pl.BlockSpec((tm, D_MODEL), lambda i: (i, 0)),
        out_shape=jax.ShapeDtypeStruct((m, D_MODEL), F32),
        compiler_params=_cparams(("parallel",)),
        name="tail_ffn",
    )(h, ox, w_xo.astype(BF16), norm_ffn.reshape(1, D_MODEL), w_gate_up[:, :hid].astype(BF16),
      w_gate_up[:, hid:].astype(BF16), w_down.astype(BF16), norm_final.reshape(1, D_MODEL))


def _alibi(n):
    return np.asarray(2.0 ** (-8.0 * np.arange(1, n + 1) / n) * LOG2E, dtype=np.float32)


def _finish(x, o_da, o_nsa, z, mk, mv, batch, w_o, norm_x, w_xq, w_xo, norm_ffn, w_gate_up, w_down, norm_final):
    m = x.shape[0]
    t = m // batch
    h1, qx = _tail_a(x, o_da, o_nsa, z, w_o, norm_x, w_xq)
    q3 = qx.reshape(batch, t, X_W)
    tpad = -(-t // 8) * 8
    if tpad != t:
        q3 = jnp.pad(q3, ((0, 0), (0, tpad - t), (0, 0)))
    ox = _cross(q3, mk, mv)[:, :t].reshape(m, X_W)
    return _tail_c(h1, ox, w_xo, norm_ffn, w_gate_up, w_down, norm_final)


def kernel(x_prompt, x_sample, cache_diff_k, cache_diff_v, cache_cmp_k, cache_cmp_v, cache_sel_k, cache_sel_v,
           cache_win_k, cache_win_v, cache_mem_k, cache_mem_v, page_table, mem_prompt,
           norm_mix, w_in, lam_q1, lam_k1, lam_q2, lam_k2, da_subln,
           w_cmp_k1, pe_cmp_k, w_cmp_k2, w_cmp_v1, pe_cmp_v, w_cmp_v2,
           w_o, norm_x, w_xq, w_mem_kv, w_xo, norm_ffn, w_gate_up, w_down, norm_final):
    batch, seq, _ = x_prompt.shape
    db, ds, _ = x_sample.shape
    assert batch == 1 and norm_mix.shape[0] == 1
    n_pages = page_table.shape[1]
    past = n_pages * PAGE
    wb = cache_win_k.shape[2]
    kvw = NSA_GROUPS * NSA_HD

    lam = (jnp.exp(jnp.sum(lam_q1[0] * lam_k1[0])) - jnp.exp(jnp.sum(lam_q2[0] * lam_k2[0])) + LAM_INIT).astype(F32)
    da_sc = jnp.concatenate([lam.reshape(1), jnp.asarray(_alibi(DA_HEADS))])
    nsa_sl = jnp.asarray(_alibi(NSA_HEADS))
    w_pad = _prep_w_in(w_in[0])
    tail_w = (w_o[0], norm_x[0], w_xq[0], w_xo[0], norm_ffn[0], w_gate_up[0], w_down[0], norm_final)
    cmp_k = (w_cmp_k1[0], pe_cmp_k[0], w_cmp_k2[0])
    cmp_v = (w_cmp_v1[0], pe_cmp_v[0], w_cmp_v2[0])

    xp = x_prompt.reshape(seq, D_MODEL)
    z, zb = _inproj(xp, norm_mix[0], w_pad)
    o_da = _da_prompt(zb, da_sc, da_subln[0])

    ident = jnp.arange(seq // PAGE, dtype=jnp.int32).reshape(1, -1)
    zero_tail = jnp.zeros((1, 8, CMP_STRIDE * kvw), F32)
    p_kc = z[:, C_KC:C_KC + kvw]
    p_vc = z[:, C_VC:C_VC + kvw]
    as_pages = lambda a: a.reshape(seq // PAGE, PAGE, kvw).transpose(0, 2, 1)
    kcb = _compress(as_pages(p_kc), ident, zero_tail, *cmp_k)
    vcb = _compress(as_pages(p_vc), ident, zero_tail, *cmp_v)
    nch = seq // CMP_STRIDE
    oc, sel, anyblk = _cmp_topk(zb, C_NQ // (NSA_HPG * NSA_HD), kcb, vcb, nsa_sl, batch=1, sq=seq, tq=128,
                                nblk=nch - 1, nsel=seq // SEL_BLOCK, qpos_base=0)
    o_nsa = _nsa_sw_prompt(z, zb, sel, anyblk, oc, nsa_sl)

    mem_kv = _matmul(mem_prompt.reshape(-1, D_MODEL), w_mem_kv[0].astype(BF16))
    p_mk, p_mv = mem_kv[:, :X_W], mem_kv[:, X_W:]
    y_prompt = _finish(xp, o_da, o_nsa, z, p_mk[None], p_mv[None], 1, *tail_w)

    r5 = lambda a, h: a.reshape(1, 1, a.shape[0], h, -1)
    p_states = (r5(z[:, C_DK:C_DK + 1024], DA_HEADS), r5(z[:, C_DV:C_DV + 1024], DA_HEADS),
                r5(p_kc, NSA_GROUPS), r5(p_vc, NSA_GROUPS),
                r5(z[:, C_KS:C_KS + kvw], NSA_GROUPS), r5(z[:, C_VS:C_VS + kvw], NSA_GROUPS),
                r5(z[seq - min(WINDOW, seq):, C_KW:C_KW + kvw], NSA_GROUPS),
                r5(z[seq - min(WINDOW, seq):, C_VW:C_VW + kvw], NSA_GROUPS),
                r5(p_mk, X_HEADS), r5(p_mv, X_HEADS))

    ms = db * ds
    xs = x_sample.reshape(ms, D_MODEL)
    zs, zsb = _inproj(xs, norm_mix[0], w_pad)
    z3 = zs.reshape(db, ds, ZP)
    pad_rows = lambda a, n: jnp.pad(a, ((0, 0), (0, n - a.shape[1]), (0, 0)))

    dq = z3[:, :, C_DQ:C_DQ + 1024].reshape(db, ds, DA_HEADS, 2, DA_HD)
    wq_da = jnp.einsum('bqhmd,mn->bhmqnd', dq, jnp.eye(2, dtype=F32))
    wq_da = wq_da.reshape(db, DA_HEADS * 2 * ds, 2 * DA_HD).astype(BF16)
    ci = np.zeros((DA_HEADS * 2 * ds, LANE), np.float32)
    ci[:, 0] = np.repeat(_alibi(DA_HEADS), 2 * ds)
    ci[:, 1] = np.tile(np.arange(ds), DA_HEADS * 2)
    ci[:, 2] = np.repeat(np.arange(DA_HEADS), 2 * ds)
    s_dk, s_dv = z3[:, :, C_DK:C_DK + 1024], z3[:, :, C_DV:C_DV + 1024]
    head_rows = lambda a: pad_rows(a.reshape(db, ds * DA_HEADS, 2 * DA_HD), PAGE)
    o_pair = _decode_attn(wq_da, jnp.asarray(ci), cache_diff_k[0].reshape(-1, PAGE * DA_HEADS, 2 * DA_HD),
                          cache_diff_v[0].reshape(-1, PAGE * DA_HEADS, 2 * DA_HD), *_all_pages(page_table, 0),
                          head_rows(s_dk), head_rows(s_dv), None, hk=DA_HEADS, qpos0=past, win=1e9, n_new=ds, pp=16)
    o_pair = o_pair.reshape(db, DA_HEADS, 2, ds, 2 * DA_HD).transpose(2, 0, 3, 1, 4)
    o_da_s = _da_post_call(o_pair[0].reshape(ms * DA_HEADS, LANE), o_pair[1].reshape(ms * DA_HEADS, LANE),
                           da_sc, da_subln[0]).reshape(ms, DA_HEADS * LANE)

    ck = CMP_STRIDE * kvw
    s_kc, s_vc = z3[:, :, C_KC:C_KC + kvw], z3[:, :, C_VC:C_VC + kvw]
    tail_of = lambda a: jnp.pad(a.reshape(db, 1, ds * kvw), ((0, 0), (0, 7), (0, ck - ds * kvw)))
    feat_major = lambda c: c[0].transpose(0, 2, 3, 1).reshape(-1, kvw, PAGE)
    kcb_s = _compress(feat_major(cache_cmp_k), page_table, tail_of(s_kc), *cmp_k)
    vcb_s = _compress(feat_major(cache_cmp_v), page_table, tail_of(s_vc), *cmp_v)
    tq_s = 16
    nq_pad = pad_rows(zsb.reshape(db, ds, ZP)[:, :, C_NQ:C_NQ + 1024], tq_s).reshape(db * tq_s, 1024)
    nsel_s = -(-(past + ds) // SEL_BLOCK)
    oc_s, sel_s, _ = _cmp_topk(nq_pad, 0, kcb_s, vcb_s, nsa_sl, batch=db, sq=tq_s, tq=tq_s,
                               nblk=(past + ds + CMP_STRIDE - 1) // CMP_STRIDE - 1, nsel=nsel_s, qpos_base=past)
    oc_s = oc_s.reshape(db, tq_s, 1024)[:, :ds].reshape(ms, 1024)

    nq = z3[:, :, C_NQ:C_NQ + 1024].reshape(db, ds, NSA_GROUPS, NSA_HPG, NSA_HD)
    wq_n = jnp.einsum('bqghd,gk->bghqkd', nq, jnp.eye(NSA_GROUPS, dtype=F32))
    wq_n = wq_n.reshape(db, NSA_HEADS * ds, kvw).astype(BF16)
    cn = np.zeros((NSA_HEADS * ds, LANE), np.float32)
    cn[:, 0] = np.repeat(_alibi(NSA_HEADS), ds)
    cn[:, 1] = np.tile(np.arange(ds), NSA_HEADS)
    cn = jnp.asarray(cn)
    sel_cols = jnp.repeat(sel_s[:, :, None, :ds, :], NSA_HPG, axis=2).reshape(db, NSA_HEADS * ds, -1)
    s_ks, s_vs = z3[:, :, C_KS:C_KS + kvw], z3[:, :, C_VS:C_VS + kvw]
    s_kw, s_vw = z3[:, :, C_KW:C_KW + kvw], z3[:, :, C_VW:C_VW + kvw]
    o_sel = _decode_attn(wq_n, cn, feat_major(cache_sel_k), feat_major(cache_sel_v),
                         *_selected_pages(page_table, sel_cols), pad_rows(s_ks, PAGE), pad_rows(s_vs, PAGE), sel_cols,
                         hk=1, qpos0=past, win=1e9, n_new=ds, pp=16, feature_major=True)
    win_pages = wb // PAGE
    win_pt = jnp.arange(db * win_pages, dtype=jnp.int32).reshape(db, win_pages)
    o_win = _decode_attn(wq_n, cn, cache_win_k[0].reshape(-1, PAGE, kvw), cache_win_v[0].reshape(-1, PAGE, kvw),
                         *_all_pages(win_pt, past - wb), pad_rows(s_kw, PAGE), pad_rows(s_vw, PAGE), None,
                         hk=1, qpos0=past, win=WINDOW, n_new=ds, pp=win_pages)

    def own_group(o):
        o = o.reshape(db, NSA_GROUPS, NSA_HPG, ds, NSA_GROUPS, NSA_HD)
        o = jnp.stack([o[:, g, :, :, g] for g in range(NSA_GROUPS)], axis=1)
        return o.transpose(0, 3, 1, 2, 4).reshape(ms, NSA_HEADS * NSA_HD)

    o_nsa_s = _gate3(zs[:, C_NG:C_NG + LANE], oc_s, own_group(o_sel), own_group(o_win))
    s_mk = cache_mem_k[0].reshape(db, -1, X_W)
    s_mv = cache_mem_v[0].reshape(db, -1, X_W)
    y_sample = _finish(xs, o_da_s, o_nsa_s, zs, s_mk, s_mv, db, *tail_w)

    s5 = lambda a, h: a.reshape(1, db, a.shape[1], h, -1)
    new_win = lambda c, a: jnp.concatenate([c[0].reshape(db, wb, kvw), a], axis=1)[:, ds:]
    s_states = (s5(s_dk, DA_HEADS), s5(s_dv, DA_HEADS), s5(s_kc, NSA_GROUPS), s5(s_vc, NSA_GROUPS),
                s5(s_ks, NSA_GROUPS), s5(s_vs, NSA_GROUPS),
                s5(new_win(cache_win_k, s_kw), NSA_GROUPS), s5(new_win(cache_win_v, s_vw), NSA_GROUPS))

    return (y_prompt.reshape(1, seq, D_MODEL), y_sample.reshape(db, ds, D_MODEL)) + p_states + s_states
```

```python
import functools

import numpy as np
import jax
import jax.numpy as jnp
from jax import lax
from jax.experimental import pallas as pl
from jax.experimental.pallas import tpu as pltpu

F32 = jnp.float32
BF16 = jnp.bfloat16

D_MODEL = 1024
DA_HEADS = 8
DA_HD = 64
NSA_HEADS = 16
NSA_GROUPS = 2
NSA_HPG = NSA_HEADS // NSA_GROUPS
NSA_HD = 64
CMP_LEN = 32
CMP_STRIDE = 16
SEL_BLOCK = 64
SEL_TOP = 16
WINDOW = 512
X_HEADS = 4
X_HD = 64
X_W = X_HEADS * X_HD
EPS = 1e-6
NEG = -1e30
PICKED = -3e38
FORCE_BONUS = 1e6
LAM_INIT = 0.2
LANE = 128
PAGE = 128
VMEM_LIMIT = 56 * 1024 * 1024

C_DQ, C_DK, C_DV, C_NQ, C_GA, C_GB = 0, 1024, 2048, 3072, 4096, 5120
C_KC, C_VC, C_KS, C_VS, C_KW, C_VW, C_NG = 6144, 6272, 6400, 6528, 6656, 6784, 6912
ZP = 7040
ZP_TILE = 1408

NT_DIMS = (((1,), (1,)), ((), ()))
LOG2E = 1.4426950408889634
Q_SCALE = DA_HD ** -0.5 * LOG2E


def _cparams(sem):
    return pltpu.CompilerParams(dimension_semantics=sem, vmem_limit_bytes=VMEM_LIMIT)


def _smem_spec():
    return pl.BlockSpec(memory_space=pltpu.SMEM)


def _pages_per_step(n_pages, cap):
    return max(p for p in range(1, cap + 1) if n_pages % p == 0)


def _sigmoid(x):
    return 1.0 / (1.0 + jnp.exp(-x))


def _rms(x, g):
    return x * lax.rsqrt(jnp.mean(x * x, axis=-1, keepdims=True) + EPS) * g


def _inproj_kernel(x_ref, g_ref, w_ref, o_ref, ob_ref, xn_ref):
    @pl.when(pl.program_id(1) == 0)
    def _():
        xn_ref[...] = _rms(x_ref[...], g_ref[...]).astype(BF16)

    acc = jnp.dot(xn_ref[...], w_ref[...], preferred_element_type=F32)
    o_ref[...] = acc
    ob_ref[...] = acc.astype(BF16)


def _inproj(x, g, w_pad):
    m = x.shape[0]
    tm = min(m, 512)
    return pl.pallas_call(
        _inproj_kernel,
        grid=(m // tm, ZP // ZP_TILE),
        in_specs=[pl.BlockSpec((tm, D_MODEL), lambda i, j: (i, 0)),
                  pl.BlockSpec((1, D_MODEL), lambda i, j: (0, 0)),
                  pl.BlockSpec((D_MODEL, ZP_TILE), lambda i, j: (0, j))],
        out_specs=[pl.BlockSpec((tm, ZP_TILE), lambda i, j: (i, j)),
                   pl.BlockSpec((tm, ZP_TILE), lambda i, j: (i, j))],
        out_shape=[jax.ShapeDtypeStruct((m, ZP), F32), jax.ShapeDtypeStruct((m, ZP), BF16)],
        scratch_shapes=[pltpu.VMEM((tm, D_MODEL), BF16)],
        compiler_params=_cparams(("parallel", "arbitrary")),
        name="inproj",
    )(x, g.reshape(1, D_MODEL), w_pad)


def _prep_w_in(w_in):
    a = jnp.concatenate([w_in[:, :1024] * Q_SCALE, w_in[:, 1024:3072], w_in[:, 3072:4096] * Q_SCALE], axis=1)
    kv = w_in[:, 4096:4864]
    ng = w_in[:, 4864:4912]
    mg = w_in[:, 4912:]
    return jnp.concatenate([a, mg, kv, jnp.pad(ng, ((0, 0), (0, LANE - ng.shape[1])))], axis=1).astype(BF16)


def _matmul_kernel(x_ref, w_ref, o_ref):
    o_ref[...] = jnp.dot(x_ref[...].astype(BF16), w_ref[...], preferred_element_type=F32)


def _matmul(x, w_bf16):
    m, n = x.shape[0], w_bf16.shape[1]
    return pl.pallas_call(
        _matmul_kernel,
        out_shape=jax.ShapeDtypeStruct((m, n), F32),
        compiler_params=pltpu.CompilerParams(vmem_limit_bytes=VMEM_LIMIT),
        name="matmul",
    )(x, w_bf16)


def _da_post(o0, o1, lam, sub):
    o = o0 - lam * o1
    return _rms(o, sub) * (1.0 - LAM_INIT)


def _split3(x):
    a = x.astype(BF16).astype(F32)
    r = x - a
    b = r.astype(BF16).astype(F32)
    return a, b, (r - b).astype(BF16).astype(F32)


def _thirds(lane, pieces):
    which = lane % 3
    return jnp.where(which == 0, pieces[0], jnp.where(which == 1, pieces[1], pieces[2]))


def _da_prompt_kernel(sc_ref, q_ref, k_ref, v_ref, sub_ref, o_ref, ka_ref, va_ref, sa_ref, sb_ref, m_ref, acc_ref,
                      *, tq, tkb):
    h = pl.program_id(0)
    qi = pl.program_id(1)
    lam = sc_ref[0]
    slope = sc_ref[1 + h]
    per_big = tkb // tq
    seq = ka_ref.shape[0]

    @pl.when(qi == 0)
    def _():
        j = lax.broadcasted_iota(jnp.int32, (tkb, LANE), 0)
        grp = lax.broadcasted_iota(jnp.int32, (tkb, LANE), 1) // 3
        kfeat = jnp.where(grp == 0, (j // 32) * 32, jnp.where(grp == 1, j % 32, jnp.where(grp == 2, 1, 0)))
        kfeat = kfeat.astype(F32).astype(BF16)
        ka_ref[:, :LANE] = k_ref[...]
        for c in range(seq // tkb):
            ka_ref[c * tkb:(c + 1) * tkb, LANE:] = kfeat
        va_ref[:, :LANE] = v_ref[...]
        va_ref[:, LANE:] = jnp.ones((seq, LANE), BF16)

    q = q_ref[...]
    lane = lax.broadcasted_iota(jnp.int32, (tq, LANE), 1)
    zero = jnp.zeros_like(q)
    qbd = jnp.concatenate([jnp.where(lane < DA_HD, q, zero), jnp.where(lane >= DA_HD, q, zero)], axis=0)
    lane2 = lax.broadcasted_iota(jnp.int32, (2 * tq, LANE), 1)
    i_loc = (lax.broadcasted_iota(jnp.int32, (2 * tq, LANE), 0) & (tq - 1)).astype(F32)
    sl_pieces = _thirds(lane2, _split3(jnp.full((2 * tq, LANE), slope, F32)))
    row_pieces = _thirds(lane2, _split3(-slope * i_loc))
    qfeat = jnp.where(lane2 < 6, sl_pieces, jnp.where(lane2 < 9, row_pieces, 0.0))
    qaug = jnp.concatenate([qbd, qfeat.astype(BF16)], axis=1)
    m_ref[...] = jnp.full(m_ref.shape, NEG, F32)
    acc_ref[...] = jnp.zeros(acc_ref.shape, F32)

    def scores(idx):
        k = ka_ref[pl.ds(pl.multiple_of(idx * tkb, tkb), tkb), :]
        return lax.dot_general(qaug, k, NT_DIMS, preferred_element_type=F32)

    def consume(idx, s_ref, masked):
        base = qi * tq - idx * tkb
        cc = slope * base.astype(F32)
        va = va_ref[pl.ds(pl.multiple_of(idx * tkb, tkb), tkb), :]
        m_old = m_ref[...]
        if masked:
            ii = lax.broadcasted_iota(jnp.int32, s_ref.shape, 0) & (tq - 1)
            jj = lax.broadcasted_iota(jnp.int32, s_ref.shape, 1)
            t = jnp.where(jj - ii <= base, s_ref[...], NEG)
            m_new = jnp.maximum(m_old, jnp.max(t, axis=-1, keepdims=True) - cc)
            p = jnp.exp2(t - (m_new + cc))
        else:
            m_new = jnp.maximum(m_old, jnp.max(s_ref[...], axis=-1, keepdims=True) - cc)
            p = jnp.exp2(s_ref[...] - (m_new + cc))
        alpha = jnp.exp2(m_old - m_new)
        acc_ref[...] = alpha * acc_ref[...] + jnp.dot(p.astype(BF16), va, preferred_element_type=F32)
        m_ref[...] = m_new

    nbig = qi // per_big
    sa_ref[...] = scores(0)

    def pair(j, carry):
        sb_ref[...] = scores(2 * j + 1)
        consume(2 * j, sa_ref, False)
        sa_ref[...] = scores(2 * j + 2)
        consume(2 * j + 1, sb_ref, False)
        return carry

    lax.fori_loop(0, nbig // 2, pair, 0)

    @pl.when(nbig % 2 == 1)
    def _():
        sb_ref[...] = scores(nbig)
        consume(nbig - 1, sa_ref, False)
        consume(nbig, sb_ref, True)

    @pl.when(nbig % 2 == 0)
    def _():
        consume(nbig, sa_ref, True)

    o0 = acc_ref[0:tq, :LANE] / acc_ref[0:tq, LANE:]
    o1 = acc_ref[tq:2 * tq, :LANE] / acc_ref[tq:2 * tq, LANE:]
    o_ref[...] = _da_post(o0, o1, lam, sub_ref[...])


def _da_prompt(zb, sc, subln, tq=512, tkb=1024):
    s = zb.shape[0]
    tq = min(tq, s)
    tkb = min(tkb, s)
    kb, vb = C_DK // LANE, C_DV // LANE
    return pl.pallas_call(
        functools.partial(_da_prompt_kernel, tq=tq, tkb=tkb),
        grid=(DA_HEADS, s // tq),
        in_specs=[_smem_spec(),
                  pl.BlockSpec((tq, LANE), lambda h, i: (i, h)),
                  pl.BlockSpec((s, LANE), lambda h, i: (0, kb + h), pipeline_mode=pl.Buffered(1)),
                  pl.BlockSpec((s, LANE), lambda h, i: (0, vb + h), pipeline_mode=pl.Buffered(1)),
                  pl.BlockSpec((1, LANE), lambda h, i: (0, 0))],
        out_specs=pl.BlockSpec((tq, LANE), lambda h, i: (i, h)),
        out_shape=jax.ShapeDtypeStruct((s, DA_HEADS * LANE), F32),
        scratch_shapes=[pltpu.VMEM((s, 2 * LANE), BF16),
                        pltpu.VMEM((s, 2 * LANE), BF16),
                        pltpu.VMEM((2 * tq, tkb), F32), pltpu.VMEM((2 * tq, tkb), F32),
                        pltpu.VMEM((2 * tq, 1), F32), pltpu.VMEM((2 * tq, 2 * LANE), F32)],
        compiler_params=_cparams(("arbitrary", "arbitrary")),
        name="da_prompt",
    )(sc, zb, zb, zb, subln.reshape(1, LANE))


def _compress_kernel(pt_ref, *refs, pp, nch):
    del pt_ref
    pages = refs[:pp]
    tail_ref, wbig_ref, w1_ref, pe_ref, w2_ref, out_ref, ab_ref, rows_ref, x_ref = refs[pp:]
    s = pl.program_id(1)
    for k in range(pp):
        rows_ref[k] = pages[k][0].T
    for k in range(pp):
        for pos in range(CMP_STRIDE):
            x_ref[8 * k:8 * (k + 1), LANE * pos:LANE * (pos + 1)] = rows_ref[k, pl.ds(pos, 8, stride=CMP_STRIDE), :]
    x = x_ref[...].astype(BF16)
    rows = 8 * pp
    ab_ref[pl.ds(pl.multiple_of(s * rows, rows), rows), :] = jnp.dot(x, wbig_ref[...], preferred_element_type=F32)

    @pl.when(s == pl.num_programs(1) - 1)
    def _():
        ab_ref[nch:nch + 8, :] = jnp.dot(tail_ref[0].astype(BF16), wbig_ref[...], preferred_element_type=F32)
        hpe = jnp.dot(pe_ref[...].astype(BF16), w1_ref[...], preferred_element_type=F32)[0:1]
        outs = []
        hid = 2 * NSA_HD
        for g in range(NSA_GROUPS):
            a = ab_ref[0:nch, 2 * hid * g:2 * hid * g + hid]
            b = ab_ref[1:nch + 1, 2 * hid * g + hid:2 * hid * (g + 1)]
            hd = a + b + hpe
            act = hd * _sigmoid(hd)
            outs.append(jnp.dot(act.astype(BF16), w2_ref[...], preferred_element_type=F32))
        out_ref[0] = jnp.concatenate(outs, axis=-1)


def _compress(pool, page_table, tail, w1, pe, w2):
    b, n_pages = page_table.shape
    nch = n_pages * 8
    pp = _pages_per_step(n_pages, 32)
    ck = CMP_STRIDE * NSA_GROUPS * NSA_HD
    hid = 2 * NSA_HD
    w1r = w1.reshape(2, CMP_STRIDE, NSA_HD, hid)
    wbig = jnp.einsum('psdh,gk->sgdkph', w1r, jnp.eye(NSA_GROUPS, dtype=F32)).reshape(ck, 2 * NSA_GROUPS * hid).astype(BF16)
    pe8 = jnp.pad(pe.reshape(1, CMP_LEN * NSA_HD), ((0, 7), (0, 0)))

    def page_spec(k):
        return pl.BlockSpec((1, LANE, PAGE), lambda bi, s, pt: (pt[bi * n_pages + s * pp + k], 0, 0))

    const = lambda shape: pl.BlockSpec(shape, lambda bi, s, pt: tuple(0 for _ in shape))
    grid_spec = pltpu.PrefetchScalarGridSpec(
        num_scalar_prefetch=1,
        grid=(b, n_pages // pp),
        in_specs=[page_spec(k) for k in range(pp)] + [
            pl.BlockSpec((1, 8, ck), lambda bi, s, pt: (bi, 0, 0)),
            const((ck, 2 * NSA_GROUPS * hid)), const((CMP_LEN * NSA_HD, hid)), const((8, CMP_LEN * NSA_HD)),
            const((hid, NSA_HD))],
        out_specs=pl.BlockSpec((1, nch, NSA_GROUPS * NSA_HD), lambda bi, s, pt: (bi, 0, 0)),
        scratch_shapes=[pltpu.VMEM((nch + 8, 2 * NSA_GROUPS * hid), F32), pltpu.VMEM((pp, PAGE, LANE), F32),
                        pltpu.VMEM((8 * pp, ck), F32)],
    )
    return pl.pallas_call(
        functools.partial(_compress_kernel, pp=pp, nch=nch),
        grid_spec=grid_spec,
        out_shape=jax.ShapeDtypeStruct((b, nch, NSA_GROUPS * NSA_HD), F32),
        compiler_params=_cparams(("arbitrary", "arbitrary")),
        name="compress",
    )(page_table.reshape(-1), *([pool] * pp), tail, wbig, w1.astype(BF16), pe8, w2.astype(BF16))


def _stack_group_queries(q, g, tq):
    lane = lax.broadcasted_iota(jnp.int32, (tq, 2 * NSA_HD), 1)
    mine = jnp.where(lane >= NSA_HD, 1, 0) == g
    parts = []
    for hh in range(NSA_HPG):
        qh = q[:, NSA_HD * hh:NSA_HD * (hh + 1)]
        parts.append(jnp.where(mine, jnp.concatenate([qh, qh], axis=1), jnp.zeros((tq, 2 * NSA_HD), q.dtype)))
    return jnp.concatenate(parts, axis=0)


def _group_half(x, g):
    return jnp.where(g == 0, x[:, :NSA_HD], x[:, NSA_HD:])


def _cmp_topk_kernel(sl_ref, q_ref, kc_ref, vc_ref, agg_ref, oc_ref, sel_ref, any_ref, *, tq, nb, nselp, qpos_base, topk,
                     levels):
    g = pl.program_id(1)
    t = pl.program_id(2)
    q0 = qpos_base + t * tq
    qpad = _stack_group_queries(q_ref[...], g, tq)

    def work(nbw, nsw):
        kcb = kc_ref[0, :nbw, :].astype(BF16)
        vcb = vc_ref[0, :nbw, :].astype(BF16)
        s_all = lax.dot_general(qpad, kcb, NT_DIMS, preferred_element_type=F32)
        i = lax.broadcasted_iota(jnp.int32, (tq, nbw), 0)
        n = lax.broadcasted_iota(jnp.int32, (tq, nbw), 1)
        dist = (q0 + i - (CMP_STRIDE * n + (CMP_LEN - 1))).astype(F32)
        mask = dist >= 0
        row_ok = dist[:, 0:1] >= 0.0
        psum = jnp.zeros((tq, nbw), F32)
        for hh in range(NSA_HPG):
            slope = sl_ref[NSA_HPG * g + hh]
            tt = jnp.where(mask, s_all[hh * tq:(hh + 1) * tq] - slope * dist, NEG)
            m = jnp.max(tt, axis=-1, keepdims=True)
            e = jnp.exp2(tt - m)
            l = jnp.sum(e, axis=-1, keepdims=True)
            p = e * jnp.where(row_ok, 1.0 / l, 0.0)
            psum = psum + p
            o = jnp.dot(p.astype(BF16), vcb, preferred_element_type=F32)
            oc_ref[:, NSA_HD * hh:NSA_HD * (hh + 1)] = _group_half(o, g)

        p_hi = psum.astype(BF16)
        p_lo = (psum - p_hi.astype(F32)).astype(BF16)
        agg = agg_ref[:nbw, :nsw]
        imp = jnp.dot(p_hi, agg, preferred_element_type=F32) + jnp.dot(p_lo, agg, preferred_element_type=F32)
        jj = lax.broadcasted_iota(jnp.int32, (tq, nsw), 1)
        cur = (q0 + lax.broadcasted_iota(jnp.int32, (tq, nsw), 0)) // SEL_BLOCK
        valid = jj <= cur
        forced = jnp.where(valid, jnp.where(jj == 0, 1, jnp.where(jj >= cur - 1, 1, 0)), 0)
        score = jnp.where(valid, imp + jnp.where(forced == 1, FORCE_BONUS, 0.0), NEG)

        tqp = -(-tq // LANE) * LANE
        if tqp > tq:
            score = jnp.concatenate([score, jnp.full((tqp - tq, nsw), NEG, F32)], axis=0)
        cand = lax.broadcasted_iota(jnp.int32, (nsw, tqp), 0)

        def pick(_, carry):
            sc, chosen = carry
            mx = jnp.max(sc, axis=0, keepdims=True)
            idx = jnp.min(jnp.where(sc == mx, cand, nsw), axis=0, keepdims=True)
            hit = cand == idx
            return jnp.where(hit, PICKED, sc), jnp.where(hit, 1.0, chosen)

        _, chosen = lax.fori_loop(0, topk, pick, (score.T, jnp.zeros((nsw, tqp), F32)))
        sel = jnp.where(valid, chosen.T[:tq], 0.0)
        if nsw < nselp:
            sel = jnp.concatenate([sel, jnp.zeros((tq, nselp - nsw), F32)], axis=1)
        sel_ref[0, 0] = sel
        any_ref[0, 0, 0] = jnp.broadcast_to(jnp.max(sel, axis=0, keepdims=True), (8, nselp))

    if levels == 1:
        work(nb, nselp)
    else:
        unit = nselp // levels
        need = (q0 + tq - 1) // SEL_BLOCK + 1
        lvl = (need + unit - 1) // unit
        for lv in range(1, levels + 1):
            @pl.when(lvl == lv)
            def _():
                work(nb * lv // levels, unit * lv)


def _sel_agg_matrix(nblk, nsel, nb, nselp):
    m = np.zeros((nb, nselp), np.float32)
    j = np.arange(nsel)
    r, c = SEL_BLOCK // CMP_STRIDE, CMP_LEN // CMP_STRIDE
    for a in range(r):
        for b in range(c):
            i = r * j + a - b
            ok = (i >= 0) & (i < nblk)
            np.add.at(m, (i[ok], j[ok]), 1.0)
    return jnp.asarray(m, dtype=BF16)


def _cmp_topk(qarr, qcol0, kcb, vcb, slopes, *, batch, sq, tq, nblk, nsel, qpos_base):
    nb = kcb.shape[1]
    nselp = -(-nsel // LANE) * LANE
    nt = sq // tq
    agg = _sel_agg_matrix(nblk, nsel, nb, nselp)
    gw = NSA_HPG * NSA_HD
    ratio = SEL_BLOCK // CMP_STRIDE
    levels = 4 if (qpos_base == 0 and nb == ratio * nselp and nb % (4 * LANE) == 0 and sq >= SEL_BLOCK * nselp) else 1
    return pl.pallas_call(
        functools.partial(_cmp_topk_kernel, tq=tq, nb=nb, nselp=nselp, qpos_base=qpos_base, topk=min(SEL_TOP, nsel),
                          levels=levels),
        grid=(batch, NSA_GROUPS, nt),
        in_specs=[_smem_spec(),
                  pl.BlockSpec((tq, gw), lambda b, g, t: (b * nt + t, qcol0 + g)),
                  pl.BlockSpec((1, nb, LANE), lambda b, g, t: (b, 0, 0)),
                  pl.BlockSpec((1, nb, LANE), lambda b, g, t: (b, 0, 0)),
                  pl.BlockSpec((nb, nselp), lambda b, g, t: (0, 0))],
        out_specs=[pl.BlockSpec((tq, gw), lambda b, g, t: (b * nt + t, g)),
                   pl.BlockSpec((1, 1, tq, nselp), lambda b, g, t: (b, g, t, 0)),
                   pl.BlockSpec((1, 1, 1, 8, nselp), lambda b, g, t: (b, g, t, 0, 0))],
        out_shape=[jax.ShapeDtypeStruct((batch * sq, NSA_GROUPS * gw), F32),
                   jax.ShapeDtypeStruct((batch, NSA_GROUPS, sq, nselp), F32),
                   jax.ShapeDtypeStruct((batch, NSA_GROUPS, nt, 8, nselp), F32)],
        compiler_params=_cparams(("arbitrary", "arbitrary", "arbitrary")),
        name="cmp_topk",
    )(slopes, qarr, kcb, vcb, agg)


def _nsa_sw_kernel(fl_ref, sl_ref, q_ref, ks_ref, vs_ref, kw_ref, vw_ref, sel_ref, oc_ref, ng_ref, out_ref,
                   ids_ref, m_ref, acc_ref, *, tq, nt, nselp, nwords, sb):
    g = pl.program_id(0)
    t = pl.program_id(1)
    rows = NSA_HPG * tq
    qpad = _stack_group_queries(q_ref[...], g, tq)
    slope_row = jnp.concatenate([jnp.full((tq, 1), sl_ref[NSA_HPG * g + hh], F32) for hh in range(NSA_HPG)], axis=0)
    i_loc = lax.broadcasted_iota(jnp.int32, (rows, LANE), 0) & (tq - 1)
    j_loc = lax.broadcasted_iota(jnp.int32, (rows, LANE), 1)
    dloc = (i_loc - j_loc).astype(F32)
    sl_dloc = slope_row * dloc
    selb = sel_ref[0, 0].astype(BF16)
    ones = jnp.ones((LANE, LANE), BF16)

    def chunk(ref, c):
        return ref[pl.ds(pl.multiple_of(jnp.maximum(c, 0) * LANE, LANE), LANE), :]

    def with_ones(v):
        return jnp.concatenate([v, jnp.concatenate([ones] * (v.shape[0] // LANE), axis=0)], axis=1)

    def sel_mask(c):
        jrow = lax.broadcasted_iota(jnp.int32, (nselp, LANE), 0)
        r = lax.broadcasted_iota(jnp.int32, (nselp, LANE), 1)
        expand = jnp.where(jrow == 2 * c + r // SEL_BLOCK, 1.0, 0.0).astype(BF16)
        return jnp.dot(selb, expand, preferred_element_type=F32)

    def per_head(mk, x):
        x3 = x.reshape(NSA_HPG, tq, LANE)
        return jnp.where(mk[None] > 0.5, x3, NEG).reshape(rows, LANE)

    s = lax.dot_general(qpad, chunk(ks_ref, t), NT_DIMS, preferred_element_type=F32)
    mk = jnp.where(dloc[:tq] >= 0.0, sel_mask(t), 0.0)
    tt = per_head(mk, s - sl_dloc)
    m0 = jnp.max(tt, axis=-1, keepdims=True)
    p = jnp.exp2(tt - m0)
    m_ref[...] = m0
    acc_ref[...] = jnp.dot(p.astype(BF16), with_ones(chunk(vs_ref, t)), preferred_element_type=F32)

    def scan(c, cnt):
        word = fl_ref[(g * nt + t) * nwords + c // 32]
        bit = lax.shift_right_logical(word, c % 32) & 1

        @pl.when(bit == 1)
        def _():
            ids_ref[cnt] = c

        return cnt + bit

    cnt = lax.fori_loop(0, t, scan, 0)
    for k in range(sb):
        ids_ref[cnt + k] = -1

    def sel_step(si, carry):
        cs = [ids_ref[si * sb + k] for k in range(sb)]
        kk = jnp.concatenate([chunk(ks_ref, c) for c in cs], axis=0)
        vv = with_ones(jnp.concatenate([chunk(vs_ref, c) for c in cs], axis=0))
        keep = [sel_mask(c) > 0.5 for c in cs]
        offs = [((t - c) * tq).astype(F32) for c in cs]
        half = rows // 2
        for r0 in (0, half):
            rs = slice(r0, r0 + half)
            s = lax.dot_general(qpad[rs], kk, NT_DIMS, preferred_element_type=F32)
            slabs = []
            for k in range(sb):
                x = s[:, k * LANE:(k + 1) * LANE] - (sl_dloc[rs] + slope_row[rs] * offs[k])
                x = jnp.where(keep[k][None], x.reshape(NSA_HPG // 2, tq, LANE), NEG)
                slabs.append(x.reshape(half, LANE))
            tt = jnp.concatenate(slabs, axis=1)
            m_old = m_ref[rs, :]
            m_new = jnp.maximum(m_old, jnp.max(tt, axis=-1, keepdims=True))
            alpha = jnp.exp2(m_old - m_new)
            p = jnp.exp2(tt - m_new)
            acc_ref[rs, :] = alpha * acc_ref[rs, :] + jnp.dot(p.astype(BF16), vv, preferred_element_type=F32)
            m_ref[rs, :] = m_new
        return carry

    nsteps = (cnt + sb - 1) // sb

    def step_pair(j, carry):
        sel_step(2 * j, carry)
        return sel_step(2 * j + 1, carry)

    lax.fori_loop(0, nsteps // 2, step_pair, 0)

    @pl.when(nsteps % 2 == 1)
    def _():
        sel_step(nsteps - 1, 0)
    o_s = acc_ref[:, :LANE] / acc_ref[:, LANE:]

    nback = WINDOW // tq
    c0 = jnp.maximum(t - nback, 0)
    wlen = (nback + 1) * LANE
    wstart = pl.multiple_of(c0 * LANE, LANE)
    s = lax.dot_general(qpad, kw_ref[pl.ds(wstart, wlen), :], NT_DIMS, preferred_element_type=F32)
    slabs = []
    for k in range(nback + 1):
        dist = dloc + ((t - (c0 + k)) * tq).astype(F32)
        x = s[:, k * LANE:(k + 1) * LANE] - slope_row * dist
        slabs.append(jnp.where(dist >= 0.0, jnp.where(dist < float(WINDOW), x, NEG), NEG))
    tt = jnp.concatenate(slabs, axis=1)
    p = jnp.exp2(tt - jnp.max(tt, axis=-1, keepdims=True))
    aw = jnp.dot(p.astype(BF16), with_ones(vw_ref[pl.ds(wstart, wlen), :]), preferred_element_type=F32)
    o_w = aw[:, :LANE] / aw[:, LANE:]

    gate = _sigmoid(ng_ref[...])
    glane = lax.broadcasted_iota(jnp.int32, (tq, LANE), 1)

    def gate_col(idx):
        return jnp.sum(jnp.where(glane == idx, gate, 0.0), axis=-1, keepdims=True)

    for hh in range(NSA_HPG):
        base = 3 * (NSA_HPG * g + hh)
        r0, r1 = hh * tq, (hh + 1) * tq
        o = (gate_col(base) * oc_ref[:, NSA_HD * hh:NSA_HD * (hh + 1)]
             + gate_col(base + 1) * _group_half(o_s[r0:r1], g)
             + gate_col(base + 2) * _group_half(o_w[r0:r1], g))
        out_ref[:, NSA_HD * hh:NSA_HD * (hh + 1)] = o


def _pack_chunk_flags(anyblk):
    g, nt, nselp = anyblk.shape
    chunk = jnp.max(anyblk.reshape(g, nt, nselp // 2, 2), axis=-1) > 0.5
    nchunk = nselp // 2
    nwords = -(-nchunk // 32)
    chunk = jnp.pad(chunk, ((0, 0), (0, 0), (0, nwords * 32 - nchunk)))
    bits = chunk.reshape(g, nt, nwords, 32).astype(jnp.uint32) << jnp.arange(32, dtype=jnp.uint32)
    words = jnp.sum(bits, axis=-1, dtype=jnp.uint32)
    return lax.bitcast_convert_type(words, jnp.int32).reshape(-1), nwords


def _nsa_sw_prompt(z, zb, sel, anyblk, oc, slopes, tq=128, sb=4):
    s = zb.shape[0]
    assert tq == LANE and s >= WINDOW + tq
    nt = s // tq
    nselp = sel.shape[-1]
    flags, nwords = _pack_chunk_flags(anyblk[0, :, :, 0, :])
    gw = NSA_HPG * NSA_HD
    rows = NSA_HPG * tq
    res = lambda col: pl.BlockSpec((s, LANE), lambda g, t, fl: (0, col // LANE), pipeline_mode=pl.Buffered(1))
    grid_spec = pltpu.PrefetchScalarGridSpec(
        num_scalar_prefetch=1,
        grid=(NSA_GROUPS, nt),
        in_specs=[_smem_spec(),
                  pl.BlockSpec((tq, gw), lambda g, t, fl: (t, C_NQ // gw + g)),
                  res(C_KS), res(C_VS), res(C_KW), res(C_VW),
                  pl.BlockSpec((1, 1, tq, nselp), lambda g, t, fl: (0, g, t, 0)),
                  pl.BlockSpec((tq, gw), lambda g, t, fl: (t, g)),
                  pl.BlockSpec((tq, LANE), lambda g, t, fl: (t, C_NG // LANE))],
        out_specs=pl.BlockSpec((tq, gw), lambda g, t, fl: (t, g)),
        scratch_shapes=[pltpu.SMEM((nt + sb,), jnp.int32), pltpu.VMEM((rows, 1), F32),
                        pltpu.VMEM((rows, 2 * LANE), F32)],
    )
    return pl.pallas_call(
        functools.partial(_nsa_sw_kernel, tq=tq, nt=nt, nselp=nselp, nwords=nwords, sb=sb),
        grid_spec=grid_spec,
        out_shape=jax.ShapeDtypeStruct((s, NSA_GROUPS * gw), F32),
        compiler_params=_cparams(("arbitrary", "arbitrary")),
        name="nsa_sel_win",
    )(flags, slopes, zb, zb, zb, zb, zb, sel, oc, z)


def _decode_kernel(pt_ref, pos_ref, cnt_ref, *refs, pp, n_pages, ncols, hk, qpos0, win, n_new, has_sel, nselp,
                   feature_major):
    del pt_ref
    kpages = refs[:pp]
    vpages = refs[pp:2 * pp]
    rest = refs[2 * pp:]
    if has_sel:
        wq_ref, ci_ref, kn_ref, vn_ref, sel_ref, o_ref, m_ref, s_ref, acc_ref = rest
    else:
        wq_ref, ci_ref, kn_ref, vn_ref, o_ref, m_ref, s_ref, acc_ref = rest
        sel_ref = None
    bi = pl.program_id(0)
    st = pl.program_id(1)
    windowed = win < 1e8
    guarded = has_sel or windowed

    @pl.when(st == 0)
    def _():
        m_ref[...] = jnp.full(m_ref.shape, NEG, F32)
        s_ref[...] = jnp.zeros(s_ref.shape, F32)
        acc_ref[...] = jnp.zeros(acc_ref.shape, F32)

    wq = wq_ref[0]
    slope = ci_ref[:, 0:1]
    qrel = ci_ref[:, 1:2]
    colhead = ci_ref[:, 2:3]

    def geometry(nrows):
        r = lax.broadcasted_iota(jnp.int32, (ncols, nrows), 1)
        kidx = (r // hk).astype(F32)
        base = slope * kidx
        if hk > 1:
            base = jnp.where((r % hk).astype(F32) == colhead, base, NEG)
        return r, kidx, base

    def scores(k, kpos0, n_valid, geom, check_range, feature_major):
        r, kidx, base = geom
        if feature_major:
            s = jnp.dot(wq, k.astype(BF16), preferred_element_type=F32)
        else:
            s = lax.dot_general(wq, k.astype(BF16), NT_DIMS, preferred_element_type=F32)
        off = qrel + (qpos0 - kpos0).astype(F32)
        t = (s + base) - slope * off
        if not (check_range or has_sel or n_valid is not None):
            return t, None
        okf = jnp.ones(t.shape, F32)
        if check_range:
            dist = off - kidx
            okf = jnp.where(dist >= 0.0, jnp.where(dist < win, 1.0, 0.0), 0.0)
        if n_valid is not None:
            okf = jnp.where(r < n_valid, okf, 0.0)
        if has_sel:
            nrows = t.shape[1]
            jrow = lax.broadcasted_iota(jnp.int32, (nselp, nrows), 0)
            kp = kpos0 + lax.broadcasted_iota(jnp.int32, (nselp, nrows), 1)
            expand = jnp.where(jrow == kp // SEL_BLOCK, 1.0, 0.0).astype(BF16)
            okf = okf * jnp.dot(sel_ref[0].astype(BF16), expand, preferred_element_type=F32)
        ok = okf > 0.5
        return jnp.where(ok, t, NEG), ok

    def update(ts, oks, vs, feature_major):
        m_old = m_ref[...]
        m_new = m_old
        for t in ts:
            m_new = jnp.maximum(m_new, jnp.max(t, axis=-1, keepdims=True))
        alpha = jnp.exp2(m_old - m_new)
        l = alpha * s_ref[...]
        acc = alpha * acc_ref[...]
        for t, ok, v in zip(ts, oks, vs):
            p = jnp.exp2(t - m_new)
            if ok is not None:
                p = jnp.where(ok, p, 0.0)
            l = l + jnp.sum(p, axis=-1, keepdims=True)
            if feature_major:
                acc = acc + lax.dot_general(p.astype(BF16), v.astype(BF16), NT_DIMS, preferred_element_type=F32)
            else:
                acc = acc + jnp.dot(p.astype(BF16), v.astype(BF16), preferred_element_type=F32)
        s_ref[...] = l
        acc_ref[...] = acc
        m_ref[...] = m_new

    cnt = cnt_ref[bi]

    @pl.when(st * pp < cnt)
    def _():
        geom = geometry(PAGE * hk)
        ts, oks, vs = [], [], []
        for k in range(pp):
            slot = st * pp + k
            n_valid = jnp.where(slot < cnt, PAGE * hk, 0) if guarded else None
            t, ok = scores(kpages[k][0], pos_ref[bi * n_pages + slot], n_valid, geom, windowed, feature_major)
            ts.append(t)
            oks.append(ok)
            vs.append(vpages[k][0])
        update(ts, oks, vs, feature_major)

    @pl.when(st == pl.num_programs(1) - 1)
    def _():
        t, ok = scores(kn_ref[0], jnp.int32(qpos0), n_new * hk, geometry(PAGE), True, False)
        update([t], [ok], [vn_ref[0]], False)
        l = s_ref[...]
        o_ref[0] = acc_ref[...] / jnp.where(l == 0.0, 1.0, l)


def _decode_attn(wq, colinfo, pool_k, pool_v, page_ids, page_pos, page_cnt, knew, vnew, sel, *, hk, qpos0, win, n_new, pp,
                 feature_major=False):
    b, n_pages = page_ids.shape
    ncols, width = wq.shape[1], wq.shape[2]
    pp = _pages_per_step(n_pages, pp)
    has_sel = sel is not None
    nselp = sel.shape[-1] if has_sel else 0

    def page_spec(k):
        return pl.BlockSpec((1, width, PAGE) if feature_major else (1, PAGE * hk, width),
                            lambda bi, s, pt, pos, cnt: (pt[bi * n_pages + s * pp + k], 0, 0))

    per_batch = lambda shape: pl.BlockSpec((1,) + shape, lambda bi, s, pt, pos, cnt: (bi, 0, 0))
    in_specs = ([page_spec(k) for k in range(pp)] * 2
                + [per_batch((ncols, width)), pl.BlockSpec((ncols, LANE), lambda bi, s, pt, pos, cnt: (0, 0)),
                   per_batch((PAGE, width)), per_batch((PAGE, width))])
    args = [pool_k] * pp + [pool_v] * pp + [wq, colinfo, knew, vnew]
    if has_sel:
        in_specs.append(per_batch((ncols, nselp)))
        args.append(sel)
    grid_spec = pltpu.PrefetchScalarGridSpec(
        num_scalar_prefetch=3,
        grid=(b, n_pages // pp),
        in_specs=in_specs,
        out_specs=per_batch((ncols, width)),
        scratch_shapes=[pltpu.VMEM((ncols, 1), F32), pltpu.VMEM((ncols, 1), F32), pltpu.VMEM((ncols, width), F32)],
    )
    return pl.pallas_call(
        functools.partial(_decode_kernel, pp=pp, n_pages=n_pages, ncols=ncols, hk=hk, qpos0=qpos0, win=float(win),
                          n_new=n_new, has_sel=has_sel, nselp=nselp, feature_major=feature_major),
        grid_spec=grid_spec,
        out_shape=jax.ShapeDtypeStruct((b, ncols, width), F32),
        compiler_params=_cparams(("arbitrary", "arbitrary")),
        name="decode_attn",
    )(page_ids.reshape(-1), page_pos.reshape(-1), page_cnt, *args)


def _all_pages(page_table, pos0):
    b, n_pages = page_table.shape
    pos = jnp.broadcast_to(pos0 + PAGE * jnp.arange(n_pages, dtype=jnp.int32), (b, n_pages))
    return page_table, pos, jnp.full((b,), n_pages, jnp.int32)


def _selected_pages(page_table, sel_cols):
    b, n_pages = page_table.shape
    per_page = PAGE // SEL_BLOCK
    hit = jnp.max(sel_cols[:, :, :per_page * n_pages].reshape(b, -1, n_pages, per_page), axis=(1, 3)) > 0.5
    cnt = jnp.sum(hit, axis=1).astype(jnp.int32)
    order = jnp.argsort(jnp.logical_not(hit), axis=1, stable=True).astype(jnp.int32)
    keep = jnp.minimum(jnp.arange(n_pages, dtype=jnp.int32)[None], jnp.maximum(cnt - 1, 0)[:, None])
    order = jnp.take_along_axis(order, keep, axis=1)
    return jnp.take_along_axis(page_table, order, axis=1), order * PAGE, cnt


def _da_post_kernel(sc_ref, o0_ref, o1_ref, sub_ref, o_ref):
    o_ref[...] = _da_post(o0_ref[...], o1_ref[...], sc_ref[0], sub_ref[...])


def _da_post_call(o0, o1, sc, subln):
    return pl.pallas_call(
        _da_post_kernel,
        in_specs=[_smem_spec(), pl.BlockSpec(o0.shape, lambda: (0, 0)), pl.BlockSpec(o0.shape, lambda: (0, 0)),
                  pl.BlockSpec((1, LANE), lambda: (0, 0))],
        out_specs=pl.BlockSpec(o0.shape, lambda: (0, 0)),
        out_shape=jax.ShapeDtypeStruct(o0.shape, F32),
        name="da_post",
    )(sc, o0, o1, subln.reshape(1, LANE))


def _gate3_kernel(ng_ref, e_ref, oc_ref, os_ref, ow_ref, o_ref):
    gate = _sigmoid(ng_ref[...])
    acc = jnp.zeros(o_ref.shape, F32)
    for br, ref in enumerate((oc_ref, os_ref, ow_ref)):
        ge = jnp.dot(gate, e_ref[br], preferred_element_type=F32, precision=lax.Precision.HIGHEST)
        acc = acc + ge * ref[...]
    o_ref[...] = acc


def _gate3(ng, oc, os_, ow):
    e = np.zeros((3, LANE, NSA_HEADS * NSA_HD), np.float32)
    for br in range(3):
        for h in range(NSA_HEADS):
            e[br, 3 * h + br, NSA_HD * h:NSA_HD * (h + 1)] = 1.0
    return pl.pallas_call(
        _gate3_kernel,
        out_shape=jax.ShapeDtypeStruct(oc.shape, F32),
        name="gate3",
    )(ng, jnp.asarray(e), oc, os_, ow)


def _tail_a_kernel(x_ref, da_ref, nsa_ref, ga_ref, gb_ref, wo_ref, nx_ref, wxq_ref, h_ref, q_ref):
    m = _sigmoid(ga_ref[...]) * da_ref[...] + _sigmoid(gb_ref[...]) * nsa_ref[...]
    h = x_ref[...] + jnp.dot(m.astype(BF16), wo_ref[...], preferred_element_type=F32)
    h_ref[...] = h
    xn = _rms(h, nx_ref[...]).astype(BF16)
    q_ref[...] = jnp.dot(xn, wxq_ref[...], preferred_element_type=F32)


def _tail_a(x, o_da, o_nsa, z, w_o, norm_x, w_xq):
    m = x.shape[0]
    tm = min(m, 512)
    row = lambda cb: pl.BlockSpec((tm, D_MODEL), lambda i: (i, cb))
    const = lambda shape: pl.BlockSpec(shape, lambda i: (0, 0), pipeline_mode=pl.Buffered(1))
    return pl.pallas_call(
        _tail_a_kernel,
        grid=(m // tm,),
        in_specs=[row(0), row(0), row(0), row(C_GA // D_MODEL), row(C_GB // D_MODEL),
                  const((D_MODEL, D_MODEL)), const((1, D_MODEL)), const((D_MODEL, X_W))],
        out_specs=[row(0), pl.BlockSpec((tm, X_W), lambda i: (i, 0))],
        out_shape=[jax.ShapeDtypeStruct((m, D_MODEL), F32), jax.ShapeDtypeStruct((m, X_W), F32)],
        compiler_params=_cparams(("parallel",)),
        name="tail_merge_wo",
    )(x, o_da, o_nsa, z, z, w_o.astype(BF16), norm_x.reshape(1, D_MODEL), w_xq.astype(BF16))


def _cross_kernel(q_ref, mk_ref, mv_ref, o_ref):
    q = (q_ref[0] * (X_HD ** -0.5)).astype(BF16)
    mk = mk_ref[0].astype(BF16)
    mv = mv_ref[0].astype(BF16)
    outs = []
    for h in range(X_HEADS):
        sl = slice(X_HD * h, X_HD * (h + 1))
        s = lax.dot_general(q[:, sl], mk[:, sl], NT_DIMS, preferred_element_type=F32)
        e = jnp.exp(s - jnp.max(s, axis=-1, keepdims=True))
        p = e / jnp.sum(e, axis=-1, keepdims=True)
        outs.append(jnp.dot(p.astype(BF16), mv[:, sl], preferred_element_type=F32))
    o_ref[0] = jnp.concatenate(outs, axis=-1)


def _cross(q, mk, mv):
    b, t, _ = q.shape
    mlen = mk.shape[1]
    tt = min(t, 512)
    return pl.pallas_call(
        _cross_kernel,
        grid=(b, t // tt),
        in_specs=[pl.BlockSpec((1, tt, X_W), lambda bi, i: (bi, i, 0)),
                  pl.BlockSpec((1, mlen, X_W), lambda bi, i: (bi, 0, 0)),
                  pl.BlockSpec((1, mlen, X_W), lambda bi, i: (bi, 0, 0))],
        out_specs=pl.BlockSpec((1, tt, X_W), lambda bi, i: (bi, i, 0)),
        out_shape=jax.ShapeDtypeStruct((b, t, X_W), F32),
        compiler_params=_cparams(("parallel", "parallel")),
        name="cross_attn",
    )(q, mk, mv)


def _tail_c_kernel(h_ref, ox_ref, wxo_ref, nf_ref, wg_ref, wu_ref, wd_ref, nfin_ref, y_ref, *, nchunk, chunk):
    h = h_ref[...] + jnp.dot(ox_ref[...].astype(BF16), wxo_ref[...], preferred_element_type=F32)
    xn = _rms(h, nf_ref[...]).astype(BF16)
    acc = jnp.zeros(h.shape, F32)
    for c in range(nchunk):
        sl = slice(c * chunk, (c + 1) * chunk)
        gt = jnp.dot(xn, wg_ref[:, sl], preferred_element_type=F32)
        up = jnp.dot(xn, wu_ref[:, sl], preferred_element_type=F32)
        act = (gt * _sigmoid(gt) * up).astype(BF16)
        acc = acc + jnp.dot(act, wd_ref[sl, :], preferred_element_type=F32)
    y_ref[...] = _rms(h + acc, nfin_ref[...])


def _tail_c(h, ox, w_xo, norm_ffn, w_gate_up, w_down, norm_final):
    m = h.shape[0]
    tm = min(m, 512)
    hid = w_down.shape[0]
    chunk = hid // 2
    const = lambda shape: pl.BlockSpec(shape, lambda i: (0, 0), pipeline_mode=pl.Buffered(1))
    return pl.pallas_call(
        functools.partial(_tail_c_kernel, nchunk=2, chunk=chunk),
        grid=(m // tm,),
        in_specs=[pl.BlockSpec((tm, D_MODEL), lambda i: (i, 0)), pl.BlockSpec((tm, X_W), lambda i: (i, 0)),
                  const((X_W, D_MODEL)), const((1, D_MODEL)), const((D_MODEL, hid)), const((D_MODEL, hid)),
                  const((hid, D_MODEL)), const((1, D_MODEL))],
        out_specs=pl.BlockSpec((tm, D_MODEL), lambda i: (i, 0)),
        out_shape=jax.ShapeDtypeStruct((m, D_MODEL), F32),
        compiler_params=_cparams(("parallel",)),
        name="tail_ffn",
    )(h, ox, w_xo.astype(BF16), norm_ffn.reshape(1, D_MODEL), w_gate_up[:, :hid].astype(BF16),
      w_gate_up[:, hid:].astype(BF16), w_down.astype(BF16), norm_final.reshape(1, D_MODEL))


def _alibi(n):
    return np.asarray(2.0 ** (-8.0 * np.arange(1, n + 1) / n) * LOG2E, dtype=np.float32)


def _finish(x, o_da, o_nsa, z, mk, mv, batch, w_o, norm_x, w_xq, w_xo, norm_ffn, w_gate_up, w_down, norm_final):
    m = x.shape[0]
    t = m // batch
    h1, qx = _tail_a(x, o_da, o_nsa, z, w_o, norm_x, w_xq)
    q3 = qx.reshape(batch, t, X_W)
    tpad = -(-t // 8) * 8
    if tpad != t:
        q3 = jnp.pad(q3, ((0, 0), (0, tpad - t), (0, 0)))
    ox = _cross(q3, mk, mv)[:, :t].reshape(m, X_W)
    return _tail_c(h1, ox, w_xo, norm_ffn, w_gate_up, w_down, norm_final)


def kernel(x_prompt, x_sample, cache_diff_k, cache_diff_v, cache_cmp_k, cache_cmp_v, cache_sel_k, cache_sel_v,
           cache_win_k, cache_win_v, cache_mem_k, cache_mem_v, page_table, mem_prompt,
           norm_mix, w_in, lam_q1, lam_k1, lam_q2, lam_k2, da_subln,
           w_cmp_k1, pe_cmp_k, w_cmp_k2, w_cmp_v1, pe_cmp_v, w_cmp_v2,
           w_o, norm_x, w_xq, w_mem_kv, w_xo, norm_ffn, w_gate_up, w_down, norm_final):
    batch, seq, _ = x_prompt.shape
    db, ds, _ = x_sample.shape
    assert batch == 1 and norm_mix.shape[0] == 1
    n_pages = page_table.shape[1]
    past = n_pages * PAGE
    wb = cache_win_k.shape[2]
    kvw = NSA_GROUPS * NSA_HD

    lam = (jnp.exp(jnp.sum(lam_q1[0] * lam_k1[0])) - jnp.exp(jnp.sum(lam_q2[0] * lam_k2[0])) + LAM_INIT).astype(F32)
    da_sc = jnp.concatenate([lam.reshape(1), jnp.asarray(_alibi(DA_HEADS))])
    nsa_sl = jnp.asarray(_alibi(NSA_HEADS))
    w_pad = _prep_w_in(w_in[0])
    tail_w = (w_o[0], norm_x[0], w_xq[0], w_xo[0], norm_ffn[0], w_gate_up[0], w_down[0], norm_final)
    cmp_k = (w_cmp_k1[0], pe_cmp_k[0], w_cmp_k2[0])
    cmp_v = (w_cmp_v1[0], pe_cmp_v[0], w_cmp_v2[0])

    xp = x_prompt.reshape(seq, D_MODEL)
    z, zb = _inproj(xp, norm_mix[0], w_pad)
    o_da = _da_prompt(zb, da_sc, da_subln[0])

    ident = jnp.arange(seq // PAGE, dtype=jnp.int32).reshape(1, -1)
    zero_tail = jnp.zeros((1, 8, CMP_STRIDE * kvw), F32)
    p_kc = z[:, C_KC:C_KC + kvw]
    p_vc = z[:, C_VC:C_VC + kvw]
    as_pages = lambda a: a.reshape(seq // PAGE, PAGE, kvw).transpose(0, 2, 1)
    kcb = _compress(as_pages(p_kc), ident, zero_tail, *cmp_k)
    vcb = _compress(as_pages(p_vc), ident, zero_tail, *cmp_v)
    nch = seq // CMP_STRIDE
    oc, sel, anyblk = _cmp_topk(zb, C_NQ // (NSA_HPG * NSA_HD), kcb, vcb, nsa_sl, batch=1, sq=seq, tq=128,
                                nblk=nch - 1, nsel=seq // SEL_BLOCK, qpos_base=0)
    o_nsa = _nsa_sw_prompt(z, zb, sel, anyblk, oc, nsa_sl)

    mem_kv = _matmul(mem_prompt.reshape(-1, D_MODEL), w_mem_kv[0].astype(BF16))
    p_mk, p_mv = mem_kv[:, :X_W], mem_kv[:, X_W:]
    y_prompt = _finish(xp, o_da, o_nsa, z, p_mk[None], p_mv[None], 1, *tail_w)

    r5 = lambda a, h: a.reshape(1, 1, a.shape[0], h, -1)
    p_states = (r5(z[:, C_DK:C_DK + 1024], DA_HEADS), r5(z[:, C_DV:C_DV + 1024], DA_HEADS),
                r5(p_kc, NSA_GROUPS), r5(p_vc, NSA_GROUPS),
                r5(z[:, C_KS:C_KS + kvw], NSA_GROUPS), r5(z[:, C_VS:C_VS + kvw], NSA_GROUPS),
                r5(z[seq - min(WINDOW, seq):, C_KW:C_KW + kvw], NSA_GROUPS),
                r5(z[seq - min(WINDOW, seq):, C_VW:C_VW + kvw], NSA_GROUPS),
                r5(p_mk, X_HEADS), r5(p_mv, X_HEADS))

    ms = db * ds
    xs = x_sample.reshape(ms, D_MODEL)
    zs, zsb = _inproj(xs, norm_mix[0], w_pad)
    z3 = zs.reshape(db, ds, ZP)
    pad_rows = lambda a, n: jnp.pad(a, ((0, 0), (0, n - a.shape[1]), (0, 0)))

    dq = z3[:, :, C_DQ:C_DQ + 1024].reshape(db, ds, DA_HEADS, 2, DA_HD)
    wq_da = jnp.einsum('bqhmd,mn->bhmqnd', dq, jnp.eye(2, dtype=F32))
    wq_da = wq_da.reshape(db, DA_HEADS * 2 * ds, 2 * DA_HD).astype(BF16)
    ci = np.zeros((DA_HEADS * 2 * ds, LANE), np.float32)
    ci[:, 0] = np.repeat(_alibi(DA_HEADS), 2 * ds)
    ci[:, 1] = np.tile(np.arange(ds), DA_HEADS * 2)
    ci[:, 2] = np.repeat(np.arange(DA_HEADS), 2 * ds)
    s_dk, s_dv = z3[:, :, C_DK:C_DK + 1024], z3[:, :, C_DV:C_DV + 1024]
    head_rows = lambda a: pad_rows(a.reshape(db, ds * DA_HEADS, 2 * DA_HD), PAGE)
    o_pair = _decode_attn(wq_da, jnp.asarray(ci), cache_diff_k[0].reshape(-1, PAGE * DA_HEADS, 2 * DA_HD),
                          cache_diff_v[0].reshape(-1, PAGE * DA_HEADS, 2 * DA_HD), *_all_pages(page_table, 0),
                          head_rows(s_dk), head_rows(s_dv), None, hk=DA_HEADS, qpos0=past, win=1e9, n_new=ds, pp=16)
    o_pair = o_pair.reshape(db, DA_HEADS, 2, ds, 2 * DA_HD).transpose(2, 0, 3, 1, 4)
    o_da_s = _da_post_call(o_pair[0].reshape(ms * DA_HEADS, LANE), o_pair[1].reshape(ms * DA_HEADS, LANE),
                           da_sc, da_subln[0]).reshape(ms, DA_HEADS * LANE)

    ck = CMP_STRIDE * kvw
    s_kc, s_vc = z3[:, :, C_KC:C_KC + kvw], z3[:, :, C_VC:C_VC + kvw]
    tail_of = lambda a: jnp.pad(a.reshape(db, 1, ds * kvw), ((0, 0), (0, 7), (0, ck - ds * kvw)))
    feat_major = lambda c: c[0].transpose(0, 2, 3, 1).reshape(-1, kvw, PAGE)
    kcb_s = _compress(feat_major(cache_cmp_k), page_table, tail_of(s_kc), *cmp_k)
    vcb_s = _compress(feat_major(cache_cmp_v), page_table, tail_of(s_vc), *cmp_v)
    tq_s = 16
    nq_pad = pad_rows(zsb.reshape(db, ds, ZP)[:, :, C_NQ:C_NQ + 1024], tq_s).reshape(db * tq_s, 1024)
    nsel_s = -(-(past + ds) // SEL_BLOCK)
    oc_s, sel_s, _ = _cmp_topk(nq_pad, 0, kcb_s, vcb_s, nsa_sl, batch=db, sq=tq_s, tq=tq_s,
                               nblk=(past + ds + CMP_STRIDE - 1) // CMP_STRIDE - 1, nsel=nsel_s, qpos_base=past)
    oc_s = oc_s.reshape(db, tq_s, 1024)[:, :ds].reshape(ms, 1024)

    nq = z3[:, :, C_NQ:C_NQ + 1024].reshape(db, ds, NSA_GROUPS, NSA_HPG, NSA_HD)
    wq_n = jnp.einsum('bqghd,gk->bghqkd', nq, jnp.eye(NSA_GROUPS, dtype=F32))
    wq_n = wq_n.reshape(db, NSA_HEADS * ds, kvw).astype(BF16)
    cn = np.zeros((NSA_HEADS * ds, LANE), np.float32)
    cn[:, 0] = np.repeat(_alibi(NSA_HEADS), ds)
    cn[:, 1] = np.tile(np.arange(ds), NSA_HEADS)
    cn = jnp.asarray(cn)
    sel_cols = jnp.repeat(sel_s[:, :, None, :ds, :], NSA_HPG, axis=2).reshape(db, NSA_HEADS * ds, -1)
    s_ks, s_vs = z3[:, :, C_KS:C_KS + kvw], z3[:, :, C_VS:C_VS + kvw]
    s_kw, s_vw = z3[:, :, C_KW:C_KW + kvw], z3[:, :, C_VW:C_VW + kvw]
    o_sel = _decode_attn(wq_n, cn, feat_major(cache_sel_k), feat_major(cache_sel_v),
                         *_selected_pages(page_table, sel_cols), pad_rows(s_ks, PAGE), pad_rows(s_vs, PAGE), sel_cols,
                         hk=1, qpos0=past, win=1e9, n_new=ds, pp=16, feature_major=True)
    win_pages = wb // PAGE
    win_pt = jnp.arange(db * win_pages, dtype=jnp.int32).reshape(db, win_pages)
    o_win = _decode_attn(wq_n, cn, cache_win_k[0].reshape(-1, PAGE, kvw), cache_win_v[0].reshape(-1, PAGE, kvw),
                         *_all_pages(win_pt, past - wb), pad_rows(s_kw, PAGE), pad_rows(s_vw, PAGE), None,
                         hk=1, qpos0=past, win=WINDOW, n_new=ds, pp=win_pages)

    def own_group(o):
        o = o.reshape(db, NSA_GROUPS, NSA_HPG, ds, NSA_GROUPS, NSA_HD)
        o = jnp.stack([o[:, g, :, :, g] for g in range(NSA_GROUPS)], axis=1)
        return o.transpose(0, 3, 1, 2, 4).reshape(ms, NSA_HEADS * NSA_HD)

    o_nsa_s = _gate3(zs[:, C_NG:C_NG + LANE], oc_s, own_group(o_sel), own_group(o_win))
    s_mk = cache_mem_k[0].reshape(db, -1, X_W)
    s_mv = cache_mem_v[0].reshape(db, -1, X_W)
    y_sample = _finish(xs, o_da_s, o_nsa_s, zs, s_mk, s_mv, db, *tail_w)

    s5 = lambda a, h: a.reshape(1, db, a.shape[1], h, -1)
    new_win = lambda c, a: jnp.concatenate([c[0].reshape(db, wb, kvw), a], axis=1)[:, ds:]
    s_states = (s5(s_dk, DA_HEADS), s5(s_dv, DA_HEADS), s5(s_kc, NSA_GROUPS), s5(s_vc, NSA_GROUPS),
                s5(s_ks, NSA_GROUPS), s5(s_vs, NSA_GROUPS),
                s5(new_win(cache_win_k, s_kw), NSA_GROUPS), s5(new_win(cache_win_v, s_vw), NSA_GROUPS))

    return (y_prompt.reshape(1, seq, D_MODEL), y_sample.reshape(db, ds, D_MODEL)) + p_states + s_states
```

```python
import functools

import numpy as np
import jax
import jax.numpy as jnp
from jax import lax
from jax.experimental import pallas as pl
from jax.experimental.pallas import tpu as pltpu

F32 = jnp.float32
BF16 = jnp.bfloat16

D_MODEL = 1024
DA_HEADS = 8
DA_HD = 64
NSA_HEADS = 16
NSA_GROUPS = 2
NSA_HPG = NSA_HEADS // NSA_GROUPS
NSA_HD = 64
CMP_LEN = 32
CMP_STRIDE = 16
SEL_BLOCK = 64
SEL_TOP = 16
WINDOW = 512
X_HEADS = 4
X_HD = 64
X_W = X_HEADS * X_HD
EPS = 1e-6
NEG = -1e30
PICKED = -3e38
FORCE_BONUS = 1e6
LAM_INIT = 0.2
LANE = 128
PAGE = 128
VMEM_LIMIT = 56 * 1024 * 1024

C_DQ, C_DK, C_DV, C_NQ, C_GA, C_GB = 0, 1024, 2048, 3072, 4096, 5120
C_KC, C_VC, C_KS, C_VS, C_KW, C_VW, C_NG = 6144, 6272, 6400, 6528, 6656, 6784, 6912
ZP = 7040
ZP_TILE = 1408

NT_DIMS = (((1,), (1,)), ((), ()))
LOG2E = 1.4426950408889634
Q_SCALE = DA_HD ** -0.5 * LOG2E


def _cparams(sem):
    return pltpu.CompilerParams(dimension_semantics=sem, vmem_limit_bytes=VMEM_LIMIT)


def _smem_spec():
    return pl.BlockSpec(memory_space=pltpu.SMEM)


def _pages_per_step(n_pages, cap):
    return max(p for p in range(1, cap + 1) if n_pages % p == 0)


def _sigmoid(x):
    return 1.0 / (1.0 + jnp.exp(-x))


def _rms(x, g):
    return x * lax.rsqrt(jnp.mean(x * x, axis=-1, keepdims=True) + EPS) * g


def _inproj_kernel(x_ref, g_ref, w_ref, o_ref, ob_ref, xn_ref):
    @pl.when(pl.program_id(1) == 0)
    def _():
        xn_ref[...] = _rms(x_ref[...], g_ref[...]).astype(BF16)

    acc = jnp.dot(xn_ref[...], w_ref[...], preferred_element_type=F32)
    o_ref[...] = acc
    ob_ref[...] = acc.astype(BF16)


def _inproj(x, g, w_pad):
    m = x.shape[0]
    tm = min(m, 1024)
    return pl.pallas_call(
        _inproj_kernel,
        grid=(m // tm, ZP // ZP_TILE),
        in_specs=[pl.BlockSpec((tm, D_MODEL), lambda i, j: (i, 0)),
                  pl.BlockSpec((1, D_MODEL), lambda i, j: (0, 0)),
                  pl.BlockSpec((D_MODEL, ZP_TILE), lambda i, j: (0, j))],
        out_specs=[pl.BlockSpec((tm, ZP_TILE), lambda i, j: (i, j)),
                   pl.BlockSpec((tm, ZP_TILE), lambda i, j: (i, j))],
        out_shape=[jax.ShapeDtypeStruct((m, ZP), F32), jax.ShapeDtypeStruct((m, ZP), BF16)],
        scratch_shapes=[pltpu.VMEM((tm, D_MODEL), BF16)],
        compiler_params=_cparams(("parallel", "arbitrary")),
        name="inproj",
    )(x, g.reshape(1, D_MODEL), w_pad)


def _prep_w_in(w_in):
    a = jnp.concatenate([w_in[:, :1024] * Q_SCALE, w_in[:, 1024:3072], w_in[:, 3072:4096] * Q_SCALE], axis=1)
    kv = w_in[:, 4096:4864]
    ng = w_in[:, 4864:4912]
    mg = w_in[:, 4912:]
    return jnp.concatenate([a, mg, kv, jnp.pad(ng, ((0, 0), (0, LANE - ng.shape[1])))], axis=1).astype(BF16)


def _matmul_kernel(x_ref, w_ref, o_ref):
    o_ref[...] = jnp.dot(x_ref[...].astype(BF16), w_ref[...], preferred_element_type=F32)


def _matmul(x, w_bf16):
    m, n = x.shape[0], w_bf16.shape[1]
    return pl.pallas_call(
        _matmul_kernel,
        out_shape=jax.ShapeDtypeStruct((m, n), F32),
        compiler_params=pltpu.CompilerParams(vmem_limit_bytes=VMEM_LIMIT),
        name="matmul",
    )(x, w_bf16)


def _da_post(o0, o1, lam, sub):
    o = o0 - lam * o1
    return _rms(o, sub) * (1.0 - LAM_INIT)


def _split3(x):
    a = x.astype(BF16).astype(F32)
    r = x - a
    b = r.astype(BF16).astype(F32)
    return a, b, (r - b).astype(BF16).astype(F32)


def _thirds(lane, pieces):
    which = lane % 3
    return jnp.where(which == 0, pieces[0], jnp.where(which == 1, pieces[1], pieces[2]))


def _da_prompt_kernel(sc_ref, q_ref, k_ref, v_ref, sub_ref, o_ref, ka_ref, va_ref, qa_ref, sa_ref, sb_ref, m_ref, acc_ref,
                      *, tq, tkb):
    h = pl.program_id(0)
    qi = pl.program_id(1)
    lam = sc_ref[0]
    slope = sc_ref[1 + h]
    per_big = tkb // tq
    seq = ka_ref.shape[0]

    @pl.when(qi == 0)
    def _():
        j = lax.broadcasted_iota(jnp.int32, (tkb, LANE), 0)
        grp = lax.broadcasted_iota(jnp.int32, (tkb, LANE), 1) // 3
        kfeat = jnp.where(grp == 0, (j // 32) * 32, jnp.where(grp == 1, j % 32, jnp.where(grp == 2, 1, 0)))
        kfeat = kfeat.astype(F32).astype(BF16)
        ka_ref[:, :LANE] = k_ref[...]
        for c in range(seq // tkb):
            ka_ref[c * tkb:(c + 1) * tkb, LANE:] = kfeat
        va_ref[:, :LANE] = v_ref[...]
        va_ref[:, LANE:] = jnp.ones((seq, LANE), BF16)
        lane2 = lax.broadcasted_iota(jnp.int32, (2 * tq, LANE), 1)
        i_loc = (lax.broadcasted_iota(jnp.int32, (2 * tq, LANE), 0) & (tq - 1)).astype(F32)
        sl_pieces = _thirds(lane2, _split3(jnp.full((2 * tq, LANE), slope, F32)))
        row_pieces = _thirds(lane2, _split3(-slope * i_loc))
        qfeat = jnp.where(lane2 < 6, sl_pieces, jnp.where(lane2 < 9, row_pieces, 0.0))
        qa_ref[:, LANE:] = qfeat.astype(BF16)

    q = q_ref[...]
    lane = lax.broadcasted_iota(jnp.int32, (tq, LANE), 1)
    zero = jnp.zeros_like(q)
    qa_ref[0:tq, :LANE] = jnp.where(lane < DA_HD, q, zero)
    qa_ref[tq:2 * tq, :LANE] = jnp.where(lane >= DA_HD, q, zero)
    m_ref[...] = jnp.full(m_ref.shape, NEG, F32)
    acc_ref[...] = jnp.zeros(acc_ref.shape, F32)

    def scores(idx):
        k = ka_ref[pl.ds(pl.multiple_of(idx * tkb, tkb), tkb), :]
        return lax.dot_general(qa_ref[...], k, NT_DIMS, preferred_element_type=F32)

    def consume(idx, s_ref, masked):
        base = qi * tq - idx * tkb
        cc = slope * base.astype(F32)
        va = va_ref[pl.ds(pl.multiple_of(idx * tkb, tkb), tkb), :]
        m_old = m_ref[...]
        if masked:
            ii = lax.broadcasted_iota(jnp.int32, s_ref.shape, 0) & (tq - 1)
            jj = lax.broadcasted_iota(jnp.int32, s_ref.shape, 1)
            t = jnp.where(jj - ii <= base, s_ref[...], NEG)
            m_new = jnp.maximum(m_old, jnp.max(t, axis=-1, keepdims=True) - cc)
            p = jnp.exp2(t - (m_new + cc))
        else:
            m_new = jnp.maximum(m_old, jnp.max(s_ref[...], axis=-1, keepdims=True) - cc)
            p = jnp.exp2(s_ref[...] - (m_new + cc))
        alpha = jnp.exp2(m_old - m_new)
        acc_ref[...] = alpha * acc_ref[...] + jnp.dot(p.astype(BF16), va, preferred_element_type=F32)
        m_ref[...] = m_new

    nbig = qi // per_big
    sa_ref[...] = scores(0)

    def run(first, count):
        bufs = (sa_ref, sb_ref)
        for i in range(count):
            bufs[(i + 1) % 2][...] = scores(first + i + 1)
            consume(first + i, bufs[i % 2], False)

    def quad(j, carry):
        run(4 * j, 4)
        return carry

    lax.fori_loop(0, nbig // 4, quad, 0)

    @pl.when(nbig % 4 >= 2)
    def _():
        run((nbig // 4) * 4, 2)

    @pl.when(nbig % 2 == 1)
    def _():
        sb_ref[...] = scores(nbig)
        consume(nbig - 1, sa_ref, False)
        consume(nbig, sb_ref, True)

    @pl.when(nbig % 2 == 0)
    def _():
        consume(nbig, sa_ref, True)

    o0 = acc_ref[0:tq, :LANE] / acc_ref[0:tq, LANE:]
    o1 = acc_ref[tq:2 * tq, :LANE] / acc_ref[tq:2 * tq, LANE:]
    o_ref[...] = _da_post(o0, o1, lam, sub_ref[...])


def _da_prompt(zb, sc, subln, tq=512, tkb=1024):
    s = zb.shape[0]
    tq = min(tq, s)
    tkb = min(tkb, s)
    kb, vb = C_DK // LANE, C_DV // LANE
    return pl.pallas_call(
        functools.partial(_da_prompt_kernel, tq=tq, tkb=tkb),
        grid=(DA_HEADS, s // tq),
        in_specs=[_smem_spec(),
                  pl.BlockSpec((tq, LANE), lambda h, i: (i, h)),
                  pl.BlockSpec((s, LANE), lambda h, i: (0, kb + h), pipeline_mode=pl.Buffered(1)),
                  pl.BlockSpec((s, LANE), lambda h, i: (0, vb + h), pipeline_mode=pl.Buffered(1)),
                  pl.BlockSpec((1, LANE), lambda h, i: (0, 0))],
        out_specs=pl.BlockSpec((tq, LANE), lambda h, i: (i, h)),
        out_shape=jax.ShapeDtypeStruct((s, DA_HEADS * LANE), F32),
        scratch_shapes=[pltpu.VMEM((s, 2 * LANE), BF16),
                        pltpu.VMEM((s, 2 * LANE), BF16),
                        pltpu.VMEM((2 * tq, 2 * LANE), BF16),
                        pltpu.VMEM((2 * tq, tkb), F32), pltpu.VMEM((2 * tq, tkb), F32),
                        pltpu.VMEM((2 * tq, 1), F32), pltpu.VMEM((2 * tq, 2 * LANE), F32)],
        compiler_params=_cparams(("arbitrary", "arbitrary")),
        name="da_prompt",
    )(sc, zb, zb, zb, subln.reshape(1, LANE))


def _compress_kernel(pt_ref, *refs, pp, nch):
    del pt_ref
    pages = refs[:pp]
    tail_ref, wbig_ref, w1_ref, pe_ref, w2_ref, out_ref, ab_ref, rows_ref, x_ref = refs[pp:]
    s = pl.program_id(1)
    for k in range(pp):
        rows_ref[k] = pages[k][0].T
    for k in range(pp):
        for pos in range(CMP_STRIDE):
            x_ref[8 * k:8 * (k + 1), LANE * pos:LANE * (pos + 1)] = rows_ref[k, pl.ds(pos, 8, stride=CMP_STRIDE), :]
    x = x_ref[...].astype(BF16)
    rows = 8 * pp
    ab_ref[pl.ds(pl.multiple_of(s * rows, rows), rows), :] = jnp.dot(x, wbig_ref[...], preferred_element_type=F32)

    @pl.when(s == pl.num_programs(1) - 1)
    def _():
        ab_ref[nch:nch + 8, :] = jnp.dot(tail_ref[0].astype(BF16), wbig_ref[...], preferred_element_type=F32)
        hpe = jnp.dot(pe_ref[...].astype(BF16), w1_ref[...], preferred_element_type=F32)[0:1]
        outs = []
        hid = 2 * NSA_HD
        for g in range(NSA_GROUPS):
            a = ab_ref[0:nch, 2 * hid * g:2 * hid * g + hid]
            b = ab_ref[1:nch + 1, 2 * hid * g + hid:2 * hid * (g + 1)]
            hd = a + b + hpe
            act = hd * _sigmoid(hd)
            outs.append(jnp.dot(act.astype(BF16), w2_ref[...], preferred_element_type=F32))
        out_ref[0] = jnp.concatenate(outs, axis=-1)


def _compress(pool, page_table, tail, w1, pe, w2):
    b, n_pages = page_table.shape
    nch = n_pages * 8
    pp = _pages_per_step(n_pages, 32)
    ck = CMP_STRIDE * NSA_GROUPS * NSA_HD
    hid = 2 * NSA_HD
    w1r = w1.reshape(2, CMP_STRIDE, NSA_HD, hid)
    wbig = jnp.einsum('psdh,gk->sgdkph', w1r, jnp.eye(NSA_GROUPS, dtype=F32)).reshape(ck, 2 * NSA_GROUPS * hid).astype(BF16)
    pe8 = jnp.pad(pe.reshape(1, CMP_LEN * NSA_HD), ((0, 7), (0, 0)))

    def page_spec(k):
        return pl.BlockSpec((1, LANE, PAGE), lambda bi, s, pt: (pt[bi * n_pages + s * pp + k], 0, 0))

    const = lambda shape: pl.BlockSpec(shape, lambda bi, s, pt: tuple(0 for _ in shape))
    grid_spec = pltpu.PrefetchScalarGridSpec(
        num_scalar_prefetch=1,
        grid=(b, n_pages // pp),
        in_specs=[page_spec(k) for k in range(pp)] + [
            pl.BlockSpec((1, 8, ck), lambda bi, s, pt: (bi, 0, 0)),
            const((ck, 2 * NSA_GROUPS * hid)), const((CMP_LEN * NSA_HD, hid)), const((8, CMP_LEN * NSA_HD)),
            const((hid, NSA_HD))],
        out_specs=pl.BlockSpec((1, nch, NSA_GROUPS * NSA_HD), lambda bi, s, pt: (bi, 0, 0)),
        scratch_shapes=[pltpu.VMEM((nch + 8, 2 * NSA_GROUPS * hid), F32), pltpu.VMEM((pp, PAGE, LANE), F32),
                        pltpu.VMEM((8 * pp, ck), F32)],
    )
    return pl.pallas_call(
        functools.partial(_compress_kernel, pp=pp, nch=nch),
        grid_spec=grid_spec,
        out_shape=jax.ShapeDtypeStruct((b, nch, NSA_GROUPS * NSA_HD), F32),
        compiler_params=_cparams(("arbitrary", "arbitrary")),
        name="compress",
    )(page_table.reshape(-1), *([pool] * pp), tail, wbig, w1.astype(BF16), pe8, w2.astype(BF16))


def _stack_group_queries(q, g, tq):
    lane = lax.broadcasted_iota(jnp.int32, (tq, 2 * NSA_HD), 1)
    mine = jnp.where(lane >= NSA_HD, 1, 0) == g
    parts = []
    for hh in range(NSA_HPG):
        qh = q[:, NSA_HD * hh:NSA_HD * (hh + 1)]
        parts.append(jnp.where(mine, jnp.concatenate([qh, qh], axis=1), jnp.zeros((tq, 2 * NSA_HD), q.dtype)))
    return jnp.concatenate(parts, axis=0)


def _group_half(x, g):
    return jnp.where(g == 0, x[:, :NSA_HD], x[:, NSA_HD:])


def _cmp_topk_kernel(sl_ref, q_ref, kc_ref, vc_ref, agg_ref, oc_ref, sel_ref, any_ref, *, tq, nb, nselp, qpos_base, topk,
                     levels):
    g = pl.program_id(1)
    t = pl.program_id(2)
    q0 = qpos_base + t * tq
    qpad = _stack_group_queries(q_ref[...], g, tq)

    def work(nbw, nsw):
        kcb = kc_ref[0, :nbw, :].astype(BF16)
        vcb = vc_ref[0, :nbw, :].astype(BF16)
        s_all = lax.dot_general(qpad, kcb, NT_DIMS, preferred_element_type=F32)
        i = lax.broadcasted_iota(jnp.int32, (tq, nbw), 0)
        n = lax.broadcasted_iota(jnp.int32, (tq, nbw), 1)
        dist = (q0 + i - (CMP_STRIDE * n + (CMP_LEN - 1))).astype(F32)
        mask = dist >= 0
        row_ok = dist[:, 0:1] >= 0.0
        psum = jnp.zeros((tq, nbw), F32)
        for hh in range(NSA_HPG):
            slope = sl_ref[NSA_HPG * g + hh]
            tt = jnp.where(mask, s_all[hh * tq:(hh + 1) * tq] - slope * dist, NEG)
            m = jnp.max(tt, axis=-1, keepdims=True)
            e = jnp.exp2(tt - m)
            l = jnp.sum(e, axis=-1, keepdims=True)
            p = e * jnp.where(row_ok, 1.0 / l, 0.0)
            psum = psum + p
            o = jnp.dot(p.astype(BF16), vcb, preferred_element_type=F32)
            oc_ref[:, NSA_HD * hh:NSA_HD * (hh + 1)] = _group_half(o, g)

        p_hi = psum.astype(BF16)
        p_lo = (psum - p_hi.astype(F32)).astype(BF16)
        agg = agg_ref[:nbw, :nsw]
        imp = jnp.dot(p_hi, agg, preferred_element_type=F32) + jnp.dot(p_lo, agg, preferred_element_type=F32)
        jj = lax.broadcasted_iota(jnp.int32, (tq, nsw), 1)
        cur = (q0 + lax.broadcasted_iota(jnp.int32, (tq, nsw), 0)) // SEL_BLOCK
        valid = jj <= cur
        forced = jnp.where(valid, jnp.where(jj == 0, 1, jnp.where(jj >= cur - 1, 1, 0)), 0)
        score = jnp.where(valid, imp + jnp.where(forced == 1, FORCE_BONUS, 0.0), NEG)

        tqp = -(-tq // LANE) * LANE
        if tqp > tq:
            score = jnp.concatenate([score, jnp.full((tqp - tq, nsw), NEG, F32)], axis=0)
        cand = lax.broadcasted_iota(jnp.int32, (nsw, tqp), 0)

        def pick(_, carry):
            sc, chosen = carry
            mx = jnp.max(sc, axis=0, keepdims=True)
            idx = jnp.min(jnp.where(sc == mx, cand, nsw), axis=0, keepdims=True)
            hit = cand == idx
            return jnp.where(hit, PICKED, sc), jnp.where(hit, 1.0, chosen)

        _, chosen = lax.fori_loop(0, topk, pick, (score.T, jnp.zeros((nsw, tqp), F32)))
        sel = jnp.where(valid, chosen.T[:tq], 0.0)
        if nsw < nselp:
            sel = jnp.concatenate([sel, jnp.zeros((tq, nselp - nsw), F32)], axis=1)
        sel_ref[0, 0] = sel
        any_ref[0, 0, 0] = jnp.broadcast_to(jnp.max(sel, axis=0, keepdims=True), (8, nselp))

    if levels == 1:
        work(nb, nselp)
    else:
        unit = nselp // levels
        need = (q0 + tq - 1) // SEL_BLOCK + 1
        lvl = (need + unit - 1) // unit
        for lv in range(1, levels + 1):
            @pl.when(lvl == lv)
            def _():
                work(nb * lv // levels, unit * lv)


def _sel_agg_matrix(nblk, nsel, nb, nselp):
    m = np.zeros((nb, nselp), np.float32)
    j = np.arange(nsel)
    r, c = SEL_BLOCK // CMP_STRIDE, CMP_LEN // CMP_STRIDE
    for a in range(r):
        for b in range(c):
            i = r * j + a - b
            ok = (i >= 0) & (i < nblk)
            np.add.at(m, (i[ok], j[ok]), 1.0)
    return jnp.asarray(m, dtype=BF16)


def _cmp_topk(qarr, qcol0, kcb, vcb, slopes, *, batch, sq, tq, nblk, nsel, qpos_base):
    nb = kcb.shape[1]
    nselp = -(-nsel // LANE) * LANE
    nt = sq // tq
    agg = _sel_agg_matrix(nblk, nsel, nb, nselp)
    gw = NSA_HPG * NSA_HD
    ratio = SEL_BLOCK // CMP_STRIDE
    levels = 4 if (qpos_base == 0 and nb == ratio * nselp and nb % (4 * LANE) == 0 and sq >= SEL_BLOCK * nselp) else 1
    return pl.pallas_call(
        functools.partial(_cmp_topk_kernel, tq=tq, nb=nb, nselp=nselp, qpos_base=qpos_base, topk=min(SEL_TOP, nsel),
                          levels=levels),
        grid=(batch, NSA_GROUPS, nt),
        in_specs=[_smem_spec(),
                  pl.BlockSpec((tq, gw), lambda b, g, t: (b * nt + t, qcol0 + g)),
                  pl.BlockSpec((1, nb, LANE), lambda b, g, t: (b, 0, 0)),
                  pl.BlockSpec((1, nb, LANE), lambda b, g, t: (b, 0, 0)),
                  pl.BlockSpec((nb, nselp), lambda b, g, t: (0, 0))],
        out_specs=[pl.BlockSpec((tq, gw), lambda b, g, t: (b * nt + t, g)),
                   pl.BlockSpec((1, 1, tq, nselp), lambda b, g, t: (b, g, t, 0)),
                   pl.BlockSpec((1, 1, 1, 8, nselp), lambda b, g, t: (b, g, t, 0, 0))],
        out_shape=[jax.ShapeDtypeStruct((batch * sq, NSA_GROUPS * gw), F32),
                   jax.ShapeDtypeStruct((batch, NSA_GROUPS, sq, nselp), F32),
                   jax.ShapeDtypeStruct((batch, NSA_GROUPS, nt, 8, nselp), F32)],
        compiler_params=_cparams(("arbitrary", "arbitrary", "arbitrary")),
        name="cmp_topk",
    )(slopes, qarr, kcb, vcb, agg)


def _nsa_sw_kernel(fl_ref, sl_ref, q_ref, ks_ref, vs_ref, kw_ref, vw_ref, sel_ref, oc_ref, ng_ref, out_ref,
                   ids_ref, m_ref, acc_ref, *, tq, nt, nselp, nwords, sb):
    g = pl.program_id(0)
    t = pl.program_id(1)
    rows = NSA_HPG * tq
    qpad = _stack_group_queries(q_ref[...], g, tq)
    slope_row = jnp.concatenate([jnp.full((tq, 1), sl_ref[NSA_HPG * g + hh], F32) for hh in range(NSA_HPG)], axis=0)
    i_loc = lax.broadcasted_iota(jnp.int32, (rows, LANE), 0) & (tq - 1)
    j_loc = lax.broadcasted_iota(jnp.int32, (rows, LANE), 1)
    dloc = (i_loc - j_loc).astype(F32)
    sl_dloc = slope_row * dloc
    selb = sel_ref[0, 0].astype(BF16)
    ones = jnp.ones((LANE, LANE), BF16)

    def chunk(ref, c):
        return ref[pl.ds(pl.multiple_of(jnp.maximum(c, 0) * LANE, LANE), LANE), :]

    def with_ones(v):
        return jnp.concatenate([v, jnp.concatenate([ones] * (v.shape[0] // LANE), axis=0)], axis=1)

    def sel_mask(c):
        jrow = lax.broadcasted_iota(jnp.int32, (nselp, LANE), 0)
        r = lax.broadcasted_iota(jnp.int32, (nselp, LANE), 1)
        expand = jnp.where(jrow == 2 * c + r // SEL_BLOCK, 1.0, 0.0).astype(BF16)
        return jnp.dot(selb, expand, preferred_element_type=F32)

    def per_head(mk, x):
        x3 = x.reshape(NSA_HPG, tq, LANE)
        return jnp.where(mk[None] > 0.5, x3, NEG).reshape(rows, LANE)

    s = lax.dot_general(qpad, chunk(ks_ref, t), NT_DIMS, preferred_element_type=F32)
    mk = jnp.where(dloc[:tq] >= 0.0, sel_mask(t), 0.0)
    tt = per_head(mk, s - sl_dloc)
    m0 = jnp.max(tt, axis=-1, keepdims=True)
    p = jnp.exp2(tt - m0)
    m_ref[...] = m0
    acc_ref[...] = jnp.dot(p.astype(BF16), with_ones(chunk(vs_ref, t)), preferred_element_type=F32)

    def scan(c, cnt):
        word = fl_ref[(g * nt + t) * nwords + c // 32]
        bit = lax.shift_right_logical(word, c % 32) & 1

        @pl.when(bit == 1)
        def _():
            ids_ref[cnt] = c

        return cnt + bit

    cnt = lax.fori_loop(0, t, scan, 0)
    for k in range(sb):
        ids_ref[cnt + k] = -1

    def sel_step(si, carry):
        cs = [ids_ref[si * sb + k] for k in range(sb)]
        kk = jnp.concatenate([chunk(ks_ref, c) for c in cs], axis=0)
        vv = with_ones(jnp.concatenate([chunk(vs_ref, c) for c in cs], axis=0))
        keep = [sel_mask(c) > 0.5 for c in cs]
        offs = [((t - c) * tq).astype(F32) for c in cs]
        half = rows // 2
        for r0 in (0, half):
            rs = slice(r0, r0 + half)
            s = lax.dot_general(qpad[rs], kk, NT_DIMS, preferred_element_type=F32)
            slabs = []
            for k in range(sb):
                x = s[:, k * LANE:(k + 1) * LANE] - (sl_dloc[rs] + slope_row[rs] * offs[k])
                x = jnp.where(keep[k][None], x.reshape(NSA_HPG // 2, tq, LANE), NEG)
                slabs.append(x.reshape(half, LANE))
            tt = jnp.concatenate(slabs, axis=1)
            m_old = m_ref[rs, :]
            m_new = jnp.maximum(m_old, jnp.max(tt, axis=-1, keepdims=True))
            alpha = jnp.exp2(m_old - m_new)
            p = jnp.exp2(tt - m_new)
            acc_ref[rs, :] = alpha * acc_ref[rs, :] + jnp.dot(p.astype(BF16), vv, preferred_element_type=F32)
            m_ref[rs, :] = m_new
        return carry

    nsteps = (cnt + sb - 1) // sb

    def step_pair(j, carry):
        sel_step(2 * j, carry)
        return sel_step(2 * j + 1, carry)

    lax.fori_loop(0, nsteps // 2, step_pair, 0)

    @pl.when(nsteps % 2 == 1)
    def _():
        sel_step(nsteps - 1, 0)
    o_s = acc_ref[:, :LANE] / acc_ref[:, LANE:]

    nback = WINDOW // tq
    c0 = jnp.maximum(t - nback, 0)
    wlen = (nback + 1) * LANE
    wstart = pl.multiple_of(c0 * LANE, LANE)
    s = lax.dot_general(qpad, kw_ref[pl.ds(wstart, wlen), :], NT_DIMS, preferred_element_type=F32)
    slabs = []
    for k in range(nback + 1):
        dist = dloc + ((t - (c0 + k)) * tq).astype(F32)
        x = s[:, k * LANE:(k + 1) * LANE] - slope_row * dist
        slabs.append(jnp.where(dist >= 0.0, jnp.where(dist < float(WINDOW), x, NEG), NEG))
    tt = jnp.concatenate(slabs, axis=1)
    p = jnp.exp2(tt - jnp.max(tt, axis=-1, keepdims=True))
    aw = jnp.dot(p.astype(BF16), with_ones(vw_ref[pl.ds(wstart, wlen), :]), preferred_element_type=F32)
    o_w = aw[:, :LANE] / aw[:, LANE:]

    gate = _sigmoid(ng_ref[...])
    glane = lax.broadcasted_iota(jnp.int32, (tq, LANE), 1)

    def gate_col(idx):
        return jnp.sum(jnp.where(glane == idx, gate, 0.0), axis=-1, keepdims=True)

    for hh in range(NSA_HPG):
        base = 3 * (NSA_HPG * g + hh)
        r0, r1 = hh * tq, (hh + 1) * tq
        o = (gate_col(base) * oc_ref[:, NSA_HD * hh:NSA_HD * (hh + 1)]
             + gate_col(base + 1) * _group_half(o_s[r0:r1], g)
             + gate_col(base + 2) * _group_half(o_w[r0:r1], g))
        out_ref[:, NSA_HD * hh:NSA_HD * (hh + 1)] = o


def _pack_chunk_flags(anyblk):
    g, nt, nselp = anyblk.shape
    chunk = jnp.max(anyblk.reshape(g, nt, nselp // 2, 2), axis=-1) > 0.5
    nchunk = nselp // 2
    nwords = -(-nchunk // 32)
    chunk = jnp.pad(chunk, ((0, 0), (0, 0), (0, nwords * 32 - nchunk)))
    bits = chunk.reshape(g, nt, nwords, 32).astype(jnp.uint32) << jnp.arange(32, dtype=jnp.uint32)
    words = jnp.sum(bits, axis=-1, dtype=jnp.uint32)
    return lax.bitcast_convert_type(words, jnp.int32).reshape(-1), nwords


def _nsa_sw_prompt(z, zb, sel, anyblk, oc, slopes, tq=128, sb=4):
    s = zb.shape[0]
    assert tq == LANE and s >= WINDOW + tq
    nt = s // tq
    nselp = sel.shape[-1]
    flags, nwords = _pack_chunk_flags(anyblk[0, :, :, 0, :])
    gw = NSA_HPG * NSA_HD
    rows = NSA_HPG * tq
    res = lambda col: pl.BlockSpec((s, LANE), lambda g, t, fl: (0, col // LANE), pipeline_mode=pl.Buffered(1))
    grid_spec = pltpu.PrefetchScalarGridSpec(
        num_scalar_prefetch=1,
        grid=(NSA_GROUPS, nt),
        in_specs=[_smem_spec(),
                  pl.BlockSpec((tq, gw), lambda g, t, fl: (t, C_NQ // gw + g)),
                  res(C_KS), res(C_VS), res(C_KW), res(C_VW),
                  pl.BlockSpec((1, 1, tq, nselp), lambda g, t, fl: (0, g, t, 0)),
                  pl.BlockSpec((tq, gw), lambda g, t, fl: (t, g)),
                  pl.BlockSpec((tq, LANE), lambda g, t, fl: (t, C_NG // LANE))],
        out_specs=pl.BlockSpec((tq, gw), lambda g, t, fl: (t, g)),
        scratch_shapes=[pltpu.SMEM((nt + sb,), jnp.int32), pltpu.VMEM((rows, 1), F32),
                        pltpu.VMEM((rows, 2 * LANE), F32)],
    )
    return pl.pallas_call(
        functools.partial(_nsa_sw_kernel, tq=tq, nt=nt, nselp=nselp, nwords=nwords, sb=sb),
        grid_spec=grid_spec,
        out_shape=jax.ShapeDtypeStruct((s, NSA_GROUPS * gw), F32),
        compiler_params=_cparams(("arbitrary", "arbitrary")),
        name="nsa_sel_win",
    )(flags, slopes, zb, zb, zb, zb, zb, sel, oc, z)


def _decode_kernel(pt_ref, pos_ref, cnt_ref, *refs, pp, n_pages, ncols, hk, qpos0, win, n_new, has_sel, nselp,
                   feature_major):
    del pt_ref
    kpages = refs[:pp]
    vpages = refs[pp:2 * pp]
    rest = refs[2 * pp:]
    if has_sel:
        wq_ref, ci_ref, kn_ref, vn_ref, sel_ref, o_ref, m_ref, s_ref, acc_ref = rest
    else:
        wq_ref, ci_ref, kn_ref, vn_ref, o_ref, m_ref, s_ref, acc_ref = rest
        sel_ref = None
    bi = pl.program_id(0)
    st = pl.program_id(1)
    windowed = win < 1e8
    guarded = has_sel or windowed

    @pl.when(st == 0)
    def _():
        m_ref[...] = jnp.full(m_ref.shape, NEG, F32)
        s_ref[...] = jnp.zeros(s_ref.shape, F32)
        acc_ref[...] = jnp.zeros(acc_ref.shape, F32)

    wq = wq_ref[0]
    slope = ci_ref[:, 0:1]
    qrel = ci_ref[:, 1:2]
    colhead = ci_ref[:, 2:3]

    def geometry(nrows):
        r = lax.broadcasted_iota(jnp.int32, (ncols, nrows), 1)
        kidx = (r // hk).astype(F32)
        base = slope * kidx
        if hk > 1:
            base = jnp.where((r % hk).astype(F32) == colhead, base, NEG)
        return r, kidx, base

    def scores(k, kpos0, n_valid, geom, check_range, feature_major):
        r, kidx, base = geom
        if feature_major:
            s = jnp.dot(wq, k.astype(BF16), preferred_element_type=F32)
        else:
            s = lax.dot_general(wq, k.astype(BF16), NT_DIMS, preferred_element_type=F32)
        off = qrel + (qpos0 - kpos0).astype(F32)
        t = (s + base) - slope * off
        if not (check_range or has_sel or n_valid is not None):
            return t, None
        okf = jnp.ones(t.shape, F32)
        if check_range:
            dist = off - kidx
            okf = jnp.where(dist >= 0.0, jnp.where(dist < win, 1.0, 0.0), 0.0)
        if n_valid is not None:
            okf = jnp.where(r < n_valid, okf, 0.0)
        if has_sel:
            nrows = t.shape[1]
            jrow = lax.broadcasted_iota(jnp.int32, (nselp, nrows), 0)
            kp = kpos0 + lax.broadcasted_iota(jnp.int32, (nselp, nrows), 1)
            expand = jnp.where(jrow == kp // SEL_BLOCK, 1.0, 0.0).astype(BF16)
            okf = okf * jnp.dot(sel_ref[0].astype(BF16), expand, preferred_element_type=F32)
        ok = okf > 0.5
        return jnp.where(ok, t, NEG), ok

    def update(ts, oks, vs, feature_major):
        m_old = m_ref[...]
        m_new = m_old
        for t in ts:
            m_new = jnp.maximum(m_new, jnp.max(t, axis=-1, keepdims=True))
        alpha = jnp.exp2(m_old - m_new)
        l = alpha * s_ref[...]
        acc = alpha * acc_ref[...]
        for t, ok, v in zip(ts, oks, vs):
            p = jnp.exp2(t - m_new)
            if ok is not None:
                p = jnp.where(ok, p, 0.0)
            l = l + jnp.sum(p, axis=-1, keepdims=True)
            if feature_major:
                acc = acc + lax.dot_general(p.astype(BF16), v.astype(BF16), NT_DIMS, preferred_element_type=F32)
            else:
                acc = acc + jnp.dot(p.astype(BF16), v.astype(BF16), preferred_element_type=F32)
        s_ref[...] = l
        acc_ref[...] = acc
        m_ref[...] = m_new

    cnt = cnt_ref[bi]

    @pl.when(st * pp < cnt)
    def _():
        geom = geometry(PAGE * hk)
        ts, oks, vs = [], [], []
        for k in range(pp):
            slot = st * pp + k
            n_valid = jnp.where(slot < cnt, PAGE * hk, 0) if guarded else None
            t, ok = scores(kpages[k][0], pos_ref[bi * n_pages + slot], n_valid, geom, windowed, feature_major)
            ts.append(t)
            oks.append(ok)
            vs.append(vpages[k][0])
        update(ts, oks, vs, feature_major)

    @pl.when(st == pl.num_programs(1) - 1)
    def _():
        t, ok = scores(kn_ref[0], jnp.int32(qpos0), n_new * hk, geometry(PAGE), True, False)
        update([t], [ok], [vn_ref[0]], False)
        l = s_ref[...]
        o_ref[0] = acc_ref[...] / jnp.where(l == 0.0, 1.0, l)


def _decode_attn(wq, colinfo, pool_k, pool_v, page_ids, page_pos, page_cnt, knew, vnew, sel, *, hk, qpos0, win, n_new, pp,
                 feature_major=False):
    b, n_pages = page_ids.shape
    ncols, width = wq.shape[1], wq.shape[2]
    pp = _pages_per_step(n_pages, pp)
    has_sel = sel is not None
    nselp = sel.shape[-1] if has_sel else 0

    def page_spec(k):
        return pl.BlockSpec((1, width, PAGE) if feature_major else (1, PAGE * hk, width),
                            lambda bi, s, pt, pos, cnt: (pt[bi * n_pages + s * pp + k], 0, 0))

    per_batch = lambda shape: pl.BlockSpec((1,) + shape, lambda bi, s, pt, pos, cnt: (bi, 0, 0))
    in_specs = ([page_spec(k) for k in range(pp)] * 2
                + [per_batch((ncols, width)), pl.BlockSpec((ncols, LANE), lambda bi, s, pt, pos, cnt: (0, 0)),
                   per_batch((PAGE, width)), per_batch((PAGE, width))])
    args = [pool_k] * pp + [pool_v] * pp + [wq, colinfo, knew, vnew]
    if has_sel:
        in_specs.append(per_batch((ncols, nselp)))
        args.append(sel)
    grid_spec = pltpu.PrefetchScalarGridSpec(
        num_scalar_prefetch=3,
        grid=(b, n_pages // pp),
        in_specs=in_specs,
        out_specs=per_batch((ncols, width)),
        scratch_shapes=[pltpu.VMEM((ncols, 1), F32), pltpu.VMEM((ncols, 1), F32), pltpu.VMEM((ncols, width), F32)],
    )
    return pl.pallas_call(
        functools.partial(_decode_kernel, pp=pp, n_pages=n_pages, ncols=ncols, hk=hk, qpos0=qpos0, win=float(win),
                          n_new=n_new, has_sel=has_sel, nselp=nselp, feature_major=feature_major),
        grid_spec=grid_spec,
        out_shape=jax.ShapeDtypeStruct((b, ncols, width), F32),
        compiler_params=_cparams(("arbitrary", "arbitrary")),
        name="decode_attn",
    )(page_ids.reshape(-1), page_pos.reshape(-1), page_cnt, *args)


def _all_pages(page_table, pos0):
    b, n_pages = page_table.shape
    pos = jnp.broadcast_to(pos0 + PAGE * jnp.arange(n_pages, dtype=jnp.int32), (b, n_pages))
    return page_table, pos, jnp.full((b,), n_pages, jnp.int32)


def _selected_pages(page_table, sel_cols):
    b, n_pages = page_table.shape
    per_page = PAGE // SEL_BLOCK
    hit = jnp.max(sel_cols[:, :, :per_page * n_pages].reshape(b, -1, n_pages, per_page), axis=(1, 3)) > 0.5
    cnt = jnp.sum(hit, axis=1).astype(jnp.int32)
    order = jnp.argsort(jnp.logical_not(hit), axis=1, stable=True).astype(jnp.int32)
    keep = jnp.minimum(jnp.arange(n_pages, dtype=jnp.int32)[None], jnp.maximum(cnt - 1, 0)[:, None])
    order = jnp.take_along_axis(order, keep, axis=1)
    return jnp.take_along_axis(page_table, order, axis=1), order * PAGE, cnt


def _da_post_kernel(sc_ref, o0_ref, o1_ref, sub_ref, o_ref):
    o_ref[...] = _da_post(o0_ref[...], o1_ref[...], sc_ref[0], sub_ref[...])


def _da_post_call(o0, o1, sc, subln):
    return pl.pallas_call(
        _da_post_kernel,
        in_specs=[_smem_spec(), pl.BlockSpec(o0.shape, lambda: (0, 0)), pl.BlockSpec(o0.shape, lambda: (0, 0)),
                  pl.BlockSpec((1, LANE), lambda: (0, 0))],
        out_specs=pl.BlockSpec(o0.shape, lambda: (0, 0)),
        out_shape=jax.ShapeDtypeStruct(o0.shape, F32),
        name="da_post",
    )(sc, o0, o1, subln.reshape(1, LANE))


def _gate3_kernel(ng_ref, e_ref, oc_ref, os_ref, ow_ref, o_ref):
    gate = _sigmoid(ng_ref[...])
    acc = jnp.zeros(o_ref.shape, F32)
    for br, ref in enumerate((oc_ref, os_ref, ow_ref)):
        ge = jnp.dot(gate, e_ref[br], preferred_element_type=F32, precision=lax.Precision.HIGHEST)
        acc = acc + ge * ref[...]
    o_ref[...] = acc


def _gate3(ng, oc, os_, ow):
    e = np.zeros((3, LANE, NSA_HEADS * NSA_HD), np.float32)
    for br in range(3):
        for h in range(NSA_HEADS):
            e[br, 3 * h + br, NSA_HD * h:NSA_HD * (h + 1)] = 1.0
    return pl.pallas_call(
        _gate3_kernel,
        out_shape=jax.ShapeDtypeStruct(oc.shape, F32),
        name="gate3",
    )(ng, jnp.asarray(e), oc, os_, ow)


def _tail_a_kernel(x_ref, da_ref, nsa_ref, ga_ref, gb_ref, wo_ref, nx_ref, wxq_ref, h_ref, q_ref):
    m = _sigmoid(ga_ref[...]) * da_ref[...] + _sigmoid(gb_ref[...]) * nsa_ref[...]
    h = x_ref[...] + jnp.dot(m.astype(BF16), wo_ref[...], preferred_element_type=F32)
    h_ref[...] = h
    xn = _rms(h, nx_ref[...]).astype(BF16)
    q_ref[...] = jnp.dot(xn, wxq_ref[...], preferred_element_type=F32)


def _tail_a(x, o_da, o_nsa, z, w_o, norm_x, w_xq):
    m = x.shape[0]
    tm = min(m, 512)
    row = lambda cb: pl.BlockSpec((tm, D_MODEL), lambda i: (i, cb))
    const = lambda shape: pl.BlockSpec(shape, lambda i: (0, 0), pipeline_mode=pl.Buffered(1))
    return pl.pallas_call(
        _tail_a_kernel,
        grid=(m // tm,),
        in_specs=[row(0), row(0), row(0), row(C_GA // D_MODEL), row(C_GB // D_MODEL),
                  const((D_MODEL, D_MODEL)), const((1, D_MODEL)), const((D_MODEL, X_W))],
        out_specs=[row(0), pl.BlockSpec((tm, X_W), lambda i: (i, 0))],
        out_shape=[jax.ShapeDtypeStruct((m, D_MODEL), F32), jax.ShapeDtypeStruct((m, X_W), F32)],
        compiler_params=_cparams(("parallel",)),
        name="tail_merge_wo",
    )(x, o_da, o_nsa, z, z, w_o.astype(BF16), norm_x.reshape(1, D_MODEL), w_xq.astype(BF16))


def _cross_kernel(q_ref, mk_ref, mv_ref, o_ref):
    q = (q_ref[0] * (X_HD ** -0.5)).astype(BF16)
    mk = mk_ref[0].astype(BF16)
    mv = mv_ref[0].astype(BF16)
    outs = []
    for h in range(X_HEADS):
        sl = slice(X_HD * h, X_HD * (h + 1))
        s = lax.dot_general(q[:, sl], mk[:, sl], NT_DIMS, preferred_element_type=F32)
        e = jnp.exp(s - jnp.max(s, axis=-1, keepdims=True))
        p = e / jnp.sum(e, axis=-1, keepdims=True)
        outs.append(jnp.dot(p.astype(BF16), mv[:, sl], preferred_element_type=F32))
    o_ref[0] = jnp.concatenate(outs, axis=-1)


def _cross(q, mk, mv):
    b, t, _ = q.shape
    mlen = mk.shape[1]
    tt = min(t, 512)
    return pl.pallas_call(
        _cross_kernel,
        grid=(b, t // tt),
        in_specs=[pl.BlockSpec((1, tt, X_W), lambda bi, i: (bi, i, 0)),
                  pl.BlockSpec((1, mlen, X_W), lambda bi, i: (bi, 0, 0)),
                  pl.BlockSpec((1, mlen, X_W), lambda bi, i: (bi, 0, 0))],
        out_specs=pl.BlockSpec((1, tt, X_W), lambda bi, i: (bi, i, 0)),
        out_shape=jax.ShapeDtypeStruct((b, t, X_W), F32),
        compiler_params=_cparams(("parallel", "parallel")),
        name="cross_attn",
    )(q, mk, mv)


def _tail_c_kernel(h_ref, ox_ref, wxo_ref, nf_ref, wg_ref, wu_ref, wd_ref, nfin_ref, y_ref, *, nchunk, chunk):
    h = h_ref[...] + jnp.dot(ox_ref[...].astype(BF16), wxo_ref[...], preferred_element_type=F32)
    xn = _rms(h, nf_ref[...]).astype(BF16)
    acc = jnp.zeros(h.shape, F32)
    for c in range(nchunk):
        sl = slice(c * chunk, (c + 1) * chunk)
        gt = jnp.dot(xn, wg_ref[:, sl], preferred_element_type=F32)
        up = jnp.dot(xn, wu_ref[:, sl], preferred_element_type=F32)
        act = (gt * _sigmoid(gt) * up).astype(BF16)
        acc = acc + jnp.dot(act, wd_ref[sl, :], preferred_element_type=F32)
    y_ref[...] = _rms(h + acc, nfin_ref[...])


def _tail_c(h, ox, w_xo, norm_ffn, w_gate_up, w_down, norm_final):
    m = h.shape[0]
    tm = min(m, 512)
    hid = w_down.shape[0]
    chunk = hid // 2
    const = lambda shape: pl.BlockSpec(shape, lambda i: (0, 0), pipeline_mode=pl.Buffered(1))
    return pl.pallas_call(
        functools.partial(_tail_c_kernel, nchunk=2, chunk=chunk),
        grid=(m // tm,),
        in_specs=[pl.BlockSpec((tm, D_MODEL), lambda i: (i, 0)), pl.BlockSpec((tm, X_W), lambda i: (i, 0)),
                  const((X_W, D_MODEL)), const((1, D_MODEL)), const((D_MODEL, hid)), const((D_MODEL, hid)),
                  const((hid, D_MODEL)), const((1, D_MODEL))],
        out_specs=pl.BlockSpec((tm, D_MODEL), lambda i: (i, 0)),
        out_shape=jax.ShapeDtypeStruct((m, D_MODEL), F32),
        compiler_params=_cparams(("parallel",)),
        name="tail_ffn",
    )(h, ox, w_xo.astype(BF16), norm_ffn.reshape(1, D_MODEL), w_gate_up[:, :hid].astype(BF16),
      w_gate_up[:, hid:].astype(BF16), w_down.astype(BF16), norm_final.reshape(1, D_MODEL))


def _alibi(n):
    return np.asarray(2.0 ** (-8.0 * np.arange(1, n + 1) / n) * LOG2E, dtype=np.float32)


def _finish(x, o_da, o_nsa, z, mk, mv, batch, w_o, norm_x, w_xq, w_xo, norm_ffn, w_gate_up, w_down, norm_final):
    m = x.shape[0]
    t = m // batch
    h1, qx = _tail_a(x, o_da, o_nsa, z, w_o, norm_x, w_xq)
    q3 = qx.reshape(batch, t, X_W)
    tpad = -(-t // 8) * 8
    if tpad != t:
        q3 = jnp.pad(q3, ((0, 0), (0, tpad - t), (0, 0)))
    ox = _cross(q3, mk, mv)[:, :t].reshape(m, X_W)
    return _tail_c(h1, ox, w_xo, norm_ffn, w_gate_up, w_down, norm_final)


def kernel(x_prompt, x_sample, cache_diff_k, cache_diff_v, cache_cmp_k, cache_cmp_v, cache_sel_k, cache_sel_v,
           cache_win_k, cache_win_v, cache_mem_k, cache_mem_v, page_table, mem_prompt,
           norm_mix, w_in, lam_q1, lam_k1, lam_q2, lam_k2, da_subln,
           w_cmp_k1, pe_cmp_k, w_cmp_k2, w_cmp_v1, pe_cmp_v, w_cmp_v2,
           w_o, norm_x, w_xq, w_mem_kv, w_xo, norm_ffn, w_gate_up, w_down, norm_final):
    batch, seq, _ = x_prompt.shape
    db, ds, _ = x_sample.shape
    assert batch == 1 and norm_mix.shape[0] == 1
    n_pages = page_table.shape[1]
    past = n_pages * PAGE
    wb = cache_win_k.shape[2]
    kvw = NSA_GROUPS * NSA_HD

    lam = (jnp.exp(jnp.sum(lam_q1[0] * lam_k1[0])) - jnp.exp(jnp.sum(lam_q2[0] * lam_k2[0])) + LAM_INIT).astype(F32)
    da_sc = jnp.concatenate([lam.reshape(1), jnp.asarray(_alibi(DA_HEADS))])
    nsa_sl = jnp.asarray(_alibi(NSA_HEADS))
    w_pad = _prep_w_in(w_in[0])
    tail_w = (w_o[0], norm_x[0], w_xq[0], w_xo[0], norm_ffn[0], w_gate_up[0], w_down[0], norm_final)
    cmp_k = (w_cmp_k1[0], pe_cmp_k[0], w_cmp_k2[0])
    cmp_v = (w_cmp_v1[0], pe_cmp_v[0], w_cmp_v2[0])

    xp = x_prompt.reshape(seq, D_MODEL)
    z, zb = _inproj(xp, norm_mix[0], w_pad)
    o_da = _da_prompt(zb, da_sc, da_subln[0])

    ident = jnp.arange(seq // PAGE, dtype=jnp.int32).reshape(1, -1)
    zero_tail = jnp.zeros((1, 8, CMP_STRIDE * kvw), F32)
    p_kc = z[:, C_KC:C_KC + kvw]
    p_vc = z[:, C_VC:C_VC + kvw]
    as_pages = lambda a: a.reshape(seq // PAGE, PAGE, kvw).transpose(0, 2, 1)
    kcb = _compress(as_pages(p_kc), ident, zero_tail, *cmp_k)
    vcb = _compress(as_pages(p_vc), ident, zero_tail, *cmp_v)
    nch = seq // CMP_STRIDE
    oc, sel, anyblk = _cmp_topk(zb, C_NQ // (NSA_HPG * NSA_HD), kcb, vcb, nsa_sl, batch=1, sq=seq, tq=128,
                                nblk=nch - 1, nsel=seq // SEL_BLOCK, qpos_base=0)
    o_nsa = _nsa_sw_prompt(z, zb, sel, anyblk, oc, nsa_sl)

    mem_kv = _matmul(mem_prompt.reshape(-1, D_MODEL), w_mem_kv[0].astype(BF16))
    p_mk, p_mv = mem_kv[:, :X_W], mem_kv[:, X_W:]
    y_prompt = _finish(xp, o_da, o_nsa, z, p_mk[None], p_mv[None], 1, *tail_w)

    r5 = lambda a, h: a.reshape(1, 1, a.shape[0], h, -1)
    p_states = (r5(z[:, C_DK:C_DK + 1024], DA_HEADS), r5(z[:, C_DV:C_DV + 1024], DA_HEADS),
                r5(p_kc, NSA_GROUPS), r5(p_vc, NSA_GROUPS),
                r5(z[:, C_KS:C_KS + kvw], NSA_GROUPS), r5(z[:, C_VS:C_VS + kvw], NSA_GROUPS),
                r5(z[seq - min(WINDOW, seq):, C_KW:C_KW + kvw], NSA_GROUPS),
                r5(z[seq - min(WINDOW, seq):, C_VW:C_VW + kvw], NSA_GROUPS),
                r5(p_mk, X_HEADS), r5(p_mv, X_HEADS))

    ms = db * ds
    xs = x_sample.reshape(ms, D_MODEL)
    zs, zsb = _inproj(xs, norm_mix[0], w_pad)
    z3 = zs.reshape(db, ds, ZP)
    pad_rows = lambda a, n: jnp.pad(a, ((0, 0), (0, n - a.shape[1]), (0, 0)))

    dq = z3[:, :, C_DQ:C_DQ + 1024].reshape(db, ds, DA_HEADS, 2, DA_HD)
    wq_da = jnp.einsum('bqhmd,mn->bhmqnd', dq, jnp.eye(2, dtype=F32))
    wq_da = wq_da.reshape(db, DA_HEADS * 2 * ds, 2 * DA_HD).astype(BF16)
    ci = np.zeros((DA_HEADS * 2 * ds, LANE), np.float32)
    ci[:, 0] = np.repeat(_alibi(DA_HEADS), 2 * ds)
    ci[:, 1] = np.tile(np.arange(ds), DA_HEADS * 2)
    ci[:, 2] = np.repeat(np.arange(DA_HEADS), 2 * ds)
    s_dk, s_dv = z3[:, :, C_DK:C_DK + 1024], z3[:, :, C_DV:C_DV + 1024]
    head_rows = lambda a: pad_rows(a.reshape(db, ds * DA_HEADS, 2 * DA_HD), PAGE)
    o_pair = _decode_attn(wq_da, jnp.asarray(ci), cache_diff_k[0].reshape(-1, PAGE * DA_HEADS, 2 * DA_HD),
                          cache_diff_v[0].reshape(-1, PAGE * DA_HEADS, 2 * DA_HD), *_all_pages(page_table, 0),
                          head_rows(s_dk), head_rows(s_dv), None, hk=DA_HEADS, qpos0=past, win=1e9, n_new=ds, pp=16)
    o_pair = o_pair.reshape(db, DA_HEADS, 2, ds, 2 * DA_HD).transpose(2, 0, 3, 1, 4)
    o_da_s = _da_post_call(o_pair[0].reshape(ms * DA_HEADS, LANE), o_pair[1].reshape(ms * DA_HEADS, LANE),
                           da_sc, da_subln[0]).reshape(ms, DA_HEADS * LANE)

    ck = CMP_STRIDE * kvw
    s_kc, s_vc = z3[:, :, C_KC:C_KC + kvw], z3[:, :, C_VC:C_VC + kvw]
    tail_of = lambda a: jnp.pad(a.reshape(db, 1, ds * kvw), ((0, 0), (0, 7), (0, ck - ds * kvw)))
    feat_major = lambda c: c[0].transpose(0, 2, 3, 1).reshape(-1, kvw, PAGE)
    kcb_s = _compress(feat_major(cache_cmp_k), page_table, tail_of(s_kc), *cmp_k)
    vcb_s = _compress(feat_major(cache_cmp_v), page_table, tail_of(s_vc), *cmp_v)
    tq_s = 16
    nq_pad = pad_rows(zsb.reshape(db, ds, ZP)[:, :, C_NQ:C_NQ + 1024], tq_s).reshape(db * tq_s, 1024)
    nsel_s = -(-(past + ds) // SEL_BLOCK)
    oc_s, sel_s, _ = _cmp_topk(nq_pad, 0, kcb_s, vcb_s, nsa_sl, batch=db, sq=tq_s, tq=tq_s,
                               nblk=(past + ds + CMP_STRIDE - 1) // CMP_STRIDE - 1, nsel=nsel_s, qpos_base=past)
    oc_s = oc_s.reshape(db, tq_s, 1024)[:, :ds].reshape(ms, 1024)

    nq = z3[:, :, C_NQ:C_NQ + 1024].reshape(db, ds, NSA_GROUPS, NSA_HPG, NSA_HD)
    wq_n = jnp.einsum('bqghd,gk->bghqkd', nq, jnp.eye(NSA_GROUPS, dtype=F32))
    wq_n = wq_n.reshape(db, NSA_HEADS * ds, kvw).astype(BF16)
    cn = np.zeros((NSA_HEADS * ds, LANE), np.float32)
    cn[:, 0] = np.repeat(_alibi(NSA_HEADS), ds)
    cn[:, 1] = np.tile(np.arange(ds), NSA_HEADS)
    cn = jnp.asarray(cn)
    sel_cols = jnp.repeat(sel_s[:, :, None, :ds, :], NSA_HPG, axis=2).reshape(db, NSA_HEADS * ds, -1)
    s_ks, s_vs = z3[:, :, C_KS:C_KS + kvw], z3[:, :, C_VS:C_VS + kvw]
    s_kw, s_vw = z3[:, :, C_KW:C_KW + kvw], z3[:, :, C_VW:C_VW + kvw]
    o_sel = _decode_attn(wq_n, cn, feat_major(cache_sel_k), feat_major(cache_sel_v),
                         *_selected_pages(page_table, sel_cols), pad_rows(s_ks, PAGE), pad_rows(s_vs, PAGE), sel_cols,
                         hk=1, qpos0=past, win=1e9, n_new=ds, pp=16, feature_major=True)
    win_pages = wb // PAGE
    win_pt = jnp.arange(db * win_pages, dtype=jnp.int32).reshape(db, win_pages)
    o_win = _decode_attn(wq_n, cn, cache_win_k[0].reshape(-1, PAGE, kvw), cache_win_v[0].reshape(-1, PAGE, kvw),
                         *_all_pages(win_pt, past - wb), pad_rows(s_kw, PAGE), pad_rows(s_vw, PAGE), None,
                         hk=1, qpos0=past, win=WINDOW, n_new=ds, pp=win_pages)

    def own_group(o):
        o = o.reshape(db, NSA_GROUPS, NSA_HPG, ds, NSA_GROUPS, NSA_HD)
        o = jnp.stack([o[:, g, :, :, g] for g in range(NSA_GROUPS)], axis=1)
        return o.transpose(0, 3, 1, 2, 4).reshape(ms, NSA_HEADS * NSA_HD)

    o_nsa_s = _gate3(zs[:, C_NG:C_NG + LANE], oc_s, own_group(o_sel), own_group(o_win))
    s_mk = cache_mem_k[0].reshape(db, -1, X_W)
    s_mv = cache_mem_v[0].reshape(db, -1, X_W)
    y_sample = _finish(xs, o_da_s, o_nsa_s, zs, s_mk, s_mv, db, *tail_w)

    s5 = lambda a, h: a.reshape(1, db, a.shape[1], h, -1)
    new_win = lambda c, a: jnp.concatenate([c[0].reshape(db, wb, kvw), a], axis=1)[:, ds:]
    s_states = (s5(s_dk, DA_HEADS), s5(s_dv, DA_HEADS), s5(s_kc, NSA_GROUPS), s5(s_vc, NSA_GROUPS),
                s5(s_ks, NSA_GROUPS), s5(s_vs, NSA_GROUPS),
                s5(new_win(cache_win_k, s_kw), NSA_GROUPS), s5(new_win(cache_win_v, s_vw), NSA_GROUPS))

    return (y_prompt.reshape(1, seq, D_MODEL), y_sample.reshape(db, ds, D_MODEL)) + p_states + s_states
```

```python
import functools

import numpy as np
import jax
import jax.numpy as jnp
from jax import lax
from jax.experimental import pallas as pl
from jax.experimental.pallas import tpu as pltpu

F32 = jnp.float32
BF16 = jnp.bfloat16

D_MODEL = 1024
DA_HEADS = 8
DA_HD = 64
NSA_HEADS = 16
NSA_GROUPS = 2
NSA_HPG = NSA_HEADS // NSA_GROUPS
NSA_HD = 64
CMP_LEN = 32
CMP_STRIDE = 16
SEL_BLOCK = 64
SEL_TOP = 16
WINDOW = 512
X_HEADS = 4
X_HD = 64
X_W = X_HEADS * X_HD
EPS = 1e-6
NEG = -1e30
PICKED = -3e38
FORCE_BONUS = 1e6
LAM_INIT = 0.2
LANE = 128
PAGE = 128
VMEM_LIMIT = 56 * 1024 * 1024

C_DQ, C_DK, C_DV, C_NQ, C_GA, C_GB = 0, 1024, 2048, 3072, 4096, 5120
C_KC, C_VC, C_KS, C_VS, C_KW, C_VW, C_NG = 6144, 6272, 6400, 6528, 6656, 6784, 6912
ZP = 7040
ZP_TILE = 1408

NT_DIMS = (((1,), (1,)), ((), ()))
LOG2E = 1.4426950408889634
Q_SCALE = DA_HD ** -0.5 * LOG2E


def _cparams(sem):
    return pltpu.CompilerParams(dimension_semantics=sem, vmem_limit_bytes=VMEM_LIMIT)


def _smem_spec():
    return pl.BlockSpec(memory_space=pltpu.SMEM)


def _pages_per_step(n_pages, cap):
    return max(p for p in range(1, cap + 1) if n_pages % p == 0)


def _sigmoid(x):
    return 1.0 / (1.0 + jnp.exp(-x))


def _rms(x, g):
    return x * lax.rsqrt(jnp.mean(x * x, axis=-1, keepdims=True) + EPS) * g


def _inproj_kernel(x_ref, g_ref, w_ref, o_ref, ob_ref, xn_ref):
    @pl.when(pl.program_id(1) == 0)
    def _():
        xn_ref[...] = _rms(x_ref[...], g_ref[...]).astype(BF16)

    acc = jnp.dot(xn_ref[...], w_ref[...], preferred_element_type=F32)
    o_ref[...] = acc
    ob_ref[...] = acc.astype(BF16)


def _inproj(x, g, w_pad):
    m = x.shape[0]
    tm = min(m, 1024)
    return pl.pallas_call(
        _inproj_kernel,
        grid=(m // tm, ZP // ZP_TILE),
        in_specs=[pl.BlockSpec((tm, D_MODEL), lambda i, j: (i, 0)),
                  pl.BlockSpec((1, D_MODEL), lambda i, j: (0, 0)),
                  pl.BlockSpec((D_MODEL, ZP_TILE), lambda i, j: (0, j))],
        out_specs=[pl.BlockSpec((tm, ZP_TILE), lambda i, j: (i, j)),
                   pl.BlockSpec((tm, ZP_TILE), lambda i, j: (i, j))],
        out_shape=[jax.ShapeDtypeStruct((m, ZP), F32), jax.ShapeDtypeStruct((m, ZP), BF16)],
        scratch_shapes=[pltpu.VMEM((tm, D_MODEL), BF16)],
        compiler_params=_cparams(("parallel", "arbitrary")),
        name="inproj",
    )(x, g.reshape(1, D_MODEL), w_pad)


def _prep_w_in(w_in):
    a = jnp.concatenate([w_in[:, :1024] * Q_SCALE, w_in[:, 1024:3072], w_in[:, 3072:4096] * Q_SCALE], axis=1)
    kv = w_in[:, 4096:4864]
    ng = w_in[:, 4864:4912]
    mg = w_in[:, 4912:]
    return jnp.concatenate([a, mg, kv, jnp.pad(ng, ((0, 0), (0, LANE - ng.shape[1])))], axis=1).astype(BF16)


def _matmul_kernel(x_ref, w_ref, o_ref):
    o_ref[...] = jnp.dot(x_ref[...].astype(BF16), w_ref[...], preferred_element_type=F32)


def _matmul(x, w_bf16):
    m, n = x.shape[0], w_bf16.shape[1]
    return pl.pallas_call(
        _matmul_kernel,
        out_shape=jax.ShapeDtypeStruct((m, n), F32),
        compiler_params=pltpu.CompilerParams(vmem_limit_bytes=VMEM_LIMIT),
        name="matmul",
    )(x, w_bf16)


def _da_post(o0, o1, lam, sub):
    o = o0 - lam * o1
    return _rms(o, sub) * (1.0 - LAM_INIT)


def _split3(x):
    a = x.astype(BF16).astype(F32)
    r = x - a
    b = r.astype(BF16).astype(F32)
    return a, b, (r - b).astype(BF16).astype(F32)


def _thirds(lane, pieces):
    which = lane % 3
    return jnp.where(which == 0, pieces[0], jnp.where(which == 1, pieces[1], pieces[2]))


def _da_prompt_kernel(sc_ref, q_ref, k_ref, v_ref, sub_ref, o_ref, ka_ref, va_ref, qa_ref, sa_ref, sb_ref, m_ref, acc_ref,
                      *, tq, tkb):
    h = pl.program_id(0)
    qi = pl.program_id(1)
    lam = sc_ref[0]
    slope = sc_ref[1 + h]
    per_big = tkb // tq
    seq = ka_ref.shape[0]

    @pl.when(qi == 0)
    def _():
        j = lax.broadcasted_iota(jnp.int32, (tkb, LANE), 0)
        grp = lax.broadcasted_iota(jnp.int32, (tkb, LANE), 1) // 3
        kfeat = jnp.where(grp == 0, (j // 32) * 32, jnp.where(grp == 1, j % 32, jnp.where(grp == 2, 1, 0)))
        kfeat = kfeat.astype(F32).astype(BF16)
        ka_ref[:, :LANE] = k_ref[...]
        for c in range(seq // tkb):
            ka_ref[c * tkb:(c + 1) * tkb, LANE:] = kfeat
        va_ref[:, :LANE] = v_ref[...]
        va_ref[:, LANE:] = jnp.ones((seq, LANE), BF16)
        lane2 = lax.broadcasted_iota(jnp.int32, (2 * tq, LANE), 1)
        i_loc = (lax.broadcasted_iota(jnp.int32, (2 * tq, LANE), 0) & (tq - 1)).astype(F32)
        sl_pieces = _thirds(lane2, _split3(jnp.full((2 * tq, LANE), slope, F32)))
        row_pieces = _thirds(lane2, _split3(-slope * i_loc))
        qfeat = jnp.where(lane2 < 6, sl_pieces, jnp.where(lane2 < 9, row_pieces, 0.0))
        qa_ref[:, LANE:] = qfeat.astype(BF16)

    q = q_ref[...]
    lane = lax.broadcasted_iota(jnp.int32, (tq, LANE), 1)
    zero = jnp.zeros_like(q)
    qa_ref[0:tq, :LANE] = jnp.where(lane < DA_HD, q, zero)
    qa_ref[tq:2 * tq, :LANE] = jnp.where(lane >= DA_HD, q, zero)
    m_ref[...] = jnp.full(m_ref.shape, NEG, F32)
    acc_ref[...] = jnp.zeros(acc_ref.shape, F32)

    def scores(idx):
        k = ka_ref[pl.ds(pl.multiple_of(idx * tkb, tkb), tkb), :]
        return lax.dot_general(qa_ref[...], k, NT_DIMS, preferred_element_type=F32)

    def consume(idx, s_ref, masked):
        base = qi * tq - idx * tkb
        cc = slope * base.astype(F32)
        va = va_ref[pl.ds(pl.multiple_of(idx * tkb, tkb), tkb), :]
        m_old = m_ref[...]
        if masked:
            ii = lax.broadcasted_iota(jnp.int32, s_ref.shape, 0) & (tq - 1)
            jj = lax.broadcasted_iota(jnp.int32, s_ref.shape, 1)
            t = jnp.where(jj - ii <= base, s_ref[...], NEG)
            m_new = jnp.maximum(m_old, jnp.max(t, axis=-1, keepdims=True) - cc)
            p = jnp.exp2(t - (m_new + cc))
        else:
            m_new = jnp.maximum(m_old, jnp.max(s_ref[...], axis=-1, keepdims=True) - cc)
            p = jnp.exp2(s_ref[...] - (m_new + cc))
        alpha = jnp.exp2(m_old - m_new)
        acc_ref[...] = alpha * acc_ref[...] + jnp.dot(p.astype(BF16), va, preferred_element_type=F32)
        m_ref[...] = m_new

    nbig = qi // per_big
    sa_ref[...] = scores(0)

    def run(first, count):
        bufs = (sa_ref, sb_ref)
        for i in range(count):
            bufs[(i + 1) % 2][...] = scores(first + i + 1)
            consume(first + i, bufs[i % 2], False)

    def quad(j, carry):
        run(4 * j, 4)
        return carry

    lax.fori_loop(0, nbig // 4, quad, 0)

    @pl.when(nbig % 4 >= 2)
    def _():
        run((nbig // 4) * 4, 2)

    @pl.when(nbig % 2 == 1)
    def _():
        sb_ref[...] = scores(nbig)
        consume(nbig - 1, sa_ref, False)
        consume(nbig, sb_ref, True)

    @pl.when(nbig % 2 == 0)
    def _():
        consume(nbig, sa_ref, True)

    o0 = acc_ref[0:tq, :LANE] / acc_ref[0:tq, LANE:]
    o1 = acc_ref[tq:2 * tq, :LANE] / acc_ref[tq:2 * tq, LANE:]
    o_ref[...] = _da_post(o0, o1, lam, sub_ref[...])


def _da_prompt(zb, sc, subln, tq=512, tkb=1024):
    s = zb.shape[0]
    tq = min(tq, s)
    tkb = min(tkb, s)
    kb, vb = C_DK // LANE, C_DV // LANE
    return pl.pallas_call(
        functools.partial(_da_prompt_kernel, tq=tq, tkb=tkb),
        grid=(DA_HEADS, s // tq),
        in_specs=[_smem_spec(),
                  pl.BlockSpec((tq, LANE), lambda h, i: (i, h)),
                  pl.BlockSpec((s, LANE), lambda h, i: (0, kb + h), pipeline_mode=pl.Buffered(1)),
                  pl.BlockSpec((s, LANE), lambda h, i: (0, vb + h), pipeline_mode=pl.Buffered(1)),
                  pl.BlockSpec((1, LANE), lambda h, i: (0, 0))],
        out_specs=pl.BlockSpec((tq, LANE), lambda h, i: (i, h)),
        out_shape=jax.ShapeDtypeStruct((s, DA_HEADS * LANE), F32),
        scratch_shapes=[pltpu.VMEM((s, 2 * LANE), BF16),
                        pltpu.VMEM((s, 2 * LANE), BF16),
                        pltpu.VMEM((2 * tq, 2 * LANE), BF16),
                        pltpu.VMEM((2 * tq, tkb), F32), pltpu.VMEM((2 * tq, tkb), F32),
                        pltpu.VMEM((2 * tq, 1), F32), pltpu.VMEM((2 * tq, 2 * LANE), F32)],
        compiler_params=_cparams(("arbitrary", "arbitrary")),
        name="da_prompt",
    )(sc, zb, zb, zb, subln.reshape(1, LANE))


def _compress_kernel(pt_ref, *refs, pp, nch):
    del pt_ref
    pages = refs[:pp]
    tail_ref, wbig_ref, w1_ref, pe_ref, w2_ref, out_ref, ab_ref, rows_ref, x_ref = refs[pp:]
    s = pl.program_id(1)
    for k in range(pp):
        rows_ref[k] = pages[k][0].T
    for k in range(pp):
        for pos in range(CMP_STRIDE):
            x_ref[8 * k:8 * (k + 1), LANE * pos:LANE * (pos + 1)] = rows_ref[k, pl.ds(pos, 8, stride=CMP_STRIDE), :]
    x = x_ref[...].astype(BF16)
    rows = 8 * pp
    ab_ref[pl.ds(pl.multiple_of(s * rows, rows), rows), :] = jnp.dot(x, wbig_ref[...], preferred_element_type=F32)

    @pl.when(s == pl.num_programs(1) - 1)
    def _():
        ab_ref[nch:nch + 8, :] = jnp.dot(tail_ref[0].astype(BF16), wbig_ref[...], preferred_element_type=F32)
        hpe = jnp.dot(pe_ref[...].astype(BF16), w1_ref[...], preferred_element_type=F32)[0:1]
        outs = []
        hid = 2 * NSA_HD
        for g in range(NSA_GROUPS):
            a = ab_ref[0:nch, 2 * hid * g:2 * hid * g + hid]
            b = ab_ref[1:nch + 1, 2 * hid * g + hid:2 * hid * (g + 1)]
            hd = a + b + hpe
            act = hd * _sigmoid(hd)
            outs.append(jnp.dot(act.astype(BF16), w2_ref[...], preferred_element_type=F32))
        out_ref[0] = jnp.concatenate(outs, axis=-1)


def _compress(pool, page_table, tail, w1, pe, w2):
    b, n_pages = page_table.shape
    nch = n_pages * 8
    pp = _pages_per_step(n_pages, 32)
    ck = CMP_STRIDE * NSA_GROUPS * NSA_HD
    hid = 2 * NSA_HD
    w1r = w1.reshape(2, CMP_STRIDE, NSA_HD, hid)
    wbig = jnp.einsum('psdh,gk->sgdkph', w1r, jnp.eye(NSA_GROUPS, dtype=F32)).reshape(ck, 2 * NSA_GROUPS * hid).astype(BF16)
    pe8 = jnp.pad(pe.reshape(1, CMP_LEN * NSA_HD), ((0, 7), (0, 0)))

    def page_spec(k):
        return pl.BlockSpec((1, LANE, PAGE), lambda bi, s, pt: (pt[bi * n_pages + s * pp + k], 0, 0))

    const = lambda shape: pl.BlockSpec(shape, lambda bi, s, pt: tuple(0 for _ in shape))
    grid_spec = pltpu.PrefetchScalarGridSpec(
        num_scalar_prefetch=1,
        grid=(b, n_pages // pp),
        in_specs=[page_spec(k) for k in range(pp)] + [
            pl.BlockSpec((1, 8, ck), lambda bi, s, pt: (bi, 0, 0)),
            const((ck, 2 * NSA_GROUPS * hid)), const((CMP_LEN * NSA_HD, hid)), const((8, CMP_LEN * NSA_HD)),
            const((hid, NSA_HD))],
        out_specs=pl.BlockSpec((1, nch, NSA_GROUPS * NSA_HD), lambda bi, s, pt: (bi, 0, 0)),
        scratch_shapes=[pltpu.VMEM((nch + 8, 2 * NSA_GROUPS * hid), F32), pltpu.VMEM((pp, PAGE, LANE), F32),
                        pltpu.VMEM((8 * pp, ck), F32)],
    )
    return pl.pallas_call(
        functools.partial(_compress_kernel, pp=pp, nch=nch),
        grid_spec=grid_spec,
        out_shape=jax.ShapeDtypeStruct((b, nch, NSA_GROUPS * NSA_HD), F32),
        compiler_params=_cparams(("arbitrary", "arbitrary")),
        name="compress",
    )(page_table.reshape(-1), *([pool] * pp), tail, wbig, w1.astype(BF16), pe8, w2.astype(BF16))


def _stack_group_queries(q, g, tq):
    lane = lax.broadcasted_iota(jnp.int32, (tq, 2 * NSA_HD), 1)
    mine = jnp.where(lane >= NSA_HD, 1, 0) == g
    parts = []
    for hh in range(NSA_HPG):
        qh = q[:, NSA_HD * hh:NSA_HD * (hh + 1)]
        parts.append(jnp.where(mine, jnp.concatenate([qh, qh], axis=1), jnp.zeros((tq, 2 * NSA_HD), q.dtype)))
    return jnp.concatenate(parts, axis=0)


def _group_half(x, g):
    return jnp.where(g == 0, x[:, :NSA_HD], x[:, NSA_HD:])


def _cmp_topk_kernel(sl_ref, q_ref, kc_ref, vc_ref, agg_ref, oc_ref, sel_ref, any_ref, *, tq, nb, nselp, qpos_base, topk,
                     levels):
    g = pl.program_id(1)
    t = pl.program_id(2)
    q0 = qpos_base + t * tq
    qpad = _stack_group_queries(q_ref[...], g, tq)

    def work(nbw, nsw):
        kcb = kc_ref[0, :nbw, :].astype(BF16)
        vcb = vc_ref[0, :nbw, :].astype(BF16)
        s_all = lax.dot_general(qpad, kcb, NT_DIMS, preferred_element_type=F32)
        i = lax.broadcasted_iota(jnp.int32, (tq, nbw), 0)
        n = lax.broadcasted_iota(jnp.int32, (tq, nbw), 1)
        dist = (q0 + i - (CMP_STRIDE * n + (CMP_LEN - 1))).astype(F32)
        mask = dist >= 0
        row_ok = dist[:, 0:1] >= 0.0
        psum = jnp.zeros((tq, nbw), F32)
        for hh in range(NSA_HPG):
            slope = sl_ref[NSA_HPG * g + hh]
            tt = jnp.where(mask, s_all[hh * tq:(hh + 1) * tq] - slope * dist, NEG)
            m = jnp.max(tt, axis=-1, keepdims=True)
            e = jnp.exp2(tt - m)
            l = jnp.sum(e, axis=-1, keepdims=True)
            p = e * jnp.where(row_ok, 1.0 / l, 0.0)
            psum = psum + p
            o = jnp.dot(p.astype(BF16), vcb, preferred_element_type=F32)
            oc_ref[:, NSA_HD * hh:NSA_HD * (hh + 1)] = _group_half(o, g)

        p_hi = psum.astype(BF16)
        p_lo = (psum - p_hi.astype(F32)).astype(BF16)
        agg = agg_ref[:nbw, :nsw]
        imp = jnp.dot(p_hi, agg, preferred_element_type=F32) + jnp.dot(p_lo, agg, preferred_element_type=F32)
        jj = lax.broadcasted_iota(jnp.int32, (tq, nsw), 1)
        cur = (q0 + lax.broadcasted_iota(jnp.int32, (tq, nsw), 0)) // SEL_BLOCK
        valid = jj <= cur
        forced = jnp.where(valid, jnp.where(jj == 0, 1, jnp.where(jj >= cur - 1, 1, 0)), 0)
        score = jnp.where(valid, imp + jnp.where(forced == 1, FORCE_BONUS, 0.0), NEG)

        tqp = -(-tq // LANE) * LANE
        if tqp > tq:
            score = jnp.concatenate([score, jnp.full((tqp - tq, nsw), NEG, F32)], axis=0)
        cand = lax.broadcasted_iota(jnp.int32, (nsw, tqp), 0)

        def pick(_, carry):
            sc, chosen = carry
            mx = jnp.max(sc, axis=0, keepdims=True)
            idx = jnp.min(jnp.where(sc == mx, cand, nsw), axis=0, keepdims=True)
            hit = cand == idx
            return jnp.where(hit, PICKED, sc), jnp.where(hit, 1.0, chosen)

        _, chosen = lax.fori_loop(0, topk, pick, (score.T, jnp.zeros((nsw, tqp), F32)))
        sel = jnp.where(valid, chosen.T[:tq], 0.0)
        if nsw < nselp:
            sel = jnp.concatenate([sel, jnp.zeros((tq, nselp - nsw), F32)], axis=1)
        sel_ref[0, 0] = sel
        any_ref[0, 0, 0] = jnp.broadcast_to(jnp.max(sel, axis=0, keepdims=True), (8, nselp))

    if levels == 1:
        work(nb, nselp)
    else:
        unit = nselp // levels
        need = (q0 + tq - 1) // SEL_BLOCK + 1
        lvl = (need + unit - 1) // unit
        for lv in range(1, levels + 1):
            @pl.when(lvl == lv)
            def _():
                work(nb * lv // levels, unit * lv)


def _sel_agg_matrix(nblk, nsel, nb, nselp):
    m = np.zeros((nb, nselp), np.float32)
    j = np.arange(nsel)
    r, c = SEL_BLOCK // CMP_STRIDE, CMP_LEN // CMP_STRIDE
    for a in range(r):
        for b in range(c):
            i = r * j + a - b
            ok = (i >= 0) & (i < nblk)
            np.add.at(m, (i[ok], j[ok]), 1.0)
    return jnp.asarray(m, dtype=BF16)


def _cmp_topk(qarr, qcol0, kcb, vcb, slopes, *, batch, sq, tq, nblk, nsel, qpos_base):
    nb = kcb.shape[1]
    nselp = -(-nsel // LANE) * LANE
    nt = sq // tq
    agg = _sel_agg_matrix(nblk, nsel, nb, nselp)
    gw = NSA_HPG * NSA_HD
    ratio = SEL_BLOCK // CMP_STRIDE
    prefix_ok = qpos_base == 0 and nb == ratio * nselp and sq >= SEL_BLOCK * nselp
    levels = max([lv for lv in (8, 4) if prefix_ok and nb % (lv * LANE) == 0], default=1)
    return pl.pallas_call(
        functools.partial(_cmp_topk_kernel, tq=tq, nb=nb, nselp=nselp, qpos_base=qpos_base, topk=min(SEL_TOP, nsel),
                          levels=levels),
        grid=(batch, NSA_GROUPS, nt),
        in_specs=[_smem_spec(),
                  pl.BlockSpec((tq, gw), lambda b, g, t: (b * nt + t, qcol0 + g)),
                  pl.BlockSpec((1, nb, LANE), lambda b, g, t: (b, 0, 0)),
                  pl.BlockSpec((1, nb, LANE), lambda b, g, t: (b, 0, 0)),
                  pl.BlockSpec((nb, nselp), lambda b, g, t: (0, 0))],
        out_specs=[pl.BlockSpec((tq, gw), lambda b, g, t: (b * nt + t, g)),
                   pl.BlockSpec((1, 1, tq, nselp), lambda b, g, t: (b, g, t, 0)),
                   pl.BlockSpec((1, 1, 1, 8, nselp), lambda b, g, t: (b, g, t, 0, 0))],
        out_shape=[jax.ShapeDtypeStruct((batch * sq, NSA_GROUPS * gw), F32),
                   jax.ShapeDtypeStruct((batch, NSA_GROUPS, sq, nselp), F32),
                   jax.ShapeDtypeStruct((batch, NSA_GROUPS, nt, 8, nselp), F32)],
        compiler_params=_cparams(("arbitrary", "arbitrary", "arbitrary")),
        name="cmp_topk",
    )(slopes, qarr, kcb, vcb, agg)


def _nsa_sw_kernel(fl_ref, sl_ref, q_ref, ks_ref, vs_ref, kw_ref, vw_ref, sel_ref, oc_ref, ng_ref, out_ref,
                   ids_ref, m_ref, acc_ref, *, tq, nt, nselp, nwords, sb):
    g = pl.program_id(0)
    t = pl.program_id(1)
    rows = NSA_HPG * tq
    qpad = _stack_group_queries(q_ref[...], g, tq)
    slope_row = jnp.concatenate([jnp.full((tq, 1), sl_ref[NSA_HPG * g + hh], F32) for hh in range(NSA_HPG)], axis=0)
    i_loc = lax.broadcasted_iota(jnp.int32, (rows, LANE), 0) & (tq - 1)
    j_loc = lax.broadcasted_iota(jnp.int32, (rows, LANE), 1)
    dloc = (i_loc - j_loc).astype(F32)
    sl_dloc = slope_row * dloc
    selb = sel_ref[0, 0].astype(BF16)
    ones = jnp.ones((LANE, LANE), BF16)

    def chunk(ref, c):
        return ref[pl.ds(pl.multiple_of(jnp.maximum(c, 0) * LANE, LANE), LANE), :]

    def with_ones(v):
        return jnp.concatenate([v, jnp.concatenate([ones] * (v.shape[0] // LANE), axis=0)], axis=1)

    def sel_mask(c):
        jrow = lax.broadcasted_iota(jnp.int32, (nselp, LANE), 0)
        r = lax.broadcasted_iota(jnp.int32, (nselp, LANE), 1)
        expand = jnp.where(jrow == 2 * c + r // SEL_BLOCK, 1.0, 0.0).astype(BF16)
        return jnp.dot(selb, expand, preferred_element_type=F32)

    def per_head(mk, x):
        x3 = x.reshape(NSA_HPG, tq, LANE)
        return jnp.where(mk[None] > 0.5, x3, NEG).reshape(rows, LANE)

    s = lax.dot_general(qpad, chunk(ks_ref, t), NT_DIMS, preferred_element_type=F32)
    mk = jnp.where(dloc[:tq] >= 0.0, sel_mask(t), 0.0)
    tt = per_head(mk, s - sl_dloc)
    m0 = jnp.max(tt, axis=-1, keepdims=True)
    p = jnp.exp2(tt - m0)
    m_ref[...] = m0
    acc_ref[...] = jnp.dot(p.astype(BF16), with_ones(chunk(vs_ref, t)), preferred_element_type=F32)

    def scan(c, cnt):
        word = fl_ref[(g * nt + t) * nwords + c // 32]
        bit = lax.shift_right_logical(word, c % 32) & 1

        @pl.when(bit == 1)
        def _():
            ids_ref[cnt] = c

        return cnt + bit

    cnt = lax.fori_loop(0, t, scan, 0)
    for k in range(sb):
        ids_ref[cnt + k] = -1

    def sel_step(si, carry):
        cs = [ids_ref[si * sb + k] for k in range(sb)]
        kk = jnp.concatenate([chunk(ks_ref, c) for c in cs], axis=0)
        vv = with_ones(jnp.concatenate([chunk(vs_ref, c) for c in cs], axis=0))
        keep = [sel_mask(c) > 0.5 for c in cs]
        offs = [((t - c) * tq).astype(F32) for c in cs]
        half = rows // 2
        for r0 in (0, half):
            rs = slice(r0, r0 + half)
            s = lax.dot_general(qpad[rs], kk, NT_DIMS, preferred_element_type=F32)
            slabs = []
            for k in range(sb):
                x = s[:, k * LANE:(k + 1) * LANE] - (sl_dloc[rs] + slope_row[rs] * offs[k])
                x = jnp.where(keep[k][None], x.reshape(NSA_HPG // 2, tq, LANE), NEG)
                slabs.append(x.reshape(half, LANE))
            tt = jnp.concatenate(slabs, axis=1)
            m_old = m_ref[rs, :]
            m_new = jnp.maximum(m_old, jnp.max(tt, axis=-1, keepdims=True))
            alpha = jnp.exp2(m_old - m_new)
            p = jnp.exp2(tt - m_new)
            acc_ref[rs, :] = alpha * acc_ref[rs, :] + jnp.dot(p.astype(BF16), vv, preferred_element_type=F32)
            m_ref[rs, :] = m_new
        return carry

    nsteps = (cnt + sb - 1) // sb

    def step_pair(j, carry):
        sel_step(2 * j, carry)
        return sel_step(2 * j + 1, carry)

    lax.fori_loop(0, nsteps // 2, step_pair, 0)

    @pl.when(nsteps % 2 == 1)
    def _():
        sel_step(nsteps - 1, 0)
    o_s = acc_ref[:, :LANE] / acc_ref[:, LANE:]

    nback = WINDOW // tq
    c0 = jnp.maximum(t - nback, 0)
    wlen = (nback + 1) * LANE
    wstart = pl.multiple_of(c0 * LANE, LANE)
    s = lax.dot_general(qpad, kw_ref[pl.ds(wstart, wlen), :], NT_DIMS, preferred_element_type=F32)
    slabs = []
    for k in range(nback + 1):
        dist = dloc + ((t - (c0 + k)) * tq).astype(F32)
        x = s[:, k * LANE:(k + 1) * LANE] - slope_row * dist
        slabs.append(jnp.where(dist >= 0.0, jnp.where(dist < float(WINDOW), x, NEG), NEG))
    tt = jnp.concatenate(slabs, axis=1)
    p = jnp.exp2(tt - jnp.max(tt, axis=-1, keepdims=True))
    aw = jnp.dot(p.astype(BF16), with_ones(vw_ref[pl.ds(wstart, wlen), :]), preferred_element_type=F32)
    o_w = aw[:, :LANE] / aw[:, LANE:]

    gate = _sigmoid(ng_ref[...])
    glane = lax.broadcasted_iota(jnp.int32, (tq, LANE), 1)

    def gate_col(idx):
        return jnp.sum(jnp.where(glane == idx, gate, 0.0), axis=-1, keepdims=True)

    for hh in range(NSA_HPG):
        base = 3 * (NSA_HPG * g + hh)
        r0, r1 = hh * tq, (hh + 1) * tq
        o = (gate_col(base) * oc_ref[:, NSA_HD * hh:NSA_HD * (hh + 1)]
             + gate_col(base + 1) * _group_half(o_s[r0:r1], g)
             + gate_col(base + 2) * _group_half(o_w[r0:r1], g))
        out_ref[:, NSA_HD * hh:NSA_HD * (hh + 1)] = o


def _pack_chunk_flags(anyblk):
    g, nt, nselp = anyblk.shape
    chunk = jnp.max(anyblk.reshape(g, nt, nselp // 2, 2), axis=-1) > 0.5
    nchunk = nselp // 2
    nwords = -(-nchunk // 32)
    chunk = jnp.pad(chunk, ((0, 0), (0, 0), (0, nwords * 32 - nchunk)))
    bits = chunk.reshape(g, nt, nwords, 32).astype(jnp.uint32) << jnp.arange(32, dtype=jnp.uint32)
    words = jnp.sum(bits, axis=-1, dtype=jnp.uint32)
    return lax.bitcast_convert_type(words, jnp.int32).reshape(-1), nwords


def _nsa_sw_prompt(z, zb, sel, anyblk, oc, slopes, tq=128, sb=4):
    s = zb.shape[0]
    assert tq == LANE and s >= WINDOW + tq
    nt = s // tq
    nselp = sel.shape[-1]
    flags, nwords = _pack_chunk_flags(anyblk[0, :, :, 0, :])
    gw = NSA_HPG * NSA_HD
    rows = NSA_HPG * tq
    res = lambda col: pl.BlockSpec((s, LANE), lambda g, t, fl: (0, col // LANE), pipeline_mode=pl.Buffered(1))
    grid_spec = pltpu.PrefetchScalarGridSpec(
        num_scalar_prefetch=1,
        grid=(NSA_GROUPS, nt),
        in_specs=[_smem_spec(),
                  pl.BlockSpec((tq, gw), lambda g, t, fl: (t, C_NQ // gw + g)),
                  res(C_KS), res(C_VS), res(C_KW), res(C_VW),
                  pl.BlockSpec((1, 1, tq, nselp), lambda g, t, fl: (0, g, t, 0)),
                  pl.BlockSpec((tq, gw), lambda g, t, fl: (t, g)),
                  pl.BlockSpec((tq, LANE), lambda g, t, fl: (t, C_NG // LANE))],
        out_specs=pl.BlockSpec((tq, gw), lambda g, t, fl: (t, g)),
        scratch_shapes=[pltpu.SMEM((nt + sb,), jnp.int32), pltpu.VMEM((rows, 1), F32),
                        pltpu.VMEM((rows, 2 * LANE), F32)],
    )
    return pl.pallas_call(
        functools.partial(_nsa_sw_kernel, tq=tq, nt=nt, nselp=nselp, nwords=nwords, sb=sb),
        grid_spec=grid_spec,
        out_shape=jax.ShapeDtypeStruct((s, NSA_GROUPS * gw), F32),
        compiler_params=_cparams(("arbitrary", "arbitrary")),
        name="nsa_sel_win",
    )(flags, slopes, zb, zb, zb, zb, zb, sel, oc, z)


def _decode_kernel(pt_ref, pos_ref, cnt_ref, *refs, pp, n_pages, ncols, hk, qpos0, win, n_new, has_sel, nselp,
                   feature_major):
    del pt_ref
    kpages = refs[:pp]
    vpages = refs[pp:2 * pp]
    rest = refs[2 * pp:]
    if has_sel:
        wq_ref, ci_ref, kn_ref, vn_ref, sel_ref, o_ref, m_ref, s_ref, acc_ref = rest
    else:
        wq_ref, ci_ref, kn_ref, vn_ref, o_ref, m_ref, s_ref, acc_ref = rest
        sel_ref = None
    bi = pl.program_id(0)
    st = pl.program_id(1)
    windowed = win < 1e8
    guarded = has_sel or windowed

    @pl.when(st == 0)
    def _():
        m_ref[...] = jnp.full(m_ref.shape, NEG, F32)
        s_ref[...] = jnp.zeros(s_ref.shape, F32)
        acc_ref[...] = jnp.zeros(acc_ref.shape, F32)

    wq = wq_ref[0]
    slope = ci_ref[:, 0:1]
    qrel = ci_ref[:, 1:2]
    colhead = ci_ref[:, 2:3]

    def geometry(nrows):
        r = lax.broadcasted_iota(jnp.int32, (ncols, nrows), 1)
        kidx = (r // hk).astype(F32)
        base = slope * kidx
        if hk > 1:
            base = jnp.where((r % hk).astype(F32) == colhead, base, NEG)
        return r, kidx, base

    def scores(k, kpos0, n_valid, geom, check_range, feature_major):
        r, kidx, base = geom
        if feature_major:
            s = jnp.dot(wq, k.astype(BF16), preferred_element_type=F32)
        else:
            s = lax.dot_general(wq, k.astype(BF16), NT_DIMS, preferred_element_type=F32)
        off = qrel + (qpos0 - kpos0).astype(F32)
        t = (s + base) - slope * off
        if not (check_range or has_sel or n_valid is not None):
            return t, None
        okf = jnp.ones(t.shape, F32)
        if check_range:
            dist = off - kidx
            okf = jnp.where(dist >= 0.0, jnp.where(dist < win, 1.0, 0.0), 0.0)
        if n_valid is not None:
            okf = jnp.where(r < n_valid, okf, 0.0)
        if has_sel:
            nrows = t.shape[1]
            jrow = lax.broadcasted_iota(jnp.int32, (nselp, nrows), 0)
            kp = kpos0 + lax.broadcasted_iota(jnp.int32, (nselp, nrows), 1)
            expand = jnp.where(jrow == kp // SEL_BLOCK, 1.0, 0.0).astype(BF16)
            okf = okf * jnp.dot(sel_ref[0].astype(BF16), expand, preferred_element_type=F32)
        ok = okf > 0.5
        return jnp.where(ok, t, NEG), ok

    def update(ts, oks, vs, feature_major):
        m_old = m_ref[...]
        m_new = m_old
        for t in ts:
            m_new = jnp.maximum(m_new, jnp.max(t, axis=-1, keepdims=True))
        alpha = jnp.exp2(m_old - m_new)
        l = alpha * s_ref[...]
        acc = alpha * acc_ref[...]
        for t, ok, v in zip(ts, oks, vs):
            p = jnp.exp2(t - m_new)
            if ok is not None:
                p = jnp.where(ok, p, 0.0)
            l = l + jnp.sum(p, axis=-1, keepdims=True)
            if feature_major:
                acc = acc + lax.dot_general(p.astype(BF16), v.astype(BF16), NT_DIMS, preferred_element_type=F32)
            else:
                acc = acc + jnp.dot(p.astype(BF16), v.astype(BF16), preferred_element_type=F32)
        s_ref[...] = l
        acc_ref[...] = acc
        m_ref[...] = m_new

    cnt = cnt_ref[bi]

    @pl.when(st * pp < cnt)
    def _():
        geom = geometry(PAGE * hk)
        ts, oks, vs = [], [], []
        for k in range(pp):
            slot = st * pp + k
            n_valid = jnp.where(slot < cnt, PAGE * hk, 0) if guarded else None
            t, ok = scores(kpages[k][0], pos_ref[bi * n_pages + slot], n_valid, geom, windowed, feature_major)
            ts.append(t)
            oks.append(ok)
            vs.append(vpages[k][0])
        update(ts, oks, vs, feature_major)

    @pl.when(st == pl.num_programs(1) - 1)
    def _():
        t, ok = scores(kn_ref[0], jnp.int32(qpos0), n_new * hk, geometry(PAGE), True, False)
        update([t], [ok], [vn_ref[0]], False)
        l = s_ref[...]
        o_ref[0] = acc_ref[...] / jnp.where(l == 0.0, 1.0, l)


def _decode_attn(wq, colinfo, pool_k, pool_v, page_ids, page_pos, page_cnt, knew, vnew, sel, *, hk, qpos0, win, n_new, pp,
                 feature_major=False):
    b, n_pages = page_ids.shape
    ncols, width = wq.shape[1], wq.shape[2]
    pp = _pages_per_step(n_pages, pp)
    has_sel = sel is not None
    nselp = sel.shape[-1] if has_sel else 0

    def page_spec(k):
        return pl.BlockSpec((1, width, PAGE) if feature_major else (1, PAGE * hk, width),
                            lambda bi, s, pt, pos, cnt: (pt[bi * n_pages + s * pp + k], 0, 0))

    per_batch = lambda shape: pl.BlockSpec((1,) + shape, lambda bi, s, pt, pos, cnt: (bi, 0, 0))
    in_specs = ([page_spec(k) for k in range(pp)] * 2
                + [per_batch((ncols, width)), pl.BlockSpec((ncols, LANE), lambda bi, s, pt, pos, cnt: (0, 0)),
                   per_batch((PAGE, width)), per_batch((PAGE, width))])
    args = [pool_k] * pp + [pool_v] * pp + [wq, colinfo, knew, vnew]
    if has_sel:
        in_specs.append(per_batch((ncols, nselp)))
        args.append(sel)
    grid_spec = pltpu.PrefetchScalarGridSpec(
        num_scalar_prefetch=3,
        grid=(b, n_pages // pp),
        in_specs=in_specs,
        out_specs=per_batch((ncols, width)),
        scratch_shapes=[pltpu.VMEM((ncols, 1), F32), pltpu.VMEM((ncols, 1), F32), pltpu.VMEM((ncols, width), F32)],
    )
    return pl.pallas_call(
        functools.partial(_decode_kernel, pp=pp, n_pages=n_pages, ncols=ncols, hk=hk, qpos0=qpos0, win=float(win),
                          n_new=n_new, has_sel=has_sel, nselp=nselp, feature_major=feature_major),
        grid_spec=grid_spec,
        out_shape=jax.ShapeDtypeStruct((b, ncols, width), F32),
        compiler_params=_cparams(("arbitrary", "arbitrary")),
        name="decode_attn",
    )(page_ids.reshape(-1), page_pos.reshape(-1), page_cnt, *args)


def _all_pages(page_table, pos0):
    b, n_pages = page_table.shape
    pos = jnp.broadcast_to(pos0 + PAGE * jnp.arange(n_pages, dtype=jnp.int32), (b, n_pages))
    return page_table, pos, jnp.full((b,), n_pages, jnp.int32)


def _selected_pages(page_table, sel_cols):
    b, n_pages = page_table.shape
    per_page = PAGE // SEL_BLOCK
    hit = jnp.max(sel_cols[:, :, :per_page * n_pages].reshape(b, -1, n_pages, per_page), axis=(1, 3)) > 0.5
    cnt = jnp.sum(hit, axis=1).astype(jnp.int32)
    order = jnp.argsort(jnp.logical_not(hit), axis=1, stable=True).astype(jnp.int32)
    keep = jnp.minimum(jnp.arange(n_pages, dtype=jnp.int32)[None], jnp.maximum(cnt - 1, 0)[:, None])
    order = jnp.take_along_axis(order, keep, axis=1)
    return jnp.take_along_axis(page_table, order, axis=1), order * PAGE, cnt


def _da_post_kernel(sc_ref, o0_ref, o1_ref, sub_ref, o_ref):
    o_ref[...] = _da_post(o0_ref[...], o1_ref[...], sc_ref[0], sub_ref[...])


def _da_post_call(o0, o1, sc, subln):
    return pl.pallas_call(
        _da_post_kernel,
        in_specs=[_smem_spec(), pl.BlockSpec(o0.shape, lambda: (0, 0)), pl.BlockSpec(o0.shape, lambda: (0, 0)),
                  pl.BlockSpec((1, LANE), lambda: (0, 0))],
        out_specs=pl.BlockSpec(o0.shape, lambda: (0, 0)),
        out_shape=jax.ShapeDtypeStruct(o0.shape, F32),
        name="da_post",
    )(sc, o0, o1, subln.reshape(1, LANE))


def _gate3_kernel(ng_ref, e_ref, oc_ref, os_ref, ow_ref, o_ref):
    gate = _sigmoid(ng_ref[...])
    acc = jnp.zeros(o_ref.shape, F32)
    for br, ref in enumerate((oc_ref, os_ref, ow_ref)):
        ge = jnp.dot(gate, e_ref[br], preferred_element_type=F32, precision=lax.Precision.HIGHEST)
        acc = acc + ge * ref[...]
    o_ref[...] = acc


def _gate3(ng, oc, os_, ow):
    e = np.zeros((3, LANE, NSA_HEADS * NSA_HD), np.float32)
    for br in range(3):
        for h in range(NSA_HEADS):
            e[br, 3 * h + br, NSA_HD * h:NSA_HD * (h + 1)] = 1.0
    return pl.pallas_call(
        _gate3_kernel,
        out_shape=jax.ShapeDtypeStruct(oc.shape, F32),
        name="gate3",
    )(ng, jnp.asarray(e), oc, os_, ow)


def _tail_a_kernel(x_ref, da_ref, nsa_ref, ga_ref, gb_ref, wo_ref, nx_ref, wxq_ref, h_ref, q_ref):
    m = _sigmoid(ga_ref[...]) * da_ref[...] + _sigmoid(gb_ref[...]) * nsa_ref[...]
    h = x_ref[...] + jnp.dot(m.astype(BF16), wo_ref[...], preferred_element_type=F32)
    h_ref[...] = h
    xn = _rms(h, nx_ref[...]).astype(BF16)
    q_ref[...] = jnp.dot(xn, wxq_ref[...], preferred_element_type=F32)


def _tail_a(x, o_da, o_nsa, z, w_o, norm_x, w_xq):
    m = x.shape[0]
    tm = min(m, 512)
    row = lambda cb: pl.BlockSpec((tm, D_MODEL), lambda i: (i, cb))
    const = lambda shape: pl.BlockSpec(shape, lambda i: (0, 0), pipeline_mode=pl.Buffered(1))
    return pl.pallas_call(
        _tail_a_kernel,
        grid=(m // tm,),
        in_specs=[row(0), row(0), row(0), row(C_GA // D_MODEL), row(C_GB // D_MODEL),
                  const((D_MODEL, D_MODEL)), const((1, D_MODEL)), const((D_MODEL, X_W))],
        out_specs=[row(0), pl.BlockSpec((tm, X_W), lambda i: (i, 0))],
        out_shape=[jax.ShapeDtypeStruct((m, D_MODEL), F32), jax.ShapeDtypeStruct((m, X_W), F32)],
        compiler_params=_cparams(("parallel",)),
        name="tail_merge_wo",
    )(x, o_da, o_nsa, z, z, w_o.astype(BF16), norm_x.reshape(1, D_MODEL), w_xq.astype(BF16))


def _cross_kernel(q_ref, mk_ref, mv_ref, o_ref):
    q = (q_ref[0] * (X_HD ** -0.5)).astype(BF16)
    mk = mk_ref[0].astype(BF16)
    mv = mv_ref[0].astype(BF16)
    outs = []
    for h in range(X_HEADS):
        sl = slice(X_HD * h, X_HD * (h + 1))
        s = lax.dot_general(q[:, sl], mk[:, sl], NT_DIMS, preferred_element_type=F32)
        e = jnp.exp(s - jnp.max(s, axis=-1, keepdims=True))
        p = e / jnp.sum(e, axis=-1, keepdims=True)
        outs.append(jnp.dot(p.astype(BF16), mv[:, sl], preferred_element_type=F32))
    o_ref[0] = jnp.concatenate(outs, axis=-1)


def _cross(q, mk, mv):
    b, t, _ = q.shape
    mlen = mk.shape[1]
    tt = min(t, 512)
    return pl.pallas_call(
        _cross_kernel,
        grid=(b, t // tt),
        in_specs=[pl.BlockSpec((1, tt, X_W), lambda bi, i: (bi, i, 0)),
                  pl.BlockSpec((1, mlen, X_W), lambda bi, i: (bi, 0, 0)),
                  pl.BlockSpec((1, mlen, X_W), lambda bi, i: (bi, 0, 0))],
        out_specs=pl.BlockSpec((1, tt, X_W), lambda bi, i: (bi, i, 0)),
        out_shape=jax.ShapeDtypeStruct((b, t, X_W), F32),
        compiler_params=_cparams(("parallel", "parallel")),
        name="cross_attn",
    )(q, mk, mv)


def _tail_c_kernel(h_ref, ox_ref, wxo_ref, nf_ref, wg_ref, wu_ref, wd_ref, nfin_ref, y_ref, *, nchunk, chunk):
    h = h_ref[...] + jnp.dot(ox_ref[...].astype(BF16), wxo_ref[...], preferred_element_type=F32)
    xn = _rms(h, nf_ref[...]).astype(BF16)
    acc = jnp.zeros(h.shape, F32)
    for c in range(nchunk):
        sl = slice(c * chunk, (c + 1) * chunk)
        gt = jnp.dot(xn, wg_ref[:, sl], preferred_element_type=F32)
        up = jnp.dot(xn, wu_ref[:, sl], preferred_element_type=F32)
        act = (gt * _sigmoid(gt) * up).astype(BF16)
        acc = acc + jnp.dot(act, wd_ref[sl, :], preferred_element_type=F32)
    y_ref[...] = _rms(h + acc, nfin_ref[...])


def _tail_c(h, ox, w_xo, norm_ffn, w_gate_up, w_down, norm_final):
    m = h.shape[0]
    tm = min(m, 512)
    hid = w_down.shape[0]
    chunk = hid // 2
    const = lambda shape: pl.BlockSpec(shape, lambda i: (0, 0), pipeline_mode=pl.Buffered(1))
    return pl.pallas_call(
        functools.partial(_tail_c_kernel, nchunk=2, chunk=chunk),
        grid=(m // tm,),
        in_specs=[pl.BlockSpec((tm, D_MODEL), lambda i: (i, 0)), pl.BlockSpec((tm, X_W), lambda i: (i, 0)),
                  const((X_W, D_MODEL)), const((1, D_MODEL)), const((D_MODEL, hid)), const((D_MODEL, hid)),
                  const((hid, D_MODEL)), const((1, D_MODEL))],
        out_specs=pl.BlockSpec((tm, D_MODEL), lambda i: (i, 0)),
        out_shape=jax.ShapeDtypeStruct((m, D_MODEL), F32),
        compiler_params=_cparams(("parallel",)),
        name="tail_ffn",
    )(h, ox, w_xo.astype(BF16), norm_ffn.reshape(1, D_MODEL), w_gate_up[:, :hid].astype(BF16),
      w_gate_up[:, hid:].astype(BF16), w_down.astype(BF16), norm_final.reshape(1, D_MODEL))


def _alibi(n):
    return np.asarray(2.0 ** (-8.0 * np.arange(1, n + 1) / n) * LOG2E, dtype=np.float32)


def _finish(x, o_da, o_nsa, z, mk, mv, batch, w_o, norm_x, w_xq, w_xo, norm_ffn, w_gate_up, w_down, norm_final):
    m = x.shape[0]
    t = m // batch
    h1, qx = _tail_a(x, o_da, o_nsa, z, w_o, norm_x, w_xq)
    q3 = qx.reshape(batch, t, X_W)
    tpad = -(-t // 8) * 8
    if tpad != t:
        q3 = jnp.pad(q3, ((0, 0), (0, tpad - t), (0, 0)))
    ox = _cross(q3, mk, mv)[:, :t].reshape(m, X_W)
    return _tail_c(h1, ox, w_xo, norm_ffn, w_gate_up, w_down, norm_final)


def kernel(x_prompt, x_sample, cache_diff_k, cache_diff_v, cache_cmp_k, cache_cmp_v, cache_sel_k, cache_sel_v,
           cache_win_k, cache_win_v, cache_mem_k, cache_mem_v, page_table, mem_prompt,
           norm_mix, w_in, lam_q1, lam_k1, lam_q2, lam_k2, da_subln,
           w_cmp_k1, pe_cmp_k, w_cmp_k2, w_cmp_v1, pe_cmp_v, w_cmp_v2,
           w_o, norm_x, w_xq, w_mem_kv, w_xo, norm_ffn, w_gate_up, w_down, norm_final):
    batch, seq, _ = x_prompt.shape
    db, ds, _ = x_sample.shape
    assert batch == 1 and norm_mix.shape[0] == 1
    n_pages = page_table.shape[1]
    past = n_pages * PAGE
    wb = cache_win_k.shape[2]
    kvw = NSA_GROUPS * NSA_HD

    lam = (jnp.exp(jnp.sum(lam_q1[0] * lam_k1[0])) - jnp.exp(jnp.sum(lam_q2[0] * lam_k2[0])) + LAM_INIT).astype(F32)
    da_sc = jnp.concatenate([lam.reshape(1), jnp.asarray(_alibi(DA_HEADS))])
    nsa_sl = jnp.asarray(_alibi(NSA_HEADS))
    w_pad = _prep_w_in(w_in[0])
    tail_w = (w_o[0], norm_x[0], w_xq[0], w_xo[0], norm_ffn[0], w_gate_up[0], w_down[0], norm_final)
    cmp_k = (w_cmp_k1[0], pe_cmp_k[0], w_cmp_k2[0])
    cmp_v = (w_cmp_v1[0], pe_cmp_v[0], w_cmp_v2[0])

    xp = x_prompt.reshape(seq, D_MODEL)
    z, zb = _inproj(xp, norm_mix[0], w_pad)
    o_da = _da_prompt(zb, da_sc, da_subln[0])

    ident = jnp.arange(seq // PAGE, dtype=jnp.int32).reshape(1, -1)
    zero_tail = jnp.zeros((1, 8, CMP_STRIDE * kvw), F32)
    p_kc = z[:, C_KC:C_KC + kvw]
    p_vc = z[:, C_VC:C_VC + kvw]
    as_pages = lambda a: a.reshape(seq // PAGE, PAGE, kvw).transpose(0, 2, 1)
    kcb = _compress(as_pages(p_kc), ident, zero_tail, *cmp_k)
    vcb = _compress(as_pages(p_vc), ident, zero_tail, *cmp_v)
    nch = seq // CMP_STRIDE
    oc, sel, anyblk = _cmp_topk(zb, C_NQ // (NSA_HPG * NSA_HD), kcb, vcb, nsa_sl, batch=1, sq=seq, tq=128,
                                nblk=nch - 1, nsel=seq // SEL_BLOCK, qpos_base=0)
    o_nsa = _nsa_sw_prompt(z, zb, sel, anyblk, oc, nsa_sl)

    mem_kv = _matmul(mem_prompt.reshape(-1, D_MODEL), w_mem_kv[0].astype(BF16))
    p_mk, p_mv = mem_kv[:, :X_W], mem_kv[:, X_W:]
    y_prompt = _finish(xp, o_da, o_nsa, z, p_mk[None], p_mv[None], 1, *tail_w)

    r5 = lambda a, h: a.reshape(1, 1, a.shape[0], h, -1)
    p_states = (r5(z[:, C_DK:C_DK + 1024], DA_HEADS), r5(z[:, C_DV:C_DV + 1024], DA_HEADS),
                r5(p_kc, NSA_GROUPS), r5(p_vc, NSA_GROUPS),
                r5(z[:, C_KS:C_KS + kvw], NSA_GROUPS), r5(z[:, C_VS:C_VS + kvw], NSA_GROUPS),
                r5(z[seq - min(WINDOW, seq):, C_KW:C_KW + kvw], NSA_GROUPS),
                r5(z[seq - min(WINDOW, seq):, C_VW:C_VW + kvw], NSA_GROUPS),
                r5(p_mk, X_HEADS), r5(p_mv, X_HEADS))

    ms = db * ds
    xs = x_sample.reshape(ms, D_MODEL)
    zs, zsb = _inproj(xs, norm_mix[0], w_pad)
    z3 = zs.reshape(db, ds, ZP)
    pad_rows = lambda a, n: jnp.pad(a, ((0, 0), (0, n - a.shape[1]), (0, 0)))

    dq = z3[:, :, C_DQ:C_DQ + 1024].reshape(db, ds, DA_HEADS, 2, DA_HD)
    wq_da = jnp.einsum('bqhmd,mn->bhmqnd', dq, jnp.eye(2, dtype=F32))
    wq_da = wq_da.reshape(db, DA_HEADS * 2 * ds, 2 * DA_HD).astype(BF16)
    ci = np.zeros((DA_HEADS * 2 * ds, LANE), np.float32)
    ci[:, 0] = np.repeat(_alibi(DA_HEADS), 2 * ds)
    ci[:, 1] = np.tile(np.arange(ds), DA_HEADS * 2)
    ci[:, 2] = np.repeat(np.arange(DA_HEADS), 2 * ds)
    s_dk, s_dv = z3[:, :, C_DK:C_DK + 1024], z3[:, :, C_DV:C_DV + 1024]
    head_rows = lambda a: pad_rows(a.reshape(db, ds * DA_HEADS, 2 * DA_HD), PAGE)
    o_pair = _decode_attn(wq_da, jnp.asarray(ci), cache_diff_k[0].reshape(-1, PAGE * DA_HEADS, 2 * DA_HD),
                          cache_diff_v[0].reshape(-1, PAGE * DA_HEADS, 2 * DA_HD), *_all_pages(page_table, 0),
                          head_rows(s_dk), head_rows(s_dv), None, hk=DA_HEADS, qpos0=past, win=1e9, n_new=ds, pp=16)
    o_pair = o_pair.reshape(db, DA_HEADS, 2, ds, 2 * DA_HD).transpose(2, 0, 3, 1, 4)
    o_da_s = _da_post_call(o_pair[0].reshape(ms * DA_HEADS, LANE), o_pair[1].reshape(ms * DA_HEADS, LANE),
                           da_sc, da_subln[0]).reshape(ms, DA_HEADS * LANE)

    ck = CMP_STRIDE * kvw
    s_kc, s_vc = z3[:, :, C_KC:C_KC + kvw], z3[:, :, C_VC:C_VC + kvw]
    tail_of = lambda a: jnp.pad(a.reshape(db, 1, ds * kvw), ((0, 0), (0, 7), (0, ck - ds * kvw)))
    feat_major = lambda c: c[0].transpose(0, 2, 3, 1).reshape(-1, kvw, PAGE)
    kcb_s = _compress(feat_major(cache_cmp_k), page_table, tail_of(s_kc), *cmp_k)
    vcb_s = _compress(feat_major(cache_cmp_v), page_table, tail_of(s_vc), *cmp_v)
    tq_s = 16
    nq_pad = pad_rows(zsb.reshape(db, ds, ZP)[:, :, C_NQ:C_NQ + 1024], tq_s).reshape(db * tq_s, 1024)
    nsel_s = -(-(past + ds) // SEL_BLOCK)
    oc_s, sel_s, _ = _cmp_topk(nq_pad, 0, kcb_s, vcb_s, nsa_sl, batch=db, sq=tq_s, tq=tq_s,
                               nblk=(past + ds + CMP_STRIDE - 1) // CMP_STRIDE - 1, nsel=nsel_s, qpos_base=past)
    oc_s = oc_s.reshape(db, tq_s, 1024)[:, :ds].reshape(ms, 1024)

    nq = z3[:, :, C_NQ:C_NQ + 1024].reshape(db, ds, NSA_GROUPS, NSA_HPG, NSA_HD)
    wq_n = jnp.einsum('bqghd,gk->bghqkd', nq, jnp.eye(NSA_GROUPS, dtype=F32))
    wq_n = wq_n.reshape(db, NSA_HEADS * ds, kvw).astype(BF16)
    cn = np.zeros((NSA_HEADS * ds, LANE), np.float32)
    cn[:, 0] = np.repeat(_alibi(NSA_HEADS), ds)
    cn[:, 1] = np.tile(np.arange(ds), NSA_HEADS)
    cn = jnp.asarray(cn)
    sel_cols = jnp.repeat(sel_s[:, :, None, :ds, :], NSA_HPG, axis=2).reshape(db, NSA_HEADS * ds, -1)
    s_ks, s_vs = z3[:, :, C_KS:C_KS + kvw], z3[:, :, C_VS:C_VS + kvw]
    s_kw, s_vw = z3[:, :, C_KW:C_KW + kvw], z3[:, :, C_VW:C_VW + kvw]
    o_sel = _decode_attn(wq_n, cn, feat_major(cache_sel_k), feat_major(cache_sel_v),
                         *_selected_pages(page_table, sel_cols), pad_rows(s_ks, PAGE), pad_rows(s_vs, PAGE), sel_cols,
                         hk=1, qpos0=past, win=1e9, n_new=ds, pp=16, feature_major=True)
    win_pages = wb // PAGE
    win_pt = jnp.arange(db * win_pages, dtype=jnp.int32).reshape(db, win_pages)
    o_win = _decode_attn(wq_n, cn, cache_win_k[0].reshape(-1, PAGE, kvw), cache_win_v[0].reshape(-1, PAGE, kvw),
                         *_all_pages(win_pt, past - wb), pad_rows(s_kw, PAGE), pad_rows(s_vw, PAGE), None,
                         hk=1, qpos0=past, win=WINDOW, n_new=ds, pp=win_pages)

    def own_group(o):
        o = o.reshape(db, NSA_GROUPS, NSA_HPG, ds, NSA_GROUPS, NSA_HD)
        o = jnp.stack([o[:, g, :, :, g] for g in range(NSA_GROUPS)], axis=1)
        return o.transpose(0, 3, 1, 2, 4).reshape(ms, NSA_HEADS * NSA_HD)

    o_nsa_s = _gate3(zs[:, C_NG:C_NG + LANE], oc_s, own_group(o_sel), own_group(o_win))
    s_mk = cache_mem_k[0].reshape(db, -1, X_W)
    s_mv = cache_mem_v[0].reshape(db, -1, X_W)
    y_sample = _finish(xs, o_da_s, o_nsa_s, zs, s_mk, s_mv, db, *tail_w)

    s5 = lambda a, h: a.reshape(1, db, a.shape[1], h, -1)
    new_win = lambda c, a: jnp.concatenate([c[0].reshape(db, wb, kvw), a], axis=1)[:, ds:]
    s_states = (s5(s_dk, DA_HEADS), s5(s_dv, DA_HEADS), s5(s_kc, NSA_GROUPS), s5(s_vc, NSA_GROUPS),
                s5(s_ks, NSA_GROUPS), s5(s_vs, NSA_GROUPS),
                s5(new_win(cache_win_k, s_kw), NSA_GROUPS), s5(new_win(cache_win_v, s_vw), NSA_GROUPS))

    return (y_prompt.reshape(1, seq, D_MODEL), y_sample.reshape(db, ds, D_MODEL)) + p_states + s_states
```
